```python
import math
import jax
import jax.numpy as jnp
from jax import lax
import numpy as np

D_MODEL = 1024
BATCH = 2
SEQ = 8192
DEPTH = 2
DEC_BATCH = 32
DEC_SEQ = 8
PAST_LEN = 16384
PAGE_SIZE = 128

N_EVEN = (DEPTH + 1) // 2
N_ODD = DEPTH // 2
RMS_EPS = 1e-6
D_FF = 4 * D_MODEL
GLA_HEADS = 4
GLA_DK = D_MODEL // 16
GLA_DV = D_MODEL // 8
GLA_GATE_RANK = 16
GLA_GATE_TEMP = 16.0
GLA_CHUNK = 64
SB_HEADS = 4
SB_HEAD_DIM = D_MODEL // 8
SB_Q_BLOCK = 128
MIX_WIDTH = GLA_HEADS * GLA_DV + SB_HEADS * SB_HEAD_DIM
EVEN_SPLIT_SIZES = (GLA_HEADS * GLA_DK, GLA_HEADS * GLA_DK, GLA_HEADS * GLA_DV, GLA_GATE_RANK,
                    GLA_HEADS * GLA_DV, SB_HEADS * SB_HEAD_DIM, SB_HEADS * SB_HEAD_DIM,
                    SB_HEADS * SB_HEAD_DIM)
EVEN_IN = sum(EVEN_SPLIT_SIZES)
SSD_INNER = 2 * D_MODEL
SSD_HEAD_DIM = 64
SSD_HEADS = SSD_INNER // SSD_HEAD_DIM
SSD_GROUPS = 8
SSD_STATE = 128
SSD_CONV = 4
SSD_CHUNK = 128
SSD_CONV_DIM = SSD_INNER + 2 * SSD_GROUPS * SSD_STATE
SSD_IN = SSD_INNER + SSD_CONV_DIM + SSD_HEADS

kernel_name = 'hybrid_gla_stickbreak_ssd_step'


def _rms(x):
    x32 = x.astype(jnp.float32)
    return x32 * lax.rsqrt(jnp.mean(jnp.square(x32), axis=-1, keepdims=True) + RMS_EPS)


def rms_norm(x, g):
    return (_rms(x) * g.astype(jnp.float32)).astype(x.dtype)


def _pad_seq(t, pad):
    if pad == 0:
        return t
    return jnp.pad(t, [(0, 0), (0, pad)] + [(0, 0)] * (t.ndim - 2))


def _to_chunks(t, n, c):
    return jnp.moveaxis(t.reshape((t.shape[0], n, c) + t.shape[2:]), 1, 0)


def _from_chunks(t, length):
    t = jnp.moveaxis(t, 0, 1)
    return t.reshape((t.shape[0], t.shape[1] * t.shape[2]) + t.shape[3:])[:, :length]


def gla_chunked(q, k, v, log_a, s0):
    length = q.shape[1]
    c = min(GLA_CHUNK, length)
    n = -(-length // c)
    pad = n * c - length
    q, k, v, log_a = (_to_chunks(_pad_seq(t, pad), n, c) for t in (q, k, v, log_a))
    tri = jnp.tril(jnp.ones((c, c), dtype=bool))[None, :, :, None, None]

    def step(s, inp):
        qc, kc, vc, ac = inp
        b = jnp.cumsum(ac, axis=1)
        decay = jnp.exp(jnp.where(tri, b[:, :, None] - b[:, None], -jnp.inf))
        att = jnp.einsum('bthd,bshd,btshd->bhts', qc, kc, decay)
        o = (jnp.einsum('bhts,bshe->bthe', att, vc)
             + jnp.einsum('bthd,bhde->bthe', qc * jnp.exp(b), s))
        last = b[:, -1]
        s_new = (jnp.exp(last)[..., None] * s
                 + jnp.einsum('bshd,bshe->bhde', kc * jnp.exp(last[:, None] - b), vc))
        return s_new, o

    s_fin, o = lax.scan(step, s0.astype(jnp.float32), (q, k, v, log_a))
    return _from_chunks(o, length), s_fin


def sb_attend(q, k, v, logit_bias, q_start):
    bsz, lq, nh, hd = q.shape
    scale = hd ** -0.5
    kpos = jnp.arange(k.shape[1])
    bias = logit_bias.astype(jnp.float32)[None, :, None, None]

    def block(args):
        qb, start = args
        z = jnp.einsum('bqhd,bkhd->bhqk', qb, k).astype(jnp.float32) * scale + bias
        qpos = start + jnp.arange(qb.shape[1])
        mask = kpos[None, :] < qpos[:, None]
        log_1mb = jnp.where(mask, jax.nn.log_sigmoid(-z), 0.0)
        suffix = lax.cumsum(log_1mb, axis=3, reverse=True) - log_1mb
        a = jnp.where(mask, jnp.exp(jax.nn.log_sigmoid(z) + suffix), 0.0)
        return jnp.einsum('bhqk,bkhd->bqhd', a.astype(v.dtype), v)

    if lq > SB_Q_BLOCK and lq % SB_Q_BLOCK == 0:
        nb = lq // SB_Q_BLOCK
        qb = jnp.swapaxes(q.reshape(bsz, nb, SB_Q_BLOCK, nh, hd), 0, 1)
        starts = q_start + jnp.arange(nb, dtype=jnp.int32) * SB_Q_BLOCK
        o = lax.map(block, (qb, starts))
        return jnp.swapaxes(o, 0, 1).reshape(bsz, lq, nh, hd)
    return block((q, jnp.asarray(q_start, jnp.int32)))


def even_mixer(h, past_k, past_v, gla_s0, w_in, w_gate2, b_gate, gla_g, sb_bias, w_out):
    bsz, length, _ = h.shape
    split_at = [int(i) for i in np.cumsum(EVEN_SPLIT_SIZES)[:-1]]
    gq, gk, gv, glr, gr, sq, sk, sv = jnp.split(h @ w_in, split_at, axis=-1)

    def heads(t, nh):
        return t.reshape(bsz, length, nh, -1)

    log_a = jax.nn.log_sigmoid((glr @ w_gate2 + b_gate).astype(jnp.float32)) / GLA_GATE_TEMP
    o_gla, s_new = gla_chunked(heads(gq, GLA_HEADS) * GLA_DK ** -0.5, heads(gk, GLA_HEADS),
                               heads(gv, GLA_HEADS), heads(log_a, GLA_HEADS), gla_s0)
    o_gla = ((_rms(o_gla) * gla_g.astype(jnp.float32)).astype(h.dtype)
             * jax.nn.silu(heads(gr, GLA_HEADS)))
    sk, sv = heads(sk, SB_HEADS), heads(sv, SB_HEADS)
    k_all = jnp.concatenate([past_k.astype(sk.dtype), sk], axis=1)
    v_all = jnp.concatenate([past_v.astype(sv.dtype), sv], axis=1)
    o_sb = sb_attend(heads(sq, SB_HEADS), k_all, v_all, sb_bias, k_all.shape[1] - length)
    mixed = jnp.concatenate([o_gla.reshape(bsz, length, -1),
                             o_sb.reshape(bsz, length, -1)], axis=-1)
    return mixed @ w_out, sk, sv, s_new.astype(gla_s0.dtype)


def ssd_chunked(x, dt, a, bm, cm, h0):
    bsz, length, nh, hp = x.shape
    ng, ns = bm.shape[2], bm.shape[3]
    hpg = nh // ng
    c = min(SSD_CHUNK, length)
    n = -(-length // c)
    pad = n * c - length
    x, dt, bm, cm = (_to_chunks(_pad_seq(t, pad), n, c) for t in (x, dt, bm, cm))
    tri = jnp.tril(jnp.ones((c, c), dtype=bool))[None, :, :, None]

    def step(h, inp):
        xc, dtc, bc, cc = inp
        cum = jnp.cumsum(dtc * a, axis=1)
        w = jnp.exp(jnp.where(tri, cum[:, :, None] - cum[:, None], -jnp.inf)) * dtc[:, None]
        cb = jnp.einsum('btgn,bsgn->btsg', cc, bc)
        w = w.reshape(bsz, c, c, ng, hpg) * cb[..., None]
        xg = xc.reshape(bsz, c, ng, hpg, hp)
        hg = h.reshape(bsz, ng, hpg, hp, ns)
        y = (jnp.einsum('btsgj,bsgjp->btgjp', w, xg)
             + jnp.einsum('btgn,bgjpn->btgjp', cc, hg)
             * jnp.exp(cum).reshape(bsz, c, ng, hpg)[..., None])
        last = cum[:, -1]
        dec = (jnp.exp(last[:, None] - cum) * dtc).reshape(bsz, c, ng, hpg)
        h_new = (jnp.exp(last)[..., None, None] * h
                 + jnp.einsum('bsgn,bsgjp->bgjpn', bc, xg * dec[..., None]).reshape(bsz, nh, hp, ns))
        return h_new, y.reshape(bsz, c, nh, hp)

    h_fin, ys = lax.scan(step, h0.astype(jnp.float32), (x, dt, bm, cm))
    return _from_chunks(ys, length), h_fin


def odd_mixer(h, conv0, h0, w_in, conv_w, conv_b, dt_bias, a_log, d_skip, norm_g, w_out):
    bsz, length, _ = h.shape
    z, xbc, dt_raw = jnp.split(h @ w_in, [SSD_INNER, SSD_INNER + SSD_CONV_DIM], axis=-1)
    xbc_ext = jnp.concatenate([conv0.astype(xbc.dtype), xbc], axis=1)
    conv_new = xbc_ext[:, length:]
    xbc = jax.nn.silu(sum(xbc_ext[:, w:w + length] * conv_w[w] for w in range(SSD_CONV)) + conv_b)
    xs, bm, cm = jnp.split(xbc, [SSD_INNER, SSD_INNER + SSD_GROUPS * SSD_STATE], axis=-1)
    xs = xs.reshape(bsz, length, SSD_HEADS, SSD_HEAD_DIM)
    bm = bm.reshape(bsz, length, SSD_GROUPS, SSD_STATE)
    cm = cm.reshape(bsz, length, SSD_GROUPS, SSD_STATE)
    dt = jax.nn.softplus(dt_raw.astype(jnp.float32) + dt_bias.astype(jnp.float32))
    a = -jnp.exp(a_log.astype(jnp.float32))
    y, h_new = ssd_chunked(xs, dt, a, bm, cm, h0)
    y = (y + xs * d_skip[:, None]).reshape(bsz, length, SSD_INNER) * jax.nn.silu(z)
    y = (_rms(y.reshape(bsz, length, SSD_GROUPS, -1)).reshape(bsz, length, SSD_INNER)
         * norm_g.astype(jnp.float32))
    return y.astype(h.dtype) @ w_out, conv_new, h_new.astype(h0.dtype)


def sq_relu_mlp(h, w_up, w_down):
    return jnp.square(jax.nn.relu(h @ w_up)) @ w_down


def gather_pages(pool, page_table):
    g = jnp.take(pool, page_table, axis=0)
    return g.reshape((g.shape[0], g.shape[1] * g.shape[2]) + g.shape[3:])


def run_trunk(x, sb_past_k, sb_past_v, gla_s0, ssm_h0, conv0, p):
    sb_k_new, sb_v_new, gla_new, ssm_new, conv_new = [], [], [], [], []
    for li in range(DEPTH):
        hn = rms_norm(x, p['norm_mix_pre'][li])
        if li % 2 == 0:
            e = li // 2
            m, k_n, v_n, s_n = even_mixer(hn, sb_past_k[e], sb_past_v[e], gla_s0[e],
                                          p['w_in_even'][e], p['gla_w_gate2'][e],
                                          p['gla_b_gate'][e], p['gla_norm_g'][e],
                                          p['sb_logit_bias'][e], p['w_out_even'][e])
            sb_k_new.append(k_n)
            sb_v_new.append(v_n)
            gla_new.append(s_n)
        else:
            o = li // 2
            m, c_n, h_n = odd_mixer(hn, conv0[o], ssm_h0[o], p['ssd_w_in'][o], p['ssd_conv_w'][o],
                                    p['ssd_conv_b'][o], p['ssd_dt_bias'][o], p['ssd_a_log'][o],
                                    p['ssd_d'][o], p['ssd_norm_g'][o], p['ssd_w_out'][o])
            conv_new.append(c_n)
            ssm_new.append(h_n)
        x = x + rms_norm(m, p['norm_mix_post'][li])
        f = sq_relu_mlp(rms_norm(x, p['norm_mlp_pre'][li]), p['mlp_w_up'][li], p['mlp_w_down'][li])
        x = x + rms_norm(f, p['norm_mlp_post'][li])
    return x, (jnp.stack(sb_k_new), jnp.stack(sb_v_new), jnp.stack(gla_new),
               jnp.stack(ssm_new), jnp.stack(conv_new))


def setup_inputs(seed: int = 0) -> dict:
    key = jax.random.key(seed)
    ks = list(jax.random.split(key, 40))

    def nrm(shape, scale):
        return jax.random.normal(ks.pop(), shape, jnp.float32) * scale

    n_pages = PAST_LEN // PAGE_SIZE
    n_pool = (DEC_BATCH * n_pages * 5 + 3) // 4
    page_table = (jax.random.permutation(ks.pop(), n_pool)[: DEC_BATCH * n_pages]
                  .reshape(DEC_BATCH, n_pages).astype(jnp.int32))
    dt0 = jnp.exp(jax.random.uniform(ks.pop(), (N_ODD, SSD_HEADS), jnp.float32,
                                     minval=math.log(1e-3), maxval=math.log(1e-1)))
    ssd_dt_bias = dt0 + jnp.log(-jnp.expm1(-dt0))
    ssd_a_log = jnp.log(jax.random.uniform(ks.pop(), (N_ODD, SSD_HEADS), jnp.float32,
                                           minval=1.0, maxval=16.0))
    sb_logit_bias = jax.random.uniform(ks.pop(), (N_EVEN, SB_HEADS), jnp.float32,
                                       minval=-9.0, maxval=-6.0)
    return {
        'x_prompt': nrm((BATCH, SEQ, D_MODEL), 1.0),
        'x_sample': nrm((DEC_BATCH, DEC_SEQ, D_MODEL), 1.0),
        'cache_sb_k': nrm((N_EVEN, n_pool, PAGE_SIZE, SB_HEADS, SB_HEAD_DIM), 1.0),
        'cache_sb_v': nrm((N_EVEN, n_pool, PAGE_SIZE, SB_HEADS, SB_HEAD_DIM), 1.0),
        'state_gla': nrm((N_EVEN, DEC_BATCH, GLA_HEADS, GLA_DK, GLA_DV), 0.5),
        'state_ssm': nrm((N_ODD, DEC_BATCH, SSD_HEADS, SSD_HEAD_DIM, SSD_STATE), 0.1),
        'state_conv': nrm((N_ODD, DEC_BATCH, SSD_CONV - 1, SSD_CONV_DIM), 1.0),
        'page_table': page_table,
        'w_in_even': nrm((N_EVEN, D_MODEL, EVEN_IN), D_MODEL ** -0.5),
        'gla_w_gate2': nrm((N_EVEN, GLA_GATE_RANK, GLA_HEADS * GLA_DK), GLA_GATE_RANK ** -0.5),
        'gla_b_gate': nrm((N_EVEN, GLA_HEADS * GLA_DK), 0.1),
        'gla_norm_g': 1.0 + nrm((N_EVEN, GLA_DV), 0.01),
        'sb_logit_bias': sb_logit_bias,
        'w_out_even': nrm((N_EVEN, MIX_WIDTH, D_MODEL), MIX_WIDTH ** -0.5),
        'ssd_w_in': nrm((N_ODD, D_MODEL, SSD_IN), D_MODEL ** -0.5),
        'ssd_conv_w': nrm((N_ODD, SSD_CONV, SSD_CONV_DIM), SSD_CONV ** -0.5),
        'ssd_conv_b': nrm((N_ODD, SSD_CONV_DIM), 0.01),
        'ssd_dt_bias': ssd_dt_bias,
        'ssd_a_log': ssd_a_log,
        'ssd_d': 1.0 + nrm((N_ODD, SSD_HEADS), 0.01),
        'ssd_norm_g': 1.0 + nrm((N_ODD, SSD_INNER), 0.01),
        'ssd_w_out': nrm((N_ODD, SSD_INNER, D_MODEL), SSD_INNER ** -0.5),
        'norm_mix_pre': 1.0 + nrm((DEPTH, D_MODEL), 0.01),
        'norm_mix_post': 1.0 + nrm((DEPTH, D_MODEL), 0.01),
        'norm_mlp_pre': 1.0 + nrm((DEPTH, D_MODEL), 0.01),
        'norm_mlp_post': 1.0 + nrm((DEPTH, D_MODEL), 0.01),
        'mlp_w_up': nrm((DEPTH, D_MODEL, D_FF), D_MODEL ** -0.5),
        'mlp_w_down': nrm((DEPTH, D_FF, D_MODEL), D_FF ** -0.5),
    }


def reference(x_prompt, x_sample, cache_sb_k, cache_sb_v, state_gla, state_ssm, state_conv,
              page_table, w_in_even, gla_w_gate2, gla_b_gate, gla_norm_g, sb_logit_bias,
              w_out_even, ssd_w_in, ssd_conv_w, ssd_conv_b, ssd_dt_bias, ssd_a_log, ssd_d,
              ssd_norm_g, ssd_w_out, norm_mix_pre, norm_mix_post, norm_mlp_pre, norm_mlp_post,
              mlp_w_up, mlp_w_down):
    params = dict(w_in_even=w_in_even, gla_w_gate2=gla_w_gate2, gla_b_gate=gla_b_gate,
                  gla_norm_g=gla_norm_g, sb_logit_bias=sb_logit_bias, w_out_even=w_out_even,
                  ssd_w_in=ssd_w_in, ssd_conv_w=ssd_conv_w, ssd_conv_b=ssd_conv_b,
                  ssd_dt_bias=ssd_dt_bias, ssd_a_log=ssd_a_log, ssd_d=ssd_d,
                  ssd_norm_g=ssd_norm_g, ssd_w_out=ssd_w_out,
                  norm_mix_pre=norm_mix_pre, norm_mix_post=norm_mix_post,
                  norm_mlp_pre=norm_mlp_pre, norm_mlp_post=norm_mlp_post,
                  mlp_w_up=mlp_w_up, mlp_w_down=mlp_w_down)
    bp = x_prompt.shape[0]
    dtp = x_prompt.dtype
    empty = jnp.zeros((bp, 0, SB_HEADS, SB_HEAD_DIM), dtp)
    y_prompt, (sb_k_p, sb_v_p, gla_p, ssm_p, conv_p) = run_trunk(
        x_prompt, [empty] * N_EVEN, [empty] * N_EVEN,
        jnp.zeros((N_EVEN, bp, GLA_HEADS, GLA_DK, GLA_DV), dtp),
        jnp.zeros((N_ODD, bp, SSD_HEADS, SSD_HEAD_DIM, SSD_STATE), dtp),
        jnp.zeros((N_ODD, bp, SSD_CONV - 1, SSD_CONV_DIM), dtp), params)
    past_k = [gather_pages(cache_sb_k[e], page_table) for e in range(N_EVEN)]
    past_v = [gather_pages(cache_sb_v[e], page_table) for e in range(N_EVEN)]
    y_sample, (sb_k_s, sb_v_s, gla_s, ssm_s, conv_s) = run_trunk(
        x_sample, past_k, past_v, state_gla, state_ssm, state_conv, params)
    return (y_prompt, y_sample, sb_k_p, sb_v_p, gla_p, ssm_p, conv_p,
            sb_k_s, sb_v_s, gla_s, ssm_s, conv_s)
```

```python
import functools

import jax
import jax.numpy as jnp
from jax import lax
from jax.experimental import pallas as pl
from jax.experimental.pallas import tpu as pltpu

F32 = jnp.float32
BF16 = jnp.bfloat16

D_MODEL = 1024
RMS_EPS = 1e-6
D_FF = 4 * D_MODEL
GLA_HEADS = 4
GLA_DK = 64
GLA_DV = 128
GLA_GATE_RANK = 16
GLA_GATE_TEMP = 16.0
GLA_QK = GLA_HEADS * GLA_DK
GLA_V = GLA_HEADS * GLA_DV
SB_HEADS = 4
SB_HEAD_DIM = 128
SB_W = SB_HEADS * SB_HEAD_DIM
PAGE_SIZE = 128
EVEN_MAIN = 2 * GLA_QK + 2 * GLA_V + 3 * SB_W
SSD_INNER = 2048
SSD_HEAD_DIM = 64
SSD_HEADS = 32
SSD_GROUPS = 8
SSD_STATE = 128
SSD_CONV = 4
SSD_CONV_DIM = SSD_INNER + 2 * SSD_GROUPS * SSD_STATE
SSD_MAIN = SSD_CONV_DIM + SSD_INNER
LANES = 128
SIDE_W = LANES
VMEM_LIMIT = 48 * 1024 * 1024

NEG_INF = float("-inf")


def _cparams(sem):
    return pltpu.CompilerParams(dimension_semantics=sem, vmem_limit_bytes=VMEM_LIMIT)


def _dot(a, b):
    return jnp.dot(a.astype(BF16), b.astype(BF16), preferred_element_type=F32)


def _dot_nt(a, b):
    return lax.dot_general(a.astype(BF16), b.astype(BF16), (((1,), (1,)), ((), ())),
                           preferred_element_type=F32)


def _dot_tn(a, b):
    return lax.dot_general(a.astype(BF16), b.astype(BF16), (((0,), (0,)), ((), ())),
                           preferred_element_type=F32)


def _split3(x):
    hi = x.astype(BF16)
    r1 = x - hi.astype(F32)
    mid = r1.astype(BF16)
    lo = (r1 - mid.astype(F32)).astype(BF16)
    return hi, mid, lo


def _sel_dot(sel, x):
    hi, mid, lo = _split3(x)
    d = functools.partial(jnp.dot, preferred_element_type=F32)
    return d(sel, hi) + d(sel, mid) + d(sel, lo)


def _dot_sel(x, sel):
    hi, mid, lo = _split3(x)
    d = functools.partial(jnp.dot, preferred_element_type=F32)
    return d(hi, sel) + d(mid, sel) + d(lo, sel)


def _transpose_f32(x):
    n = x.shape[1]
    eye = (lax.broadcasted_iota(jnp.int32, (n, n), 0)
           == lax.broadcasted_iota(jnp.int32, (n, n), 1)).astype(BF16)
    hi, mid, lo = _split3(x)
    d = lambda p: lax.dot_general(eye, p, (((1,), (1,)), ((), ())), preferred_element_type=F32)
    return d(hi) + d(mid) + d(lo)


def _softplus_neg_abs(x):
    return jnp.log1p(jnp.exp(-jnp.abs(x)))


def _log_sigmoid(x):
    return jnp.minimum(x, 0.0) - _softplus_neg_abs(x)


def _softplus(x):
    return jnp.maximum(x, 0.0) + _softplus_neg_abs(x)


def _silu(x):
    return x / (1.0 + jnp.exp(-x))


def _rms_scale(x):
    return x * lax.rsqrt(jnp.mean(x * x, axis=-1, keepdims=True) + RMS_EPS)


def _norm_matmul_kernel(x_ref, g_ref, w_ref, ws_ref, o_ref, os_ref, xn_ref):
    @pl.when(pl.program_id(1) == 0)
    def _():
        xn_ref[...] = (_rms_scale(x_ref[...]) * g_ref[...]).astype(BF16)
        os_ref[...] = jnp.dot(xn_ref[...], ws_ref[...], preferred_element_type=F32)

    o_ref[...] = jnp.dot(xn_ref[...], w_ref[...], preferred_element_type=F32)


def _norm_matmul(x, g, w, w_side, tm, tn):
    m, d = x.shape
    n = w.shape[1]
    return pl.pallas_call(
        _norm_matmul_kernel,
        grid=(m // tm, n // tn),
        in_specs=[pl.BlockSpec((tm, d), lambda i, j: (i, 0)),
                  pl.BlockSpec((1, d), lambda i, j: (0, 0)),
                  pl.BlockSpec((d, tn), lambda i, j: (0, j)),
                  pl.BlockSpec((d, SIDE_W), lambda i, j: (0, 0))],
        out_specs=[pl.BlockSpec((tm, tn), lambda i, j: (i, j)),
                   pl.BlockSpec((tm, SIDE_W), lambda i, j: (i, 0))],
        out_shape=[jax.ShapeDtypeStruct((m, n), F32),
                   jax.ShapeDtypeStruct((m, SIDE_W), F32)],
        scratch_shapes=[pltpu.VMEM((tm, d), BF16)],
        compiler_params=_cparams(("parallel", "arbitrary")),
        name="norm_matmul",
    )(x, g, w, w_side)


def _out_proj_kernel(*refs, widths):
    n = len(widths)
    part_refs, w_ref, x_ref, g_ref, o_ref = refs[:n], refs[n], refs[n + 1], refs[n + 2], refs[n + 3]
    acc = None
    off = 0
    for p_ref, wd in zip(part_refs, widths):
        t = jnp.dot(p_ref[...].astype(BF16), w_ref[off:off + wd, :], preferred_element_type=F32)
        acc = t if acc is None else acc + t
        off += wd
    o_ref[...] = x_ref[...] + _rms_scale(acc) * g_ref[...]


def _out_proj(parts, w, x, g, tm):
    m, d = x.shape
    widths = tuple(p.shape[1] for p in parts)
    in_specs = [pl.BlockSpec((tm, wd), lambda i: (i, 0)) for wd in widths]
    in_specs += [pl.BlockSpec(w.shape, lambda i: (0, 0)),
                 pl.BlockSpec((tm, d), lambda i: (i, 0)),
                 pl.BlockSpec((1, d), lambda i: (0, 0))]
    return pl.pallas_call(
        functools.partial(_out_proj_kernel, widths=widths),
        grid=(m // tm,),
        in_specs=in_specs,
        out_specs=pl.BlockSpec((tm, d), lambda i: (i, 0)),
        out_shape=jax.ShapeDtypeStruct((m, d), F32),
        compiler_params=_cparams(("parallel",)),
        name="out_proj",
    )(*parts, w, x, g)


def _mlp_kernel(x_ref, g1_ref, wu_ref, wd_ref, g2_ref, o_ref, xn_ref, acc_ref):
    f = pl.program_id(1)

    @pl.when(f == 0)
    def _():
        xn_ref[...] = (_rms_scale(x_ref[...]) * g1_ref[...]).astype(BF16)
        acc_ref[...] = jnp.zeros_like(acc_ref)

    h = jnp.dot(xn_ref[...], wu_ref[...], preferred_element_type=F32)
    h = jnp.square(jnp.maximum(h, 0.0))
    acc_ref[...] += jnp.dot(h.astype(BF16), wd_ref[...], preferred_element_type=F32)

    @pl.when(f == pl.num_programs(1) - 1)
    def _():
        o_ref[...] = x_ref[...] + _rms_scale(acc_ref[...]) * g2_ref[...]


def _mlp(x, g1, wu, wd, g2, tm, tf):
    m, d = x.shape
    ff = wu.shape[1]
    return pl.pallas_call(
        _mlp_kernel,
        grid=(m // tm, ff // tf),
        in_specs=[pl.BlockSpec((tm, d), lambda i, f: (i, 0)),
                  pl.BlockSpec((1, d), lambda i, f: (0, 0)),
                  pl.BlockSpec((d, tf), lambda i, f: (0, f)),
                  pl.BlockSpec((tf, d), lambda i, f: (f, 0)),
                  pl.BlockSpec((1, d), lambda i, f: (0, 0))],
        out_specs=pl.BlockSpec((tm, d), lambda i, f: (i, 0)),
        out_shape=jax.ShapeDtypeStruct((m, d), F32),
        scratch_shapes=[pltpu.VMEM((tm, d), BF16), pltpu.VMEM((tm, d), F32)],
        compiler_params=_cparams(("parallel", "arbitrary")),
        name="mlp",
    )(x, g1, wu, wd, g2)


def _gla_kernel(q_ref, k_ref, v_ref, r_ref, glr_ref, w2_ref, bg_ref, g_ref, s0_ref,
                o_ref, s_ref,
                st_ref, qs_ref, b_ref, lat_ref, kdt_ref, vb_ref, *, sub):
    tb = pl.program_id(1)
    t_rows = q_ref.shape[0]
    nsub = t_rows // sub

    @pl.when(tb == 0)
    def _():
        st_ref[...] = s0_ref[0]

    la = _log_sigmoid(_dot(glr_ref[...], w2_ref[...]) + bg_ref[...]) * (1.0 / GLA_GATE_TEMP)
    row = lax.broadcasted_iota(jnp.int32, (t_rows, t_rows), 0)
    col = lax.broadcasted_iota(jnp.int32, (t_rows, t_rows), 1)
    same = (row // sub) == (col // sub)
    tri = jnp.where(same & (col <= row), 1.0, 0.0).astype(BF16)
    blk = jnp.where(same, 1.0, 0.0).astype(BF16)
    b = _sel_dot(tri, la)
    b_end = _sel_dot(blk, la)
    b_ref[...] = b
    qs_ref[...] = q_ref[...] * (GLA_DK ** -0.5)
    kd = k_ref[...] * jnp.exp(b_end - b)
    kdt_ref[...] = _transpose_f32(kd)
    lat_ref[...] = _transpose_f32(la)
    vb_ref[...] = v_ref[...].astype(BF16)

    er = lax.broadcasted_iota(jnp.int32, (GLA_QK, GLA_V), 0) // GLA_DK
    ec = lax.broadcasted_iota(jnp.int32, (GLA_QK, GLA_V), 1) // GLA_DV
    expand = jnp.where(er == ec, 1.0, 0.0).astype(BF16)
    t_iota = lax.broadcasted_iota(jnp.int32, (sub, GLA_QK), 0)
    zeros_half = jnp.zeros((GLA_DK, GLA_DV), F32)

    def body(i, carry):
        r0 = pl.multiple_of(i * sub, sub)
        rows = pl.ds(r0, sub)
        b_i = b_ref[rows, :]
        q_i = qs_ref[rows, :]
        k_i = k_ref[rows, :]
        v_i = v_ref[rows, :]
        in_sub = jnp.where(lax.broadcasted_iota(jnp.int32, (t_rows, LANES), 0) // sub == i,
                           1.0, 0.0).astype(BF16)
        dfull = jnp.exp(_dot_sel(lat_ref[...], in_sub))
        colmask = jnp.where(lax.broadcasted_iota(jnp.int32, (1, t_rows), 1) // sub == i, 1.0, 0.0)
        qe = q_i * jnp.exp(b_i)
        inter = []
        for h in range(GLA_HEADS):
            s_h = st_ref[h]
            pair = slice((h // 2) * LANES, (h // 2 + 1) * LANES)
            w_h = jnp.concatenate([s_h, zeros_half] if h % 2 == 0 else [zeros_half, s_h], axis=0)
            inter.append(_dot(qe[:, pair], w_h))
            hd = slice(h * GLA_DK, (h + 1) * GLA_DK)
            u_h = jnp.dot((kdt_ref[hd, :] * colmask).astype(BF16),
                          vb_ref[:, h * GLA_DV:(h + 1) * GLA_DV], preferred_element_type=F32)
            st_ref[h] = s_h * dfull[hd, :] + u_h
        prods = []
        for j in range(sub):
            e = jnp.exp(jnp.where(t_iota >= j, b_i - b_i[j:j + 1, :], NEG_INF))
            prods.append(q_i * e * k_i[j:j + 1, :])
        p_all = jnp.concatenate(prods, axis=0)
        r_all = _dot(p_all, expand)
        o = jnp.concatenate(inter, axis=1)
        for j in range(sub):
            o = o + r_all[j * sub:(j + 1) * sub, :] * v_i[j:j + 1, :]
        r_i = r_ref[rows, :]
        outs = []
        for h in range(GLA_HEADS):
            hv = slice(h * GLA_DV, (h + 1) * GLA_DV)
            outs.append(_rms_scale(o[:, hv]) * g_ref[...] * _silu(r_i[:, hv]))
        o_ref[rows, :] = jnp.concatenate(outs, axis=1)
        return carry

    lax.fori_loop(0, nsub, body, 0)

    @pl.when(tb == pl.num_programs(1) - 1)
    def _():
        s_ref[0] = st_ref[...]


def _gla(proj, side, w2, bg, g, s0, bsz, length, t_rows, sub):
    nt = length // t_rows
    m = bsz * length
    row = lambda b, t: b * nt + t
    return pl.pallas_call(
        functools.partial(_gla_kernel, sub=sub),
        grid=(bsz, nt),
        in_specs=[pl.BlockSpec((t_rows, GLA_QK), lambda b, t: (row(b, t), 0)),
                  pl.BlockSpec((t_rows, GLA_QK), lambda b, t: (row(b, t), 1)),
                  pl.BlockSpec((t_rows, GLA_V), lambda b, t: (row(b, t), 1)),
                  pl.BlockSpec((t_rows, GLA_V), lambda b, t: (row(b, t), 2)),
                  pl.BlockSpec((t_rows, SIDE_W), lambda b, t: (row(b, t), 0)),
                  pl.BlockSpec((SIDE_W, GLA_QK), lambda b, t: (0, 0)),
                  pl.BlockSpec((1, GLA_QK), lambda b, t: (0, 0)),
                  pl.BlockSpec((1, GLA_DV), lambda b, t: (0, 0)),
                  pl.BlockSpec((1, GLA_HEADS, GLA_DK, GLA_DV), lambda b, t: (b, 0, 0, 0))],
        out_specs=[pl.BlockSpec((t_rows, GLA_V), lambda b, t: (row(b, t), 0)),
                   pl.BlockSpec((1, GLA_HEADS, GLA_DK, GLA_DV), lambda b, t: (b, 0, 0, 0))],
        out_shape=[jax.ShapeDtypeStruct((m, GLA_V), F32),
                   jax.ShapeDtypeStruct((bsz, GLA_HEADS, GLA_DK, GLA_DV), F32)],
        scratch_shapes=[pltpu.VMEM((GLA_HEADS, GLA_DK, GLA_DV), F32),
                        pltpu.VMEM((t_rows, GLA_QK), F32),
                        pltpu.VMEM((t_rows, GLA_QK), F32),
                        pltpu.VMEM((GLA_QK, t_rows), F32),
                        pltpu.VMEM((GLA_QK, t_rows), F32),
                        pltpu.VMEM((t_rows, GLA_V), BF16)],
        compiler_params=_cparams(("parallel", "arbitrary")),
        name="gla",
    )(proj, proj, proj, proj, side, w2, bg, g, s0)


def _sb_tile(q_bf, k_blk, v_blk, bias, carry, acc, mask):
    tk = k_blk.shape[0]
    z = _dot_nt(q_bf, k_blk) * (SB_HEAD_DIM ** -0.5) + bias
    sp = _softplus_neg_abs(z)
    log_b = jnp.minimum(z, 0.0) - sp
    log_1mb = -jnp.maximum(z, 0.0) - sp
    if mask is not None:
        log_1mb = jnp.where(mask, log_1mb, 0.0)
    jr = lax.broadcasted_iota(jnp.int32, (tk, 2 * tk), 0)
    jc = lax.broadcasted_iota(jnp.int32, (tk, 2 * tk), 1)
    sums = _dot(log_1mb, jnp.where((jr > jc) | (jc >= tk), 1.0, 0.0).astype(BF16))
    a = jnp.exp(log_b + sums[:, :tk] + carry)
    if mask is not None:
        a = jnp.where(mask, a, 0.0)
    return carry + sums[:, tk:], acc + _dot(a, v_blk)


def _sb_prompt_kernel(bias_ref, q_ref, k_ref, v_ref, o_ref, *, tk):
    h = pl.program_id(1)
    qi = pl.program_id(2)
    tq = q_ref.shape[0]
    bias = bias_ref[h]
    q_bf = q_ref[...].astype(BF16)
    q0 = qi * tq
    qpos = q0 + lax.broadcasted_iota(jnp.int32, (tq, tk), 0)
    kof = lax.broadcasted_iota(jnp.int32, (tq, tk), 1)
    nkb = (q0 + tq) // tk

    def body(it, st):
        carry, acc = st
        k0 = pl.multiple_of((nkb - 1 - it) * tk, tk)
        mask = (k0 + kof) < qpos
        return _sb_tile(q_bf, k_ref[pl.ds(k0, tk), :], v_ref[pl.ds(k0, tk), :], bias, carry, acc, mask)

    zero = jnp.zeros((tq, tk), F32)
    _, acc = lax.fori_loop(0, nkb, body, (zero, zero))
    o_ref[...] = acc


def _sb_prompt(proj, bias, bsz, length, tq, tk):
    m = bsz * length
    nq = length // tq
    qcol, kcol, vcol = (2 * GLA_QK + 2 * GLA_V) // SB_HEAD_DIM, 0, 0
    kcol = qcol + SB_HEADS
    vcol = kcol + SB_HEADS
    return pl.pallas_call(
        functools.partial(_sb_prompt_kernel, tk=tk),
        grid=(bsz, SB_HEADS, nq),
        in_specs=[pl.BlockSpec(memory_space=pltpu.SMEM),
                  pl.BlockSpec((tq, SB_HEAD_DIM), lambda b, h, i: (b * nq + i, qcol + h)),
                  pl.BlockSpec((length, SB_HEAD_DIM), lambda b, h, i: (b, kcol + h)),
                  pl.BlockSpec((length, SB_HEAD_DIM), lambda b, h, i: (b, vcol + h))],
        out_specs=pl.BlockSpec((tq, SB_HEAD_DIM), lambda b, h, i: (b * nq + i, h)),
        out_shape=jax.ShapeDtypeStruct((m, SB_W), F32),
        compiler_params=_cparams(("parallel", "parallel", "arbitrary")),
        name="sb_prompt",
    )(bias, proj, proj, proj)


def _sb_paged_kernel(pt_ref, bias_ref, q_ref, kn_ref, vn_ref, *refs, pages_per_step):
    g = pages_per_step
    k_refs, v_refs = refs[:g], refs[g:2 * g]
    o_ref, qbd_ref, carry_ref, acc_ref, kpad_ref, vpad_ref = refs[2 * g:]
    j = pl.program_id(1)
    lq = q_ref.shape[0]
    nrow = SB_HEADS * lq
    rhead = lax.broadcasted_iota(jnp.int32, (nrow, LANES), 0) // lq
    bias = jnp.zeros((nrow, LANES), F32)
    for h in range(SB_HEADS):
        bias = jnp.where(rhead == h, bias_ref[h], bias)

    @pl.when(j == 0)
    def _():
        q = q_ref[...]
        lane_head = lax.broadcasted_iota(jnp.int32, (lq, SB_W), 1) // SB_HEAD_DIM
        qbd = jnp.concatenate([jnp.where(lane_head == h, q, 0.0) for h in range(SB_HEADS)], axis=0)
        qbd_ref[...] = qbd.astype(BF16)
        kpad_ref[...] = jnp.zeros_like(kpad_ref)
        vpad_ref[...] = jnp.zeros_like(vpad_ref)
        kpad_ref[0:lq, :] = kn_ref[...]
        vpad_ref[0:lq, :] = vn_ref[...]
        qidx = lax.broadcasted_iota(jnp.int32, (nrow, PAGE_SIZE), 0) % lq
        kidx = lax.broadcasted_iota(jnp.int32, (nrow, PAGE_SIZE), 1)
        zero = jnp.zeros((nrow, PAGE_SIZE), F32)
        carry, acc = _sb_tile(qbd_ref[...], kpad_ref[...], vpad_ref[...], bias, zero,
                              jnp.zeros((nrow, SB_W), F32), kidx < qidx)
        carry_ref[...] = carry
        acc_ref[...] = acc

    carry, acc = carry_ref[...], acc_ref[...]
    for p in range(g):
        carry, acc = _sb_tile(qbd_ref[...], k_refs[p][0], v_refs[p][0], bias, carry, acc, None)
    carry_ref[...] = carry
    acc_ref[...] = acc

    @pl.when(j == pl.num_programs(1) - 1)
    def _():
        for h in range(SB_HEADS):
            hs = slice(h * SB_HEAD_DIM, (h + 1) * SB_HEAD_DIM)
            o_ref[:, hs] = acc[h * lq:(h + 1) * lq, hs]


def _sb_paged(proj, bias, pool_k, pool_v, page_table, bsz, lq, pages_per_step):
    n_pages = page_table.shape[1]
    g = pages_per_step
    nsteps = n_pages // g
    qcol = (2 * GLA_QK + 2 * GLA_V) // SB_W

    def page_spec(p):
        return pl.BlockSpec((1, PAGE_SIZE, SB_W),
                            lambda b, j, pt: (pt[b, n_pages - 1 - (j * g + p)], 0, 0))

    grid_spec = pltpu.PrefetchScalarGridSpec(
        num_scalar_prefetch=1,
        grid=(bsz, nsteps),
        in_specs=[pl.BlockSpec(memory_space=pltpu.SMEM),
                  pl.BlockSpec((lq, SB_W), lambda b, j, pt: (b, qcol)),
                  pl.BlockSpec((lq, SB_W), lambda b, j, pt: (b, qcol + 1)),
                  pl.BlockSpec((lq, SB_W), lambda b, j, pt: (b, qcol + 2))]
        + [page_spec(p) for p in range(g)] + [page_spec(p) for p in range(g)],
        out_specs=pl.BlockSpec((lq, SB_W), lambda b, j, pt: (b, 0)),
        scratch_shapes=[pltpu.VMEM((SB_HEADS * lq, SB_W), BF16),
                        pltpu.VMEM((SB_HEADS * lq, PAGE_SIZE), F32),
                        pltpu.VMEM((SB_HEADS * lq, SB_W), F32),
                        pltpu.VMEM((PAGE_SIZE, SB_W), F32),
                        pltpu.VMEM((PAGE_SIZE, SB_W), F32)])
    return pl.pallas_call(
        functools.partial(_sb_paged_kernel, pages_per_step=g),
        grid_spec=grid_spec,
        out_shape=jax.ShapeDtypeStruct((bsz * lq, SB_W), F32),
        compiler_params=_cparams(("parallel", "arbitrary")),
        name="sb_paged",
    )(page_table, bias, proj, proj, proj, *([pool_k] * g), *([pool_v] * g))


def _ssd_kernel(xbc_ref, z_ref, dtr_ref, cw_ref, cb_ref, dtb_ref, alog_ref, dsk_ref, ng_ref,
                conv0_ref, h0_ref, y_ref, hout_ref, xext_ref, hs_ref):
    ci = pl.program_id(1)
    c = xbc_ref.shape[0]
    tail = SSD_CONV - 1
    pad = 8

    @pl.when(ci == 0)
    def _():
        xext_ref[0:pad, :] = conv0_ref[0]
        hs_ref[...] = h0_ref[0]

    xext_ref[pad:pad + c, :] = xbc_ref[...]
    acc = cb_ref[...]
    for w in range(SSD_CONV):
        acc = acc + xext_ref[pl.ds(pad - tail + w, c), :] * cw_ref[w:w + 1, :]
    xc = _silu(acc)
    xext_ref[0:pad, :] = xext_ref[c:c + pad, :]

    dt = _softplus(dtr_ref[...] + dtb_ref[...])
    a = -jnp.exp(alog_ref[...])
    row = lax.broadcasted_iota(jnp.int32, (c, c), 0)
    col = lax.broadcasted_iota(jnp.int32, (c, c), 1)
    causal = col <= row
    cum = _sel_dot(jnp.where(causal, 1.0, 0.0).astype(BF16), dt * a)
    cum_t = _transpose_f32(cum)
    dt_t = _transpose_f32(dt)
    last = cum[c - 1:c, :]
    e_cum = jnp.exp(cum)
    dec = jnp.exp(last - cum) * dt
    e_last_t = jnp.exp(cum_t[:, c - 1:c])
    lo_lanes = lax.broadcasted_iota(jnp.int32, (c, LANES), 1) < SSD_HEAD_DIM
    lo_rows = lax.broadcasted_iota(jnp.int32, (LANES, LANES), 0) < SSD_HEAD_DIM
    hpg = SSD_HEADS // SSD_GROUPS

    for g in range(SSD_GROUPS):
        b_g = xc[:, SSD_INNER + g * SSD_STATE:SSD_INNER + (g + 1) * SSD_STATE]
        c_g = xc[:, SSD_INNER + (SSD_GROUPS + g) * SSD_STATE:SSD_INNER + (SSD_GROUPS + g + 1) * SSD_STATE]
        cb = _dot_nt(c_g, b_g)
        for pp in range(hpg // 2):
            pair = g * (hpg // 2) + pp
            h0, h1 = 2 * pair, 2 * pair + 1
            x_pair = xc[:, pair * LANES:(pair + 1) * LANES]
            hs = hs_ref[pair]
            y_pair = _dot_nt(c_g, hs) * jnp.where(lo_lanes, e_cum[:, h0:h0 + 1], e_cum[:, h1:h1 + 1])
            for hh, keep in ((h0, lo_lanes), (h1, jnp.logical_not(lo_lanes))):
                lmat = jnp.exp(jnp.where(causal, cum[:, hh:hh + 1] - cum_t[hh:hh + 1, :], NEG_INF))
                w = lmat * dt_t[hh:hh + 1, :] * cb
                y_pair = y_pair + _dot(w, jnp.where(keep, x_pair, 0.0))
            xd = x_pair * jnp.where(lo_lanes, dec[:, h0:h0 + 1], dec[:, h1:h1 + 1])
            scale = jnp.where(lo_rows, e_last_t[h0:h0 + 1, :], e_last_t[h1:h1 + 1, :])
            hs_ref[pair] = scale * hs + _dot_tn(xd, b_g)
            y_pair = y_pair + x_pair * dsk_ref[:, pair * LANES:(pair + 1) * LANES]
            zz = z_ref[:, pair * LANES:(pair + 1) * LANES]
            y_ref[:, pair * LANES:(pair + 1) * LANES] = y_pair * _silu(zz)

    gw = SSD_INNER // SSD_GROUPS
    for g in range(SSD_GROUPS):
        gs = slice(g * gw, (g + 1) * gw)
        y_ref[:, gs] = _rms_scale(y_ref[:, gs]) * ng_ref[:, gs]

    @pl.when(ci == pl.num_programs(1) - 1)
    def _():
        hout_ref[0] = hs_ref[...]


def _ssd(zx, side, cw, cb, dtb, alog, dsk, ng, conv0, h0, bsz, length, chunk):
    nc = length // chunk
    m = bsz * length
    row = lambda b, c: b * nc + c
    const = lambda b, c: (0, 0)
    npair = SSD_HEADS // 2
    return pl.pallas_call(
        _ssd_kernel,
        grid=(bsz, nc),
        in_specs=[pl.BlockSpec((chunk, SSD_CONV_DIM), lambda b, c: (row(b, c), 0)),
                  pl.BlockSpec((chunk, SSD_INNER), lambda b, c: (row(b, c), SSD_CONV_DIM // SSD_INNER)),
                  pl.BlockSpec((chunk, SIDE_W), lambda b, c: (row(b, c), 0)),
                  pl.BlockSpec((SSD_CONV, SSD_CONV_DIM), const),
                  pl.BlockSpec((1, SSD_CONV_DIM), const),
                  pl.BlockSpec((1, SIDE_W), const),
                  pl.BlockSpec((1, SIDE_W), const),
                  pl.BlockSpec((1, SSD_INNER), const),
                  pl.BlockSpec((1, SSD_INNER), const),
                  pl.BlockSpec((1, 8, SSD_CONV_DIM), lambda b, c: (b, 0, 0)),
                  pl.BlockSpec((1, npair, LANES, SSD_STATE), lambda b, c: (b, 0, 0, 0))],
        out_specs=[pl.BlockSpec((chunk, SSD_INNER), lambda b, c: (row(b, c), 0)),
                   pl.BlockSpec((1, npair, LANES, SSD_STATE), lambda b, c: (b, 0, 0, 0))],
        out_shape=[jax.ShapeDtypeStruct((m, SSD_INNER), F32),
                   jax.ShapeDtypeStruct((bsz, npair, LANES, SSD_STATE), F32)],
        scratch_shapes=[pltpu.VMEM((chunk + 8, SSD_CONV_DIM), F32),
                        pltpu.VMEM((npair, LANES, SSD_STATE), F32)],
        compiler_params=_cparams(("parallel", "arbitrary")),
        name="ssd",
    )(zx, zx, side, cw, cb, dtb, alog, dsk, ng, conv0, h0)


def _pad_cols(w, width):
    return jnp.pad(w, ((0, 0), (0, width - w.shape[1])))


def _row(v, width=None):
    v = v.reshape(1, -1).astype(F32)
    return v if width is None else _pad_cols(v, width)


def _prep_params(p):
    w_in = p["w_in_even"][0]
    n_gla = 2 * GLA_QK + GLA_V
    q = {}
    q["even_main"] = jnp.concatenate([w_in[:, :n_gla], w_in[:, n_gla + GLA_GATE_RANK:]], axis=1).astype(BF16)
    q["even_side"] = _pad_cols(w_in[:, n_gla:n_gla + GLA_GATE_RANK], SIDE_W).astype(BF16)
    q["gate_w2"] = jnp.pad(p["gla_w_gate2"][0], ((0, SIDE_W - GLA_GATE_RANK), (0, 0))).astype(BF16)
    q["gate_b"] = _row(p["gla_b_gate"][0])
    q["gla_g"] = _row(p["gla_norm_g"][0])
    q["sb_bias"] = p["sb_logit_bias"][0].astype(F32)
    q["even_out"] = p["w_out_even"][0].astype(BF16)
    w_ssd = p["ssd_w_in"][0]
    q["ssd_main"] = jnp.concatenate([w_ssd[:, SSD_INNER:SSD_INNER + SSD_CONV_DIM], w_ssd[:, :SSD_INNER]],
                                    axis=1).astype(BF16)
    q["ssd_side"] = _pad_cols(w_ssd[:, SSD_INNER + SSD_CONV_DIM:], SIDE_W).astype(BF16)
    q["conv_w"] = p["ssd_conv_w"][0].astype(F32)
    q["conv_b"] = _row(p["ssd_conv_b"][0])
    q["dt_bias"] = _row(p["ssd_dt_bias"][0], SIDE_W)
    q["a_log"] = _row(p["ssd_a_log"][0], SIDE_W)
    q["d_skip"] = _row(jnp.repeat(p["ssd_d"][0], SSD_HEAD_DIM))
    q["ssd_g"] = _row(p["ssd_norm_g"][0])
    q["ssd_out"] = p["ssd_w_out"][0].astype(BF16)
    for name in ("norm_mix_pre", "norm_mix_post", "norm_mlp_pre", "norm_mlp_post"):
        q[name] = [_row(p[name][li]) for li in range(2)]
    q["mlp_up"] = [p["mlp_w_up"][li].astype(BF16) for li in range(2)]
    q["mlp_down"] = [p["mlp_w_down"][li].astype(BF16) for li in range(2)]
    return q


def _trunk(x, paged, gla_s0, ssm_h0, conv0, q, cfg):
    bsz, length, d = x.shape
    m = bsz * length
    tm = cfg["tm"]
    x2 = x.reshape(m, d)

    proj, side = _norm_matmul(x2, q["norm_mix_pre"][0], q["even_main"], q["even_side"], tm, 512)
    o_gla, gla_new = _gla(proj, side, q["gate_w2"], q["gate_b"], q["gla_g"], gla_s0,
                          bsz, length, cfg["gla_rows"], cfg["gla_sub"])
    if paged is None:
        o_sb = _sb_prompt(proj, q["sb_bias"], bsz, length, cfg["sb_tq"], PAGE_SIZE)
    else:
        o_sb = _sb_paged(proj, q["sb_bias"], paged[0], paged[1], paged[2], bsz, length, cfg["sb_pages"])
    kcol = 2 * GLA_QK + 2 * GLA_V + SB_W
    sb_k = proj[:, kcol:kcol + SB_W].reshape(bsz, length, SB_HEADS, SB_HEAD_DIM)
    sb_v = proj[:, kcol + SB_W:kcol + 2 * SB_W].reshape(bsz, length, SB_HEADS, SB_HEAD_DIM)
    x2 = _out_proj([o_gla, o_sb], q["even_out"], x2, q["norm_mix_post"][0], tm)
    x2 = _mlp(x2, q["norm_mlp_pre"][0], q["mlp_up"][0], q["mlp_down"][0], q["norm_mlp_post"][0], tm, 1024)

    zx, dtr = _norm_matmul(x2, q["norm_mix_pre"][1], q["ssd_main"], q["ssd_side"], tm, 512)
    conv_pad = jnp.pad(conv0, ((0, 0), (8 - (SSD_CONV - 1), 0), (0, 0)))
    h0 = ssm_h0.reshape(bsz, SSD_HEADS // 2, 2 * SSD_HEAD_DIM, SSD_STATE)
    y, h_new = _ssd(zx, dtr, q["conv_w"], q["conv_b"], q["dt_bias"], q["a_log"], q["d_skip"], q["ssd_g"],
                    conv_pad, h0, bsz, length, cfg["ssd_chunk"])
    conv_new = zx.reshape(bsz, length, SSD_MAIN)[:, length - (SSD_CONV - 1):, :SSD_CONV_DIM]
    x2 = _out_proj([y], q["ssd_out"], x2, q["norm_mix_post"][1], tm)
    x2 = _mlp(x2, q["norm_mlp_pre"][1], q["mlp_up"][1], q["mlp_down"][1], q["norm_mlp_post"][1], tm, 1024)

    return (x2.reshape(bsz, length, d), sb_k[None], sb_v[None], gla_new[None],
            h_new.reshape(bsz, SSD_HEADS, SSD_HEAD_DIM, SSD_STATE)[None], conv_new[None])


def _prompt_cfg(length):
    return dict(tm=min(512, 2 * length), gla_rows=min(256, length), gla_sub=16, sb_tq=min(256, length),
                ssd_chunk=min(128, length))


def _sample_cfg(bsz, length, n_pages):
    return dict(tm=bsz * length, gla_rows=length, gla_sub=length, sb_pages=min(8, n_pages), ssd_chunk=length)


def kernel(x_prompt, x_sample, cache_sb_k, cache_sb_v, state_gla, state_ssm, state_conv, page_table, w_in_even, gla_w_gate2, gla_b_gate, gla_norm_g, sb_logit_bias, w_out_even, ssd_w_in, ssd_conv_w, ssd_conv_b, ssd_dt_bias, ssd_a_log, ssd_d, ssd_norm_g, ssd_w_out, norm_mix_pre, norm_mix_post, norm_mlp_pre, norm_mlp_post, mlp_w_up, mlp_w_down):
    q = _prep_params(dict(
        w_in_even=w_in_even, gla_w_gate2=gla_w_gate2, gla_b_gate=gla_b_gate, gla_norm_g=gla_norm_g,
        sb_logit_bias=sb_logit_bias, w_out_even=w_out_even, ssd_w_in=ssd_w_in, ssd_conv_w=ssd_conv_w,
        ssd_conv_b=ssd_conv_b, ssd_dt_bias=ssd_dt_bias, ssd_a_log=ssd_a_log, ssd_d=ssd_d,
        ssd_norm_g=ssd_norm_g, ssd_w_out=ssd_w_out, norm_mix_pre=norm_mix_pre, norm_mix_post=norm_mix_post,
        norm_mlp_pre=norm_mlp_pre, norm_mlp_post=norm_mlp_post, mlp_w_up=mlp_w_up, mlp_w_down=mlp_w_down))
    bp, lp, _ = x_prompt.shape
    bs, ls, _ = x_sample.shape
    zeros = lambda *s: jnp.zeros(s, F32)
    prompt = _trunk(x_prompt, None, zeros(bp, GLA_HEADS, GLA_DK, GLA_DV),
                    zeros(bp, SSD_HEADS, SSD_HEAD_DIM, SSD_STATE), zeros(bp, SSD_CONV - 1, SSD_CONV_DIM),
                    q, _prompt_cfg(lp))
    n_pool = cache_sb_k.shape[1]
    pool_k = cache_sb_k[0].reshape(n_pool, PAGE_SIZE, SB_W)
    pool_v = cache_sb_v[0].reshape(n_pool, PAGE_SIZE, SB_W)
    sample = _trunk(x_sample, (pool_k, pool_v, page_table), state_gla[0], state_ssm[0], state_conv[0],
                    q, _sample_cfg(bs, ls, page_table.shape[1]))
    return (prompt[0], sample[0]) + prompt[1:] + sample[1:]
```

```python
import functools

import jax
import jax.numpy as jnp
from jax import lax
from jax.experimental import pallas as pl
from jax.experimental.pallas import tpu as pltpu

F32 = jnp.float32
BF16 = jnp.bfloat16

D_MODEL = 1024
RMS_EPS = 1e-6
D_FF = 4 * D_MODEL
GLA_HEADS = 4
GLA_DK = 64
GLA_DV = 128
GLA_GATE_RANK = 16
GLA_GATE_TEMP = 16.0
GLA_QK = GLA_HEADS * GLA_DK
GLA_V = GLA_HEADS * GLA_DV
SB_HEADS = 4
SB_HEAD_DIM = 128
SB_W = SB_HEADS * SB_HEAD_DIM
PAGE_SIZE = 128
EVEN_MAIN = 2 * GLA_QK + 2 * GLA_V + 3 * SB_W
SSD_INNER = 2048
SSD_HEAD_DIM = 64
SSD_HEADS = 32
SSD_GROUPS = 8
SSD_STATE = 128
SSD_CONV = 4
SSD_CONV_DIM = SSD_INNER + 2 * SSD_GROUPS * SSD_STATE
SSD_MAIN = SSD_CONV_DIM + SSD_INNER
LANES = 128
SIDE_W = LANES
VMEM_LIMIT = 48 * 1024 * 1024

NEG_INF = float("-inf")


def _cparams(sem):
    return pltpu.CompilerParams(dimension_semantics=sem, vmem_limit_bytes=VMEM_LIMIT)


def _dot(a, b):
    return jnp.dot(a.astype(BF16), b.astype(BF16), preferred_element_type=F32)


def _dot_nt(a, b):
    return lax.dot_general(a.astype(BF16), b.astype(BF16), (((1,), (1,)), ((), ())),
                           preferred_element_type=F32)


def _dot_tn(a, b):
    return lax.dot_general(a.astype(BF16), b.astype(BF16), (((0,), (0,)), ((), ())),
                           preferred_element_type=F32)


def _split3(x):
    hi = x.astype(BF16)
    r1 = x - hi.astype(F32)
    mid = r1.astype(BF16)
    lo = (r1 - mid.astype(F32)).astype(BF16)
    return hi, mid, lo


def _sel_dot(sel, x):
    hi, mid, lo = _split3(x)
    d = functools.partial(jnp.dot, preferred_element_type=F32)
    return d(sel, hi) + d(sel, mid) + d(sel, lo)


def _dot_sel(x, sel):
    hi, mid, lo = _split3(x)
    d = functools.partial(jnp.dot, preferred_element_type=F32)
    return d(hi, sel) + d(mid, sel) + d(lo, sel)


def _transpose_f32(x):
    n = x.shape[1]
    eye = (lax.broadcasted_iota(jnp.int32, (n, n), 0)
           == lax.broadcasted_iota(jnp.int32, (n, n), 1)).astype(BF16)
    hi, mid, lo = _split3(x)
    d = lambda p: lax.dot_general(eye, p, (((1,), (1,)), ((), ())), preferred_element_type=F32)
    return d(hi) + d(mid) + d(lo)


def _softplus_neg_abs(x):
    return jnp.log1p(jnp.exp(-jnp.abs(x)))


def _log_sigmoid(x):
    return jnp.minimum(x, 0.0) - _softplus_neg_abs(x)


def _softplus(x):
    return jnp.maximum(x, 0.0) + _softplus_neg_abs(x)


def _silu(x):
    return x / (1.0 + jnp.exp(-x))


def _rms_scale(x):
    return x * lax.rsqrt(jnp.mean(x * x, axis=-1, keepdims=True) + RMS_EPS)


def _norm_matmul_kernel(x_ref, g_ref, w_ref, ws_ref, o_ref, os_ref, xn_ref):
    j = pl.program_id(1)
    tn = o_ref.shape[1]

    @pl.when(j == 0)
    def _():
        xn_ref[...] = (_rms_scale(x_ref[...]) * g_ref[...]).astype(BF16)
        os_ref[...] = jnp.dot(xn_ref[...], ws_ref[...], preferred_element_type=F32)

    cols = pl.ds(pl.multiple_of(j * tn, tn), tn)
    o_ref[...] = jnp.dot(xn_ref[...], w_ref[:, cols], preferred_element_type=F32)


def _norm_matmul(x, g, w, w_side, tm, tn):
    m, d = x.shape
    n = w.shape[1]
    once = pl.Buffered(1)
    return pl.pallas_call(
        _norm_matmul_kernel,
        grid=(m // tm, n // tn),
        in_specs=[pl.BlockSpec((tm, d), lambda i, j: (i, 0)),
                  pl.BlockSpec((1, d), lambda i, j: (0, 0)),
                  pl.BlockSpec((d, n), lambda i, j: (0, 0), pipeline_mode=once),
                  pl.BlockSpec((d, SIDE_W), lambda i, j: (0, 0), pipeline_mode=once)],
        out_specs=[pl.BlockSpec((tm, tn), lambda i, j: (i, j)),
                   pl.BlockSpec((tm, SIDE_W), lambda i, j: (i, 0))],
        out_shape=[jax.ShapeDtypeStruct((m, n), F32),
                   jax.ShapeDtypeStruct((m, SIDE_W), F32)],
        scratch_shapes=[pltpu.VMEM((tm, d), BF16)],
        compiler_params=_cparams(("parallel", "arbitrary")),
        name="norm_matmul",
    )(x, g, w, w_side)


def _out_proj_kernel(*refs, widths):
    n = len(widths)
    part_refs, w_ref, x_ref, g_ref, o_ref = refs[:n], refs[n], refs[n + 1], refs[n + 2], refs[n + 3]
    acc = None
    off = 0
    for p_ref, wd in zip(part_refs, widths):
        t = jnp.dot(p_ref[...].astype(BF16), w_ref[off:off + wd, :], preferred_element_type=F32)
        acc = t if acc is None else acc + t
        off += wd
    o_ref[...] = x_ref[...] + _rms_scale(acc) * g_ref[...]


def _out_proj(parts, w, x, g, tm):
    m, d = x.shape
    widths = tuple(p.shape[1] for p in parts)
    in_specs = [pl.BlockSpec((tm, wd), lambda i: (i, 0)) for wd in widths]
    in_specs += [pl.BlockSpec(w.shape, lambda i: (0, 0)),
                 pl.BlockSpec((tm, d), lambda i: (i, 0)),
                 pl.BlockSpec((1, d), lambda i: (0, 0))]
    return pl.pallas_call(
        functools.partial(_out_proj_kernel, widths=widths),
        grid=(m // tm,),
        in_specs=in_specs,
        out_specs=pl.BlockSpec((tm, d), lambda i: (i, 0)),
        out_shape=jax.ShapeDtypeStruct((m, d), F32),
        compiler_params=_cparams(("parallel",)),
        name="out_proj",
    )(*parts, w, x, g)


def _mlp_kernel(x_ref, g1_ref, wu_ref, wd_ref, g2_ref, o_ref, xn_ref, acc_ref):
    f = pl.program_id(1)

    @pl.when(f == 0)
    def _():
        xn_ref[...] = (_rms_scale(x_ref[...]) * g1_ref[...]).astype(BF16)
        acc_ref[...] = jnp.zeros_like(acc_ref)

    h = jnp.dot(xn_ref[...], wu_ref[...], preferred_element_type=F32)
    h = jnp.square(jnp.maximum(h, 0.0))
    acc_ref[...] += jnp.dot(h.astype(BF16), wd_ref[...], preferred_element_type=F32)

    @pl.when(f == pl.num_programs(1) - 1)
    def _():
        o_ref[...] = x_ref[...] + _rms_scale(acc_ref[...]) * g2_ref[...]


def _mlp(x, g1, wu, wd, g2, tm, tf):
    m, d = x.shape
    ff = wu.shape[1]
    return pl.pallas_call(
        _mlp_kernel,
        grid=(m // tm, ff // tf),
        in_specs=[pl.BlockSpec((tm, d), lambda i, f: (i, 0)),
                  pl.BlockSpec((1, d), lambda i, f: (0, 0)),
                  pl.BlockSpec((d, tf), lambda i, f: (0, f)),
                  pl.BlockSpec((tf, d), lambda i, f: (f, 0)),
                  pl.BlockSpec((1, d), lambda i, f: (0, 0))],
        out_specs=pl.BlockSpec((tm, d), lambda i, f: (i, 0)),
        out_shape=jax.ShapeDtypeStruct((m, d), F32),
        scratch_shapes=[pltpu.VMEM((tm, d), BF16), pltpu.VMEM((tm, d), F32)],
        compiler_params=_cparams(("parallel", "arbitrary")),
        name="mlp",
    )(x, g1, wu, wd, g2)


def _gla_kernel(q_ref, k_ref, v_ref, r_ref, glr_ref, w2_ref, bg_ref, g_ref, s0_ref,
                o_ref, s_ref,
                st_ref, qs_ref, b_ref, lat_ref, kdt_ref, vb_ref, *, sub):
    tb = pl.program_id(1)
    t_rows = q_ref.shape[0]
    nsub = t_rows // sub

    @pl.when(tb == 0)
    def _():
        st_ref[...] = s0_ref[0]

    la = _log_sigmoid(_dot(glr_ref[...], w2_ref[...]) + bg_ref[...]) * (1.0 / GLA_GATE_TEMP)
    row = lax.broadcasted_iota(jnp.int32, (t_rows, t_rows), 0)
    col = lax.broadcasted_iota(jnp.int32, (t_rows, t_rows), 1)
    same = (row // sub) == (col // sub)
    tri = jnp.where(same & (col <= row), 1.0, 0.0).astype(BF16)
    blk = jnp.where(same, 1.0, 0.0).astype(BF16)
    b = _sel_dot(tri, la)
    b_end = _sel_dot(blk, la)
    b_ref[...] = b
    qs_ref[...] = q_ref[...] * (GLA_DK ** -0.5)
    kd = k_ref[...] * jnp.exp(b_end - b)
    kdt_ref[...] = _transpose_f32(kd)
    lat_ref[...] = _transpose_f32(la)
    vb_ref[...] = v_ref[...].astype(BF16)

    er = lax.broadcasted_iota(jnp.int32, (GLA_QK, GLA_V), 0) // GLA_DK
    ec = lax.broadcasted_iota(jnp.int32, (GLA_QK, GLA_V), 1) // GLA_DV
    expand = jnp.where(er == ec, 1.0, 0.0).astype(BF16)
    t_iota = lax.broadcasted_iota(jnp.int32, (sub, GLA_QK), 0)
    zeros_half = jnp.zeros((GLA_DK, GLA_DV), F32)

    def body(i, carry):
        r0 = pl.multiple_of(i * sub, sub)
        rows = pl.ds(r0, sub)
        b_i = b_ref[rows, :]
        q_i = qs_ref[rows, :]
        k_i = k_ref[rows, :]
        v_i = v_ref[rows, :]
        in_sub = jnp.where(lax.broadcasted_iota(jnp.int32, (t_rows, LANES), 0) // sub == i,
                           1.0, 0.0).astype(BF16)
        dfull = jnp.exp(_dot_sel(lat_ref[...], in_sub))
        colmask = jnp.where(lax.broadcasted_iota(jnp.int32, (1, t_rows), 1) // sub == i, 1.0, 0.0)
        qe = q_i * jnp.exp(b_i)
        inter = []
        for h in range(GLA_HEADS):
            s_h = st_ref[h]
            pair = slice((h // 2) * LANES, (h // 2 + 1) * LANES)
            w_h = jnp.concatenate([s_h, zeros_half] if h % 2 == 0 else [zeros_half, s_h], axis=0)
            inter.append(_dot(qe[:, pair], w_h))
            hd = slice(h * GLA_DK, (h + 1) * GLA_DK)
            u_h = jnp.dot((kdt_ref[hd, :] * colmask).astype(BF16),
                          vb_ref[:, h * GLA_DV:(h + 1) * GLA_DV], preferred_element_type=F32)
            st_ref[h] = s_h * dfull[hd, :] + u_h
        prods = []
        for j in range(sub):
            e = jnp.exp(jnp.where(t_iota >= j, b_i - b_i[j:j + 1, :], NEG_INF))
            prods.append(q_i * e * k_i[j:j + 1, :])
        p_all = jnp.concatenate(prods, axis=0)
        r_all = _dot(p_all, expand)
        o = jnp.concatenate(inter, axis=1)
        for j in range(sub):
            o = o + r_all[j * sub:(j + 1) * sub, :] * v_i[j:j + 1, :]
        r_i = r_ref[rows, :]
        outs = []
        for h in range(GLA_HEADS):
            hv = slice(h * GLA_DV, (h + 1) * GLA_DV)
            outs.append(_rms_scale(o[:, hv]) * g_ref[...] * _silu(r_i[:, hv]))
        o_ref[rows, :] = jnp.concatenate(outs, axis=1)
        return carry

    lax.fori_loop(0, nsub, body, 0)

    @pl.when(tb == pl.num_programs(1) - 1)
    def _():
        s_ref[0] = st_ref[...]


def _gla(proj, side, w2, bg, g, s0, bsz, length, t_rows, sub):
    nt = length // t_rows
    m = bsz * length
    row = lambda b, t: b * nt + t
    return pl.pallas_call(
        functools.partial(_gla_kernel, sub=sub),
        grid=(bsz, nt),
        in_specs=[pl.BlockSpec((t_rows, GLA_QK), lambda b, t: (row(b, t), 0)),
                  pl.BlockSpec((t_rows, GLA_QK), lambda b, t: (row(b, t), 1)),
                  pl.BlockSpec((t_rows, GLA_V), lambda b, t: (row(b, t), 1)),
                  pl.BlockSpec((t_rows, GLA_V), lambda b, t: (row(b, t), 2)),
                  pl.BlockSpec((t_rows, SIDE_W), lambda b, t: (row(b, t), 0)),
                  pl.BlockSpec((SIDE_W, GLA_QK), lambda b, t: (0, 0)),
                  pl.BlockSpec((1, GLA_QK), lambda b, t: (0, 0)),
                  pl.BlockSpec((1, GLA_DV), lambda b, t: (0, 0)),
                  pl.BlockSpec((1, GLA_HEADS, GLA_DK, GLA_DV), lambda b, t: (b, 0, 0, 0))],
        out_specs=[pl.BlockSpec((t_rows, GLA_V), lambda b, t: (row(b, t), 0)),
                   pl.BlockSpec((1, GLA_HEADS, GLA_DK, GLA_DV), lambda b, t: (b, 0, 0, 0))],
        out_shape=[jax.ShapeDtypeStruct((m, GLA_V), F32),
                   jax.ShapeDtypeStruct((bsz, GLA_HEADS, GLA_DK, GLA_DV), F32)],
        scratch_shapes=[pltpu.VMEM((GLA_HEADS, GLA_DK, GLA_DV), F32),
                        pltpu.VMEM((t_rows, GLA_QK), F32),
                        pltpu.VMEM((t_rows, GLA_QK), F32),
                        pltpu.VMEM((GLA_QK, t_rows), F32),
                        pltpu.VMEM((GLA_QK, t_rows), F32),
                        pltpu.VMEM((t_rows, GLA_V), BF16)],
        compiler_params=_cparams(("parallel", "arbitrary")),
        name="gla",
    )(proj, proj, proj, proj, side, w2, bg, g, s0)


LOG2_E = 1.4426950408889634
SB_LOGIT_SCALE = SB_HEAD_DIM ** -0.5 * LOG2_E


def _suffix_matrix(tk):
    jr = lax.broadcasted_iota(jnp.int32, (tk, 2 * tk), 0)
    jc = lax.broadcasted_iota(jnp.int32, (tk, 2 * tk), 1)
    return jnp.where((jr > jc) | (jc >= tk), 1.0, 0.0).astype(BF16)


def _sb_logits(t, mask):
    n = jnp.maximum(t, 0.0) + jnp.log2(1.0 + jnp.exp2(-jnp.abs(t)))
    log_b = t - n
    if mask is not None:
        n = jnp.where(mask, n, 0.0)
    return log_b, n.astype(BF16)


def _sb_weights(log_b, n, suffix, carry, mask):
    tk = log_b.shape[1]
    sums = jnp.dot(n, suffix, preferred_element_type=F32)
    a = jnp.exp2(log_b - sums[:, :tk] - carry)
    if mask is not None:
        a = jnp.where(mask, a, 0.0)
    return a.astype(BF16), carry + sums[:, tk:]


def _sb_tile(q_bf, k_blk, v_blk, bias2, suffix, carry, acc, mask):
    log_b, n = _sb_logits(_dot_nt(q_bf, k_blk) * SB_LOGIT_SCALE + bias2, mask)
    a, carry = _sb_weights(log_b, n, suffix, carry, mask)
    return carry, acc + jnp.dot(a, v_blk.astype(BF16), preferred_element_type=F32)


def _sb_prompt_kernel(bias_ref, q_ref, k_ref, v_ref, o_ref, lb_ref, n_ref, a_ref, carry_ref, acc_ref,
                      *, tk, nsplit, unroll):
    h = pl.program_id(1)
    qi = pl.program_id(2)
    tq = q_ref.shape[0]
    th = tq // nsplit
    bias2 = bias_ref[h] * LOG2_E
    suffix = _suffix_matrix(tk)
    q0 = qi * tq
    q_bf = [q_ref[s * th:(s + 1) * th, :].astype(BF16) for s in range(nsplit)]
    zero = jnp.zeros((th, tk), F32)
    carry = [zero] * nsplit
    acc = [zero] * nsplit
    row = lax.broadcasted_iota(jnp.int32, (th, tk), 0)
    col = lax.broadcasted_iota(jnp.int32, (th, tk), 1)

    for d in reversed(range(tq // tk)):
        k0 = pl.multiple_of(q0 + d * tk, tk)
        k_blk = k_ref[pl.ds(k0, tk), :]
        v_blk = v_ref[pl.ds(k0, tk), :]
        for s in range(nsplit):
            if d * tk >= (s + 1) * th:
                continue
            mask = None if (d + 1) * tk <= s * th else (col + d * tk) < (row + s * th)
            carry[s], acc[s] = _sb_tile(q_bf[s], k_blk, v_blk, bias2, suffix, carry[s], acc[s], mask)
    for s in range(nsplit):
        carry_ref[s * th:(s + 1) * th, :] = carry[s]
        acc_ref[s * th:(s + 1) * th, :] = acc[s]

    q_all = q_ref[...].astype(BF16)
    nfull = q0 // tk

    def key_rows(f):
        return pl.ds(pl.multiple_of(jnp.maximum(nfull - 1 - f, 0) * tk, tk), tk)

    def stage_logits(f, slot):
        log_b, n = _sb_logits(_dot_nt(q_all, k_ref[key_rows(f), :]) * SB_LOGIT_SCALE + bias2, None)
        lb_ref[slot] = log_b
        n_ref[slot] = n

    def stage_weights(slot):
        a, carry = _sb_weights(lb_ref[slot], n_ref[slot], suffix, carry_ref[...], None)
        a_ref[slot] = a
        carry_ref[...] = carry

    def stage_values(f, slot):
        acc_ref[...] += jnp.dot(a_ref[slot], v_ref[key_rows(f), :].astype(BF16), preferred_element_type=F32)

    for u in range(unroll):
        stage_logits(u, u)
        stage_weights(u)
    for u in range(unroll):
        stage_logits(unroll + u, u)

    def body(it, c):
        f0 = it * unroll
        for u in range(unroll):
            stage_values(f0 + u, u)
        for u in range(unroll):
            stage_weights(u)
        for u in range(unroll):
            stage_logits(f0 + 2 * unroll + u, u)
        return c

    lax.fori_loop(0, nfull // unroll, body, 0)
    o_ref[...] = acc_ref[...]


def _sb_prompt(proj, bias, bsz, length, tq, tk, nsplit, unroll):
    m = bsz * length
    nq = length // tq
    qcol = (2 * GLA_QK + 2 * GLA_V) // SB_HEAD_DIM
    kcol = qcol + SB_HEADS
    vcol = kcol + SB_HEADS
    return pl.pallas_call(
        functools.partial(_sb_prompt_kernel, tk=tk, nsplit=nsplit, unroll=unroll),
        grid=(bsz, SB_HEADS, nq),
        in_specs=[pl.BlockSpec(memory_space=pltpu.SMEM),
                  pl.BlockSpec((tq, SB_HEAD_DIM), lambda b, h, i: (b * nq + i, qcol + h)),
                  pl.BlockSpec((length, SB_HEAD_DIM), lambda b, h, i: (b, kcol + h)),
                  pl.BlockSpec((length, SB_HEAD_DIM), lambda b, h, i: (b, vcol + h))],
        out_specs=pl.BlockSpec((tq, SB_HEAD_DIM), lambda b, h, i: (b * nq + i, h)),
        out_shape=jax.ShapeDtypeStruct((m, SB_W), F32),
        scratch_shapes=[pltpu.VMEM((unroll, tq, tk), F32),
                        pltpu.VMEM((unroll, tq, tk), BF16),
                        pltpu.VMEM((unroll, tq, tk), BF16),
                        pltpu.VMEM((tq, tk), F32),
                        pltpu.VMEM((tq, tk), F32)],
        compiler_params=_cparams(("parallel", "parallel", "arbitrary")),
        name="sb_prompt",
    )(bias, proj, proj, proj)


def _sb_paged_kernel(pt_ref, bias_ref, q_ref, kn_ref, vn_ref, *refs, pages_per_step):
    g = pages_per_step
    k_refs, v_refs = refs[:g], refs[g:2 * g]
    o_ref, carry_ref, acc_ref, kpad_ref, vpad_ref = refs[2 * g:]
    j = pl.program_id(1)
    lq = q_ref.shape[0]
    nrow = SB_HEADS * lq
    suffix = _suffix_matrix(PAGE_SIZE)
    rhead = lax.broadcasted_iota(jnp.int32, (nrow, PAGE_SIZE), 0) // lq
    bias2 = jnp.zeros((nrow, PAGE_SIZE), F32)
    for h in range(SB_HEADS):
        bias2 = jnp.where(rhead == h, bias_ref[h] * LOG2_E, bias2)
    q_bf = [q_ref[:, h * SB_HEAD_DIM:(h + 1) * SB_HEAD_DIM].astype(BF16) for h in range(SB_HEADS)]

    def tile(k_heads, v_heads, carry, acc, mask):
        z = jnp.concatenate([_dot_nt(q_bf[h], k_heads[h]) for h in range(SB_HEADS)], axis=0)
        log_b, n = _sb_logits(z * SB_LOGIT_SCALE + bias2, mask)
        a, carry = _sb_weights(log_b, n, suffix, carry, mask)
        acc = [acc[h] + _dot(a[h * lq:(h + 1) * lq, :], v_heads[h]) for h in range(SB_HEADS)]
        return carry, acc

    @pl.when(j == 0)
    def _():
        kpad_ref[...] = jnp.zeros_like(kpad_ref)
        vpad_ref[...] = jnp.zeros_like(vpad_ref)
        for h in range(SB_HEADS):
            hs = slice(h * SB_HEAD_DIM, (h + 1) * SB_HEAD_DIM)
            kpad_ref[h, 0:lq, :] = kn_ref[:, hs]
            vpad_ref[h, 0:lq, :] = vn_ref[:, hs]
        qidx = lax.broadcasted_iota(jnp.int32, (nrow, PAGE_SIZE), 0) % lq
        kidx = lax.broadcasted_iota(jnp.int32, (nrow, PAGE_SIZE), 1)
        carry, acc = tile([kpad_ref[h] for h in range(SB_HEADS)], [vpad_ref[h] for h in range(SB_HEADS)],
                          jnp.zeros((nrow, PAGE_SIZE), F32),
                          [jnp.zeros((lq, SB_HEAD_DIM), F32)] * SB_HEADS, kidx < qidx)
        carry_ref[...] = carry
        acc_ref[...] = jnp.concatenate(acc, axis=0)

    nchunk = (PAGE_SIZE * SB_HEADS) // LANES
    ntile = g * nchunk
    q_stack = jnp.concatenate(q_bf, axis=0)
    own = (lax.broadcasted_iota(jnp.int32, (nrow, LANES), 1) % SB_HEADS) == rhead
    tiles = []
    for p in range(g):
        z = _dot_nt(q_stack, k_refs[p][0])
        tiles += [z[:, c * LANES:(c + 1) * LANES] for c in reversed(range(nchunk))]
    z = jnp.concatenate(tiles, axis=0).reshape(ntile, nrow, LANES)
    log_b, n = _sb_logits(z * SB_LOGIT_SCALE + bias2[None], own[None])
    sums = jnp.dot(n.reshape(ntile * nrow, LANES), suffix, preferred_element_type=F32)
    sums = sums.reshape(ntile, nrow, 2 * LANES)
    carry = carry_ref[...]
    carries = []
    for t in range(ntile):
        carries.append(carry)
        carry = carry + sums[t, :, LANES:]
    carry_ref[...] = carry
    a = jnp.exp2(log_b - sums[:, :, :LANES] - jnp.stack(carries, axis=0))
    a = jnp.where(own[None], a, 0.0).astype(BF16)
    acc = acc_ref[...]
    for p in range(g):
        a_page = jnp.concatenate([a[p * nchunk + (nchunk - 1 - c)] for c in range(nchunk)], axis=1)
        acc = acc + jnp.dot(a_page, v_refs[p][0].astype(BF16), preferred_element_type=F32)
    acc_ref[...] = acc

    @pl.when(j == pl.num_programs(1) - 1)
    def _():
        for h in range(SB_HEADS):
            o_ref[:, h * SB_HEAD_DIM:(h + 1) * SB_HEAD_DIM] = acc[h * lq:(h + 1) * lq, :]


def _sb_paged(proj, bias, pool_k, pool_v, page_table, bsz, lq, pages_per_step):
    n_pages = page_table.shape[1]
    g = pages_per_step
    nsteps = n_pages // g
    qcol = (2 * GLA_QK + 2 * GLA_V) // SB_W

    def page_spec(p):
        return pl.BlockSpec((1, PAGE_SIZE * SB_HEADS, SB_HEAD_DIM),
                            lambda b, j, pt: (pt[b, n_pages - 1 - (j * g + p)], 0, 0))

    grid_spec = pltpu.PrefetchScalarGridSpec(
        num_scalar_prefetch=1,
        grid=(bsz, nsteps),
        in_specs=[pl.BlockSpec(memory_space=pltpu.SMEM),
                  pl.BlockSpec((lq, SB_W), lambda b, j, pt: (b, qcol)),
                  pl.BlockSpec((lq, SB_W), lambda b, j, pt: (b, qcol + 1)),
                  pl.BlockSpec((lq, SB_W), lambda b, j, pt: (b, qcol + 2))]
        + [page_spec(p) for p in range(g)] + [page_spec(p) for p in range(g)],
        out_specs=pl.BlockSpec((lq, SB_W), lambda b, j, pt: (b, 0)),
        scratch_shapes=[pltpu.VMEM((SB_HEADS * lq, PAGE_SIZE), F32),
                        pltpu.VMEM((SB_HEADS * lq, SB_HEAD_DIM), F32),
                        pltpu.VMEM((SB_HEADS, PAGE_SIZE, SB_HEAD_DIM), F32),
                        pltpu.VMEM((SB_HEADS, PAGE_SIZE, SB_HEAD_DIM), F32)])
    return pl.pallas_call(
        functools.partial(_sb_paged_kernel, pages_per_step=g),
        grid_spec=grid_spec,
        out_shape=jax.ShapeDtypeStruct((bsz * lq, SB_W), F32),
        compiler_params=_cparams(("parallel", "arbitrary")),
        name="sb_paged",
    )(page_table, bias, proj, proj, proj, *([pool_k] * g), *([pool_v] * g))


def _ssd_kernel(xbc_ref, z_ref, dtr_ref, cw_ref, cb_ref, dtb_ref, alog_ref, dsk_ref, ng_ref,
                conv0_ref, h0_ref, y_ref, hout_ref, xext_ref, hs_ref):
    ci = pl.program_id(1)
    c = xbc_ref.shape[0]
    tail = SSD_CONV - 1
    pad = 8

    @pl.when(ci == 0)
    def _():
        xext_ref[0:pad, :] = conv0_ref[0]
        hs_ref[...] = h0_ref[0]

    xext_ref[pad:pad + c, :] = xbc_ref[...]
    acc = cb_ref[...]
    for w in range(SSD_CONV):
        acc = acc + xext_ref[pl.ds(pad - tail + w, c), :] * cw_ref[w:w + 1, :]
    xc = _silu(acc)
    xext_ref[0:pad, :] = xext_ref[c:c + pad, :]

    dt = _softplus(dtr_ref[...] + dtb_ref[...])
    a = -jnp.exp(alog_ref[...])
    row = lax.broadcasted_iota(jnp.int32, (c, c), 0)
    col = lax.broadcasted_iota(jnp.int32, (c, c), 1)
    causal = col <= row
    cum = _sel_dot(jnp.where(causal, 1.0, 0.0).astype(BF16), dt * a)
    cum_t = _transpose_f32(cum)
    dt_t = _transpose_f32(dt)
    last = cum[c - 1:c, :]
    e_cum = jnp.exp(cum)
    dec = jnp.exp(last - cum) * dt
    e_last_t = jnp.exp(cum_t[:, c - 1:c])
    lo_lanes = lax.broadcasted_iota(jnp.int32, (c, LANES), 1) < SSD_HEAD_DIM
    lo_rows = lax.broadcasted_iota(jnp.int32, (LANES, LANES), 0) < SSD_HEAD_DIM
    hpg = SSD_HEADS // SSD_GROUPS

    for g in range(SSD_GROUPS):
        b_g = xc[:, SSD_INNER + g * SSD_STATE:SSD_INNER + (g + 1) * SSD_STATE]
        c_g = xc[:, SSD_INNER + (SSD_GROUPS + g) * SSD_STATE:SSD_INNER + (SSD_GROUPS + g + 1) * SSD_STATE]
        cb = _dot_nt(c_g, b_g)
        for pp in range(hpg // 2):
            pair = g * (hpg // 2) + pp
            h0, h1 = 2 * pair, 2 * pair + 1
            x_pair = xc[:, pair * LANES:(pair + 1) * LANES]
            hs = hs_ref[pair]
            y_pair = _dot_nt(c_g, hs) * jnp.where(lo_lanes, e_cum[:, h0:h0 + 1], e_cum[:, h1:h1 + 1])
            for hh, keep in ((h0, lo_lanes), (h1, jnp.logical_not(lo_lanes))):
                lmat = jnp.exp(jnp.where(causal, cum[:, hh:hh + 1] - cum_t[hh:hh + 1, :], NEG_INF))
                w = lmat * dt_t[hh:hh + 1, :] * cb
                y_pair = y_pair + _dot(w, jnp.where(keep, x_pair, 0.0))
            xd = x_pair * jnp.where(lo_lanes, dec[:, h0:h0 + 1], dec[:, h1:h1 + 1])
            scale = jnp.where(lo_rows, e_last_t[h0:h0 + 1, :], e_last_t[h1:h1 + 1, :])
            hs_ref[pair] = scale * hs + _dot_tn(xd, b_g)
            y_pair = y_pair + x_pair * dsk_ref[:, pair * LANES:(pair + 1) * LANES]
            zz = z_ref[:, pair * LANES:(pair + 1) * LANES]
            y_ref[:, pair * LANES:(pair + 1) * LANES] = y_pair * _silu(zz)

    gw = SSD_INNER // SSD_GROUPS
    for g in range(SSD_GROUPS):
        gs = slice(g * gw, (g + 1) * gw)
        y_ref[:, gs] = _rms_scale(y_ref[:, gs]) * ng_ref[:, gs]

    @pl.when(ci == pl.num_programs(1) - 1)
    def _():
        hout_ref[0] = hs_ref[...]


def _ssd(zx, side, cw, cb, dtb, alog, dsk, ng, conv0, h0, bsz, length, chunk):
    nc = length // chunk
    m = bsz * length
    row = lambda b, c: b * nc + c
    const = lambda b, c: (0, 0)
    npair = SSD_HEADS // 2
    return pl.pallas_call(
        _ssd_kernel,
        grid=(bsz, nc),
        in_specs=[pl.BlockSpec((chunk, SSD_CONV_DIM), lambda b, c: (row(b, c), 0)),
                  pl.BlockSpec((chunk, SSD_INNER), lambda b, c: (row(b, c), SSD_CONV_DIM // SSD_INNER)),
                  pl.BlockSpec((chunk, SIDE_W), lambda b, c: (row(b, c), 0)),
                  pl.BlockSpec((SSD_CONV, SSD_CONV_DIM), const),
                  pl.BlockSpec((1, SSD_CONV_DIM), const),
                  pl.BlockSpec((1, SIDE_W), const),
                  pl.BlockSpec((1, SIDE_W), const),
                  pl.BlockSpec((1, SSD_INNER), const),
                  pl.BlockSpec((1, SSD_INNER), const),
                  pl.BlockSpec((1, 8, SSD_CONV_DIM), lambda b, c: (b, 0, 0)),
                  pl.BlockSpec((1, npair, LANES, SSD_STATE), lambda b, c: (b, 0, 0, 0))],
        out_specs=[pl.BlockSpec((chunk, SSD_INNER), lambda b, c: (row(b, c), 0)),
                   pl.BlockSpec((1, npair, LANES, SSD_STATE), lambda b, c: (b, 0, 0, 0))],
        out_shape=[jax.ShapeDtypeStruct((m, SSD_INNER), F32),
                   jax.ShapeDtypeStruct((bsz, npair, LANES, SSD_STATE), F32)],
        scratch_shapes=[pltpu.VMEM((chunk + 8, SSD_CONV_DIM), F32),
                        pltpu.VMEM((npair, LANES, SSD_STATE), F32)],
        compiler_params=_cparams(("parallel", "arbitrary")),
        name="ssd",
    )(zx, zx, side, cw, cb, dtb, alog, dsk, ng, conv0, h0)


def _pad_cols(w, width):
    return jnp.pad(w, ((0, 0), (0, width - w.shape[1])))


def _row(v, width=None):
    v = v.reshape(1, -1).astype(F32)
    return v if width is None else _pad_cols(v, width)


def _prep_params(p):
    w_in = p["w_in_even"][0]
    n_gla = 2 * GLA_QK + GLA_V
    q = {}
    q["even_main"] = jnp.concatenate([w_in[:, :n_gla], w_in[:, n_gla + GLA_GATE_RANK:]], axis=1).astype(BF16)
    q["even_side"] = _pad_cols(w_in[:, n_gla:n_gla + GLA_GATE_RANK], SIDE_W).astype(BF16)
    q["gate_w2"] = jnp.pad(p["gla_w_gate2"][0], ((0, SIDE_W - GLA_GATE_RANK), (0, 0))).astype(BF16)
    q["gate_b"] = _row(p["gla_b_gate"][0])
    q["gla_g"] = _row(p["gla_norm_g"][0])
    q["sb_bias"] = p["sb_logit_bias"][0].astype(F32)
    q["even_out"] = p["w_out_even"][0].astype(BF16)
    w_ssd = p["ssd_w_in"][0]
    q["ssd_main"] = jnp.concatenate([w_ssd[:, SSD_INNER:SSD_INNER + SSD_CONV_DIM], w_ssd[:, :SSD_INNER]],
                                    axis=1).astype(BF16)
    q["ssd_side"] = _pad_cols(w_ssd[:, SSD_INNER + SSD_CONV_DIM:], SIDE_W).astype(BF16)
    q["conv_w"] = p["ssd_conv_w"][0].astype(F32)
    q["conv_b"] = _row(p["ssd_conv_b"][0])
    q["dt_bias"] = _row(p["ssd_dt_bias"][0], SIDE_W)
    q["a_log"] = _row(p["ssd_a_log"][0], SIDE_W)
    q["d_skip"] = _row(jnp.repeat(p["ssd_d"][0], SSD_HEAD_DIM))
    q["ssd_g"] = _row(p["ssd_norm_g"][0])
    q["ssd_out"] = p["ssd_w_out"][0].astype(BF16)
    for name in ("norm_mix_pre", "norm_mix_post", "norm_mlp_pre", "norm_mlp_post"):
        q[name] = [_row(p[name][li]) for li in range(2)]
    q["mlp_up"] = [p["mlp_w_up"][li].astype(BF16) for li in range(2)]
    q["mlp_down"] = [p["mlp_w_down"][li].astype(BF16) for li in range(2)]
    return q


def _trunk(x, paged, gla_s0, ssm_h0, conv0, q, cfg):
    bsz, length, d = x.shape
    m = bsz * length
    tm = cfg["tm"]
    x2 = x.reshape(m, d)

    proj, side = _norm_matmul(x2, q["norm_mix_pre"][0], q["even_main"], q["even_side"], tm, cfg["tn"])
    o_gla, gla_new = _gla(proj, side, q["gate_w2"], q["gate_b"], q["gla_g"], gla_s0,
                          bsz, length, cfg["gla_rows"], cfg["gla_sub"])
    if paged is None:
        o_sb = _sb_prompt(proj, q["sb_bias"], bsz, length, cfg["sb_tq"], PAGE_SIZE, cfg["sb_split"],
                          cfg["sb_unroll"])
    else:
        o_sb = _sb_paged(proj, q["sb_bias"], paged[0], paged[1], paged[2], bsz, length, cfg["sb_pages"])
    kcol = 2 * GLA_QK + 2 * GLA_V + SB_W
    sb_k = proj[:, kcol:kcol + SB_W].reshape(bsz, length, SB_HEADS, SB_HEAD_DIM)
    sb_v = proj[:, kcol + SB_W:kcol + 2 * SB_W].reshape(bsz, length, SB_HEADS, SB_HEAD_DIM)
    x2 = _out_proj([o_gla, o_sb], q["even_out"], x2, q["norm_mix_post"][0], tm)
    x2 = _mlp(x2, q["norm_mlp_pre"][0], q["mlp_up"][0], q["mlp_down"][0], q["norm_mlp_post"][0], tm, 1024)

    zx, dtr = _norm_matmul(x2, q["norm_mix_pre"][1], q["ssd_main"], q["ssd_side"], tm, cfg["tn"])
    conv_pad = jnp.pad(conv0, ((0, 0), (8 - (SSD_CONV - 1), 0), (0, 0)))
    h0 = ssm_h0.reshape(bsz, SSD_HEADS // 2, 2 * SSD_HEAD_DIM, SSD_STATE)
    y, h_new = _ssd(zx, dtr, q["conv_w"], q["conv_b"], q["dt_bias"], q["a_log"], q["d_skip"], q["ssd_g"],
                    conv_pad, h0, bsz, length, cfg["ssd_chunk"])
    conv_new = zx.reshape(bsz, length, SSD_MAIN)[:, length - (SSD_CONV - 1):, :SSD_CONV_DIM]
    x2 = _out_proj([y], q["ssd_out"], x2, q["norm_mix_post"][1], tm)
    x2 = _mlp(x2, q["norm_mlp_pre"][1], q["mlp_up"][1], q["mlp_down"][1], q["norm_mlp_post"][1], tm, 1024)

    return (x2.reshape(bsz, length, d), sb_k[None], sb_v[None], gla_new[None],
            h_new.reshape(bsz, SSD_HEADS, SSD_HEAD_DIM, SSD_STATE)[None], conv_new[None])


def _prompt_cfg(length):
    return dict(tm=min(512, 2 * length), tn=1024, gla_rows=min(256, length), gla_sub=16,
                sb_tq=min(512, length), sb_split=2, sb_unroll=2,
                ssd_chunk=min(128, length))


def _sample_cfg(bsz, length, n_pages):
    return dict(tm=bsz * length, tn=1024, gla_rows=length, gla_sub=length, sb_pages=min(16, n_pages),
                ssd_chunk=length)


def kernel(x_prompt, x_sample, cache_sb_k, cache_sb_v, state_gla, state_ssm, state_conv, page_table, w_in_even, gla_w_gate2, gla_b_gate, gla_norm_g, sb_logit_bias, w_out_even, ssd_w_in, ssd_conv_w, ssd_conv_b, ssd_dt_bias, ssd_a_log, ssd_d, ssd_norm_g, ssd_w_out, norm_mix_pre, norm_mix_post, norm_mlp_pre, norm_mlp_post, mlp_w_up, mlp_w_down):
    q = _prep_params(dict(
        w_in_even=w_in_even, gla_w_gate2=gla_w_gate2, gla_b_gate=gla_b_gate, gla_norm_g=gla_norm_g,
        sb_logit_bias=sb_logit_bias, w_out_even=w_out_even, ssd_w_in=ssd_w_in, ssd_conv_w=ssd_conv_w,
        ssd_conv_b=ssd_conv_b, ssd_dt_bias=ssd_dt_bias, ssd_a_log=ssd_a_log, ssd_d=ssd_d,
        ssd_norm_g=ssd_norm_g, ssd_w_out=ssd_w_out, norm_mix_pre=norm_mix_pre, norm_mix_post=norm_mix_post,
        norm_mlp_pre=norm_mlp_pre, norm_mlp_post=norm_mlp_post, mlp_w_up=mlp_w_up, mlp_w_down=mlp_w_down))
    bp, lp, _ = x_prompt.shape
    bs, ls, _ = x_sample.shape
    zeros = lambda *s: jnp.zeros(s, F32)
    prompt = _trunk(x_prompt, None, zeros(bp, GLA_HEADS, GLA_DK, GLA_DV),
                    zeros(bp, SSD_HEADS, SSD_HEAD_DIM, SSD_STATE), zeros(bp, SSD_CONV - 1, SSD_CONV_DIM),
                    q, _prompt_cfg(lp))
    n_pool = cache_sb_k.shape[1]
    pool_k = cache_sb_k.reshape(n_pool, PAGE_SIZE * SB_HEADS, SB_HEAD_DIM)
    pool_v = cache_sb_v.reshape(n_pool, PAGE_SIZE * SB_HEADS, SB_HEAD_DIM)
    sample = _trunk(x_sample, (pool_k, pool_v, page_table), state_gla[0], state_ssm[0], state_conv[0],
                    q, _sample_cfg(bs, ls, page_table.shape[1]))
    return (prompt[0], sample[0]) + prompt[1:] + sample[1:]
```

```python
import functools

import jax
import jax.numpy as jnp
from jax import lax
from jax.experimental import pallas as pl
from jax.experimental.pallas import tpu as pltpu

F32 = jnp.float32
BF16 = jnp.bfloat16

D_MODEL = 1024
RMS_EPS = 1e-6
D_FF = 4 * D_MODEL
GLA_HEADS = 4
GLA_DK = 64
GLA_DV = 128
GLA_GATE_RANK = 16
GLA_GATE_TEMP = 16.0
GLA_QK = GLA_HEADS * GLA_DK
GLA_V = GLA_HEADS * GLA_DV
SB_HEADS = 4
SB_HEAD_DIM = 128
SB_W = SB_HEADS * SB_HEAD_DIM
PAGE_SIZE = 128
EVEN_MAIN = 2 * GLA_QK + 2 * GLA_V + 3 * SB_W
SSD_INNER = 2048
SSD_HEAD_DIM = 64
SSD_HEADS = 32
SSD_GROUPS = 8
SSD_STATE = 128
SSD_CONV = 4
SSD_CONV_DIM = SSD_INNER + 2 * SSD_GROUPS * SSD_STATE
SSD_MAIN = SSD_CONV_DIM + SSD_INNER
LANES = 128
SIDE_W = LANES
VMEM_LIMIT = 48 * 1024 * 1024

NEG_INF = float("-inf")


def _cparams(sem):
    return pltpu.CompilerParams(dimension_semantics=sem, vmem_limit_bytes=VMEM_LIMIT)


def _dot(a, b):
    return jnp.dot(a.astype(BF16), b.astype(BF16), preferred_element_type=F32)


def _dot_nt(a, b):
    return lax.dot_general(a.astype(BF16), b.astype(BF16), (((1,), (1,)), ((), ())),
                           preferred_element_type=F32)


def _dot_tn(a, b):
    return lax.dot_general(a.astype(BF16), b.astype(BF16), (((0,), (0,)), ((), ())),
                           preferred_element_type=F32)


def _split3(x):
    hi = x.astype(BF16)
    r1 = x - hi.astype(F32)
    mid = r1.astype(BF16)
    lo = (r1 - mid.astype(F32)).astype(BF16)
    return hi, mid, lo


def _sel_dot(sel, x):
    hi, mid, lo = _split3(x)
    d = functools.partial(jnp.dot, preferred_element_type=F32)
    return d(sel, hi) + d(sel, mid) + d(sel, lo)


def _dot_sel(x, sel):
    hi, mid, lo = _split3(x)
    d = functools.partial(jnp.dot, preferred_element_type=F32)
    return d(hi, sel) + d(mid, sel) + d(lo, sel)


def _transpose_f32(x):
    n = x.shape[1]
    eye = (lax.broadcasted_iota(jnp.int32, (n, n), 0)
           == lax.broadcasted_iota(jnp.int32, (n, n), 1)).astype(BF16)
    hi, mid, lo = _split3(x)
    d = lambda p: lax.dot_general(eye, p, (((1,), (1,)), ((), ())), preferred_element_type=F32)
    return d(hi) + d(mid) + d(lo)


def _softplus_neg_abs(x):
    return jnp.log1p(jnp.exp(-jnp.abs(x)))


def _log_sigmoid(x):
    return jnp.minimum(x, 0.0) - _softplus_neg_abs(x)


def _softplus(x):
    return jnp.maximum(x, 0.0) + _softplus_neg_abs(x)


def _silu(x):
    return x / (1.0 + jnp.exp(-x))


def _rms_scale(x):
    return x * lax.rsqrt(jnp.mean(x * x, axis=-1, keepdims=True) + RMS_EPS)


def _norm_matmul_kernel(x_ref, g_ref, w_ref, ws_ref, o_ref, os_ref, xn_ref):
    j = pl.program_id(1)
    tn = o_ref.shape[1]

    @pl.when(j == 0)
    def _():
        xn_ref[...] = (_rms_scale(x_ref[...]) * g_ref[...]).astype(BF16)
        os_ref[...] = jnp.dot(xn_ref[...], ws_ref[...], preferred_element_type=F32)

    cols = pl.ds(pl.multiple_of(j * tn, tn), tn)
    o_ref[...] = jnp.dot(xn_ref[...], w_ref[:, cols], preferred_element_type=F32)


def _norm_matmul(x, g, w, w_side, tm, tn):
    m, d = x.shape
    n = w.shape[1]
    once = pl.Buffered(1)
    return pl.pallas_call(
        _norm_matmul_kernel,
        grid=(m // tm, n // tn),
        in_specs=[pl.BlockSpec((tm, d), lambda i, j: (i, 0)),
                  pl.BlockSpec((1, d), lambda i, j: (0, 0)),
                  pl.BlockSpec((d, n), lambda i, j: (0, 0), pipeline_mode=once),
                  pl.BlockSpec((d, SIDE_W), lambda i, j: (0, 0), pipeline_mode=once)],
        out_specs=[pl.BlockSpec((tm, tn), lambda i, j: (i, j)),
                   pl.BlockSpec((tm, SIDE_W), lambda i, j: (i, 0))],
        out_shape=[jax.ShapeDtypeStruct((m, n), F32),
                   jax.ShapeDtypeStruct((m, SIDE_W), F32)],
        scratch_shapes=[pltpu.VMEM((tm, d), BF16)],
        compiler_params=_cparams(("parallel", "arbitrary")),
        name="norm_matmul",
    )(x, g, w, w_side)


def _out_proj_kernel(*refs, widths):
    n = len(widths)
    part_refs, w_ref, x_ref, g_ref, o_ref = refs[:n], refs[n], refs[n + 1], refs[n + 2], refs[n + 3]
    acc = None
    off = 0
    for p_ref, wd in zip(part_refs, widths):
        t = jnp.dot(p_ref[...].astype(BF16), w_ref[off:off + wd, :], preferred_element_type=F32)
        acc = t if acc is None else acc + t
        off += wd
    o_ref[...] = x_ref[...] + _rms_scale(acc) * g_ref[...]


def _out_proj(parts, w, x, g, tm):
    m, d = x.shape
    widths = tuple(p.shape[1] for p in parts)
    in_specs = [pl.BlockSpec((tm, wd), lambda i: (i, 0)) for wd in widths]
    in_specs += [pl.BlockSpec(w.shape, lambda i: (0, 0)),
                 pl.BlockSpec((tm, d), lambda i: (i, 0)),
                 pl.BlockSpec((1, d), lambda i: (0, 0))]
    return pl.pallas_call(
        functools.partial(_out_proj_kernel, widths=widths),
        grid=(m // tm,),
        in_specs=in_specs,
        out_specs=pl.BlockSpec((tm, d), lambda i: (i, 0)),
        out_shape=jax.ShapeDtypeStruct((m, d), F32),
        compiler_params=_cparams(("parallel",)),
        name="out_proj",
    )(*parts, w, x, g)


def _mlp_kernel(x_ref, g1_ref, wu_ref, wd_ref, g2_ref, o_ref, xn_ref, acc_ref):
    f = pl.program_id(1)

    @pl.when(f == 0)
    def _():
        xn_ref[...] = (_rms_scale(x_ref[...]) * g1_ref[...]).astype(BF16)
        acc_ref[...] = jnp.zeros_like(acc_ref)

    h = jnp.dot(xn_ref[...], wu_ref[...], preferred_element_type=F32)
    h = jnp.square(jnp.maximum(h, 0.0))
    acc_ref[...] += jnp.dot(h.astype(BF16), wd_ref[...], preferred_element_type=F32)

    @pl.when(f == pl.num_programs(1) - 1)
    def _():
        o_ref[...] = x_ref[...] + _rms_scale(acc_ref[...]) * g2_ref[...]


def _mlp(x, g1, wu, wd, g2, tm, tf):
    m, d = x.shape
    ff = wu.shape[1]
    return pl.pallas_call(
        _mlp_kernel,
        grid=(m // tm, ff // tf),
        in_specs=[pl.BlockSpec((tm, d), lambda i, f: (i, 0)),
                  pl.BlockSpec((1, d), lambda i, f: (0, 0)),
                  pl.BlockSpec((d, tf), lambda i, f: (0, f)),
                  pl.BlockSpec((tf, d), lambda i, f: (f, 0)),
                  pl.BlockSpec((1, d), lambda i, f: (0, 0))],
        out_specs=pl.BlockSpec((tm, d), lambda i, f: (i, 0)),
        out_shape=jax.ShapeDtypeStruct((m, d), F32),
        scratch_shapes=[pltpu.VMEM((tm, d), BF16), pltpu.VMEM((tm, d), F32)],
        compiler_params=_cparams(("parallel", "arbitrary")),
        name="mlp",
    )(x, g1, wu, wd, g2)


def _gla_kernel(q_ref, k_ref, v_ref, r_ref, glr_ref, w2_ref, bg_ref, g_ref, s0_ref,
                o_ref, s_ref, st_ref, b_ref, hi_ref, lo_ref, *, sub, unroll):
    tb = pl.program_id(1)
    t_rows = q_ref.shape[0]
    nsub = t_rows // sub
    npair = GLA_HEADS // 2

    @pl.when(tb == 0)
    def _():
        st_ref[...] = s0_ref[0]

    la = _log_sigmoid(_dot(glr_ref[...], w2_ref[...]) + bg_ref[...]) * (1.0 / GLA_GATE_TEMP)
    row = lax.broadcasted_iota(jnp.int32, (t_rows, t_rows), 0)
    col = lax.broadcasted_iota(jnp.int32, (t_rows, t_rows), 1)
    tri = jnp.where(((row // sub) == (col // sub)) & (col <= row), 1.0, 0.0).astype(BF16)
    b_ref[...] = _sel_dot(tri, la)
    la_hi = la.astype(BF16)
    hi_ref[...] = la_hi
    lo_ref[...] = (la - la_hi.astype(F32)).astype(BF16)
    tn = lambda a, b: lax.dot_general(a, b, (((0,), (0,)), ((), ())), preferred_element_type=F32)
    ones_sub = jnp.ones((sub, LANES), BF16)

    er = lax.broadcasted_iota(jnp.int32, (GLA_QK, GLA_V), 0) // GLA_DK
    ec = lax.broadcasted_iota(jnp.int32, (GLA_QK, GLA_V), 1) // GLA_DV
    expand = jnp.where(er == ec, 1.0, 0.0).astype(BF16)
    t_iota = lax.broadcasted_iota(jnp.int32, (sub, GLA_QK), 0)
    zeros_blk = jnp.zeros((GLA_DK, GLA_DV), F32)

    def sub_chunk(i):
        rows = pl.ds(pl.multiple_of(i * sub, sub), sub)
        b_i = b_ref[rows, :]
        q_i = q_ref[rows, :] * (GLA_DK ** -0.5)
        k_i = k_ref[rows, :]
        v_i = v_ref[rows, :]
        dfull = jnp.exp(tn(hi_ref[rows, :], ones_sub) + tn(lo_ref[rows, :], ones_sub))
        qe = (q_i * jnp.exp(b_i)).astype(BF16)
        kd = (k_i * jnp.exp(b_i[sub - 1:sub, :] - b_i)).astype(BF16)
        v_bf = v_i.astype(BF16)
        inter = []
        for p in range(npair):
            ha, hb = 2 * p, 2 * p + 1
            s_a, s_b = st_ref[ha], st_ref[hb]
            lanes = slice(p * LANES, (p + 1) * LANES)
            w_pair = jnp.concatenate([jnp.concatenate([s_a, zeros_blk], axis=1),
                                      jnp.concatenate([zeros_blk, s_b], axis=1)], axis=0)
            inter.append(jnp.dot(qe[:, lanes], w_pair.astype(BF16), preferred_element_type=F32))
            u = tn(kd[:, lanes], v_bf[:, ha * GLA_DV:(hb + 1) * GLA_DV])
            st_ref[ha] = s_a * dfull[ha * GLA_DK:(ha + 1) * GLA_DK, :] + u[:GLA_DK, :GLA_DV]
            st_ref[hb] = s_b * dfull[hb * GLA_DK:(hb + 1) * GLA_DK, :] + u[GLA_DK:, GLA_DV:]
        prods = []
        for j in range(sub):
            e = jnp.exp(jnp.where(t_iota >= j, b_i - b_i[j:j + 1, :], NEG_INF))
            prods.append(q_i * e * k_i[j:j + 1, :])
        p_all = jnp.concatenate(prods, axis=0)
        r_all = _dot(p_all, expand)
        o = jnp.concatenate(inter, axis=1)
        for j in range(sub):
            o = o + r_all[j * sub:(j + 1) * sub, :] * v_i[j:j + 1, :]
        r_i = r_ref[rows, :]
        outs = []
        for h in range(GLA_HEADS):
            hv = slice(h * GLA_DV, (h + 1) * GLA_DV)
            outs.append(_rms_scale(o[:, hv]) * g_ref[...] * _silu(r_i[:, hv]))
        o_ref[rows, :] = jnp.concatenate(outs, axis=1)

    def body(it, carry):
        for u in range(unroll):
            sub_chunk(it * unroll + u)
        return carry

    lax.fori_loop(0, nsub // unroll, body, 0)

    @pl.when(tb == pl.num_programs(1) - 1)
    def _():
        s_ref[0] = st_ref[...]


def _gla(proj, side, w2, bg, g, s0, bsz, length, t_rows, sub):
    nt = length // t_rows
    m = bsz * length
    row = lambda b, t: b * nt + t
    return pl.pallas_call(
        functools.partial(_gla_kernel, sub=sub, unroll=2 if (t_rows // sub) % 2 == 0 else 1),
        grid=(bsz, nt),
        in_specs=[pl.BlockSpec((t_rows, GLA_QK), lambda b, t: (row(b, t), 0)),
                  pl.BlockSpec((t_rows, GLA_QK), lambda b, t: (row(b, t), 1)),
                  pl.BlockSpec((t_rows, GLA_V), lambda b, t: (row(b, t), 1)),
                  pl.BlockSpec((t_rows, GLA_V), lambda b, t: (row(b, t), 2)),
                  pl.BlockSpec((t_rows, SIDE_W), lambda b, t: (row(b, t), 0)),
                  pl.BlockSpec((SIDE_W, GLA_QK), lambda b, t: (0, 0)),
                  pl.BlockSpec((1, GLA_QK), lambda b, t: (0, 0)),
                  pl.BlockSpec((1, GLA_DV), lambda b, t: (0, 0)),
                  pl.BlockSpec((1, GLA_HEADS, GLA_DK, GLA_DV), lambda b, t: (b, 0, 0, 0))],
        out_specs=[pl.BlockSpec((t_rows, GLA_V), lambda b, t: (row(b, t), 0)),
                   pl.BlockSpec((1, GLA_HEADS, GLA_DK, GLA_DV), lambda b, t: (b, 0, 0, 0))],
        out_shape=[jax.ShapeDtypeStruct((m, GLA_V), F32),
                   jax.ShapeDtypeStruct((bsz, GLA_HEADS, GLA_DK, GLA_DV), F32)],
        scratch_shapes=[pltpu.VMEM((GLA_HEADS, GLA_DK, GLA_DV), F32),
                        pltpu.VMEM((t_rows, GLA_QK), F32),
                        pltpu.VMEM((t_rows, GLA_QK), BF16),
                        pltpu.VMEM((t_rows, GLA_QK), BF16)],
        compiler_params=_cparams(("parallel", "arbitrary")),
        name="gla",
    )(proj, proj, proj, proj, side, w2, bg, g, s0)


LOG2_E = 1.4426950408889634
SB_LOGIT_SCALE = SB_HEAD_DIM ** -0.5 * LOG2_E


def _suffix_matrix(tk):
    jr = lax.broadcasted_iota(jnp.int32, (tk, tk + LANES), 0)
    jc = lax.broadcasted_iota(jnp.int32, (tk, tk + LANES), 1)
    return jnp.where((jr > jc) | (jc >= tk), 1.0, 0.0).astype(BF16)


def _sb_logits(t, mask):
    n = jnp.maximum(t, 0.0) + jnp.log2(1.0 + jnp.exp2(-jnp.abs(t)))
    log_b = t - n
    if mask is not None:
        n = jnp.where(mask, n, 0.0)
    return log_b, n.astype(BF16)


def _sb_weights(log_b, n, suffix, carry, mask):
    tk = log_b.shape[1]
    sums = jnp.dot(n, suffix, preferred_element_type=F32)
    a = jnp.exp2(log_b - sums[:, :tk] - jnp.concatenate([carry] * (tk // LANES), axis=1))
    if mask is not None:
        a = jnp.where(mask, a, 0.0)
    return a.astype(BF16), carry + sums[:, tk:]


def _sb_tile(q_bf, k_blk, v_blk, bias2, suffix, carry, acc, mask):
    log_b, n = _sb_logits(_dot_nt(q_bf, k_blk) * SB_LOGIT_SCALE + bias2, mask)
    a, carry = _sb_weights(log_b, n, suffix, carry, mask)
    return carry, acc + jnp.dot(a, v_blk.astype(BF16), preferred_element_type=F32)


def _sb_prompt_kernel(bias_ref, q_ref, k_ref, v_ref, o_ref, lb_ref, n_ref, a_ref, carry_ref, acc_ref,
                      *, tk, nsplit, unroll):
    h = pl.program_id(1)
    qi = pl.program_id(2)
    tq = q_ref.shape[0]
    th = tq // nsplit
    bias2 = bias_ref[h] * LOG2_E
    suffix = _suffix_matrix(tk)
    q0 = qi * tq
    q_bf = [q_ref[s * th:(s + 1) * th, :].astype(BF16) for s in range(nsplit)]
    zero = jnp.zeros((th, LANES), F32)
    carry = [zero] * nsplit
    acc = [zero] * nsplit
    row = lax.broadcasted_iota(jnp.int32, (th, tk), 0)
    col = lax.broadcasted_iota(jnp.int32, (th, tk), 1)

    for d in reversed(range(tq // tk)):
        k0 = pl.multiple_of(q0 + d * tk, tk)
        k_blk = k_ref[pl.ds(k0, tk), :]
        v_blk = v_ref[pl.ds(k0, tk), :]
        for s in range(nsplit):
            if d * tk >= (s + 1) * th:
                continue
            mask = None if (d + 1) * tk <= s * th else (col + d * tk) < (row + s * th)
            carry[s], acc[s] = _sb_tile(q_bf[s], k_blk, v_blk, bias2, suffix, carry[s], acc[s], mask)
    for s in range(nsplit):
        carry_ref[s * th:(s + 1) * th, :] = carry[s]
        acc_ref[s * th:(s + 1) * th, :] = acc[s]

    q_all = q_ref[...].astype(BF16)
    nfull = q0 // tk

    def key_rows(f):
        return pl.ds(pl.multiple_of(jnp.maximum(nfull - 1 - f, 0) * tk, tk), tk)

    def stage_logits(f, slot):
        log_b, n = _sb_logits(_dot_nt(q_all, k_ref[key_rows(f), :]) * SB_LOGIT_SCALE + bias2, None)
        lb_ref[slot] = log_b
        n_ref[slot] = n

    def stage_weights(slot):
        a, carry = _sb_weights(lb_ref[slot], n_ref[slot], suffix, carry_ref[...], None)
        a_ref[slot] = a
        carry_ref[...] = carry

    def stage_values(f, slot):
        acc_ref[...] += jnp.dot(a_ref[slot], v_ref[key_rows(f), :].astype(BF16), preferred_element_type=F32)

    for u in range(2 * unroll):
        stage_logits(u, u)
    for u in range(unroll):
        stage_weights(u)

    def body(it, c):
        for parity in range(2):
            cur = parity * unroll
            nxt = unroll - cur
            f0 = (2 * it + parity) * unroll
            for u in range(unroll):
                stage_logits(f0 + 2 * unroll + u, cur + u)
            for u in range(unroll):
                stage_weights(nxt + u)
            for u in range(unroll):
                stage_values(f0 + u, cur + u)
        return c

    lax.fori_loop(0, nfull // (2 * unroll), body, 0)
    o_ref[...] = acc_ref[...]


def _sb_prompt(proj, bias, bsz, length, tq, tk, nsplit, unroll):
    m = bsz * length
    nq = length // tq
    qcol = (2 * GLA_QK + 2 * GLA_V) // SB_HEAD_DIM
    kcol = qcol + SB_HEADS
    vcol = kcol + SB_HEADS
    return pl.pallas_call(
        functools.partial(_sb_prompt_kernel, tk=tk, nsplit=nsplit, unroll=unroll),
        grid=(bsz, SB_HEADS, nq),
        in_specs=[pl.BlockSpec(memory_space=pltpu.SMEM),
                  pl.BlockSpec((tq, SB_HEAD_DIM), lambda b, h, i: (b * nq + i, qcol + h)),
                  pl.BlockSpec((length, SB_HEAD_DIM), lambda b, h, i: (b, kcol + h)),
                  pl.BlockSpec((length, SB_HEAD_DIM), lambda b, h, i: (b, vcol + h))],
        out_specs=pl.BlockSpec((tq, SB_HEAD_DIM), lambda b, h, i: (b * nq + i, h)),
        out_shape=jax.ShapeDtypeStruct((m, SB_W), F32),
        scratch_shapes=[pltpu.VMEM((2 * unroll, tq, tk), F32),
                        pltpu.VMEM((2 * unroll, tq, tk), BF16),
                        pltpu.VMEM((2 * unroll, tq, tk), BF16),
                        pltpu.VMEM((tq, LANES), F32),
                        pltpu.VMEM((tq, SB_HEAD_DIM), F32)],
        compiler_params=_cparams(("parallel", "parallel", "arbitrary")),
        name="sb_prompt",
    )(bias, proj, proj, proj)


def _sb_paged_kernel(pt_ref, bias_ref, q_ref, kn_ref, vn_ref, *refs, pages_per_step):
    g = pages_per_step
    k_refs, v_refs = refs[:g], refs[g:2 * g]
    o_ref, carry_ref, acc_ref, kpad_ref, vpad_ref = refs[2 * g:]
    j = pl.program_id(1)
    lq = q_ref.shape[0]
    nrow = SB_HEADS * lq
    suffix = _suffix_matrix(PAGE_SIZE)
    rhead = lax.broadcasted_iota(jnp.int32, (nrow, PAGE_SIZE), 0) // lq
    bias2 = jnp.zeros((nrow, PAGE_SIZE), F32)
    for h in range(SB_HEADS):
        bias2 = jnp.where(rhead == h, bias_ref[h] * LOG2_E, bias2)
    q_bf = [q_ref[:, h * SB_HEAD_DIM:(h + 1) * SB_HEAD_DIM].astype(BF16) for h in range(SB_HEADS)]

    def tile(k_heads, v_heads, carry, acc, mask):
        z = jnp.concatenate([_dot_nt(q_bf[h], k_heads[h]) for h in range(SB_HEADS)], axis=0)
        log_b, n = _sb_logits(z * SB_LOGIT_SCALE + bias2, mask)
        a, carry = _sb_weights(log_b, n, suffix, carry, mask)
        acc = [acc[h] + _dot(a[h * lq:(h + 1) * lq, :], v_heads[h]) for h in range(SB_HEADS)]
        return carry, acc

    @pl.when(j == 0)
    def _():
        kpad_ref[...] = jnp.zeros_like(kpad_ref)
        vpad_ref[...] = jnp.zeros_like(vpad_ref)
        for h in range(SB_HEADS):
            hs = slice(h * SB_HEAD_DIM, (h + 1) * SB_HEAD_DIM)
            kpad_ref[h, 0:lq, :] = kn_ref[:, hs]
            vpad_ref[h, 0:lq, :] = vn_ref[:, hs]
        qidx = lax.broadcasted_iota(jnp.int32, (nrow, PAGE_SIZE), 0) % lq
        kidx = lax.broadcasted_iota(jnp.int32, (nrow, PAGE_SIZE), 1)
        carry, acc = tile([kpad_ref[h] for h in range(SB_HEADS)], [vpad_ref[h] for h in range(SB_HEADS)],
                          jnp.zeros((nrow, PAGE_SIZE), F32),
                          [jnp.zeros((lq, SB_HEAD_DIM), F32)] * SB_HEADS, kidx < qidx)
        carry_ref[...] = carry
        acc_ref[...] = jnp.concatenate(acc, axis=0)

    nchunk = (PAGE_SIZE * SB_HEADS) // LANES
    ntile = g * nchunk
    q_stack = jnp.concatenate(q_bf, axis=0)
    own = (lax.broadcasted_iota(jnp.int32, (nrow, LANES), 1) % SB_HEADS) == rhead
    tiles = []
    for p in range(g):
        z = _dot_nt(q_stack, k_refs[p][0])
        tiles += [z[:, c * LANES:(c + 1) * LANES] for c in reversed(range(nchunk))]
    z = jnp.concatenate(tiles, axis=0).reshape(ntile, nrow, LANES)
    log_b, n = _sb_logits(z * SB_LOGIT_SCALE + bias2[None], own[None])
    sums = jnp.dot(n.reshape(ntile * nrow, LANES), suffix, preferred_element_type=F32)
    sums = sums.reshape(ntile, nrow, 2 * LANES)
    carry = carry_ref[...]
    carries = []
    for t in range(ntile):
        carries.append(carry)
        carry = carry + sums[t, :, LANES:]
    carry_ref[...] = carry
    a = jnp.exp2(log_b - sums[:, :, :LANES] - jnp.stack(carries, axis=0))
    a = jnp.where(own[None], a, 0.0).astype(BF16)
    acc = acc_ref[...]
    for p in range(g):
        a_page = jnp.concatenate([a[p * nchunk + (nchunk - 1 - c)] for c in range(nchunk)], axis=1)
        acc = acc + jnp.dot(a_page, v_refs[p][0].astype(BF16), preferred_element_type=F32)
    acc_ref[...] = acc

    @pl.when(j == pl.num_programs(1) - 1)
    def _():
        for h in range(SB_HEADS):
            o_ref[:, h * SB_HEAD_DIM:(h + 1) * SB_HEAD_DIM] = acc[h * lq:(h + 1) * lq, :]


def _sb_paged(proj, bias, pool_k, pool_v, page_table, bsz, lq, pages_per_step):
    n_pages = page_table.shape[1]
    g = pages_per_step
    nsteps = n_pages // g
    qcol = (2 * GLA_QK + 2 * GLA_V) // SB_W

    def page_spec(p):
        return pl.BlockSpec((1, PAGE_SIZE * SB_HEADS, SB_HEAD_DIM),
                            lambda b, j, pt: (pt[b, n_pages - 1 - (j * g + p)], 0, 0))

    grid_spec = pltpu.PrefetchScalarGridSpec(
        num_scalar_prefetch=1,
        grid=(bsz, nsteps),
        in_specs=[pl.BlockSpec(memory_space=pltpu.SMEM),
                  pl.BlockSpec((lq, SB_W), lambda b, j, pt: (b, qcol)),
                  pl.BlockSpec((lq, SB_W), lambda b, j, pt: (b, qcol + 1)),
                  pl.BlockSpec((lq, SB_W), lambda b, j, pt: (b, qcol + 2))]
        + [page_spec(p) for p in range(g)] + [page_spec(p) for p in range(g)],
        out_specs=pl.BlockSpec((lq, SB_W), lambda b, j, pt: (b, 0)),
        scratch_shapes=[pltpu.VMEM((SB_HEADS * lq, PAGE_SIZE), F32),
                        pltpu.VMEM((SB_HEADS * lq, SB_HEAD_DIM), F32),
                        pltpu.VMEM((SB_HEADS, PAGE_SIZE, SB_HEAD_DIM), F32),
                        pltpu.VMEM((SB_HEADS, PAGE_SIZE, SB_HEAD_DIM), F32)])
    return pl.pallas_call(
        functools.partial(_sb_paged_kernel, pages_per_step=g),
        grid_spec=grid_spec,
        out_shape=jax.ShapeDtypeStruct((bsz * lq, SB_W), F32),
        compiler_params=_cparams(("parallel", "arbitrary")),
        name="sb_paged",
    )(page_table, bias, proj, proj, proj, *([pool_k] * g), *([pool_v] * g))


def _ssd_kernel(xbc_ref, z_ref, dtr_ref, cw_ref, cb_ref, dtb_ref, alog_ref, dsk_ref, ng_ref,
                conv0_ref, h0_ref, y_ref, hout_ref, xext_ref, hs_ref):
    ci = pl.program_id(1)
    c = xbc_ref.shape[0]
    tail = SSD_CONV - 1
    pad = 8

    @pl.when(ci == 0)
    def _():
        xext_ref[0:pad, :] = conv0_ref[0]
        hs_ref[...] = h0_ref[0]

    xext_ref[pad:pad + c, :] = xbc_ref[...]
    acc = cb_ref[...]
    for w in range(SSD_CONV):
        acc = acc + xext_ref[pl.ds(pad - tail + w, c), :] * cw_ref[w:w + 1, :]
    xc = _silu(acc)
    xext_ref[0:pad, :] = xext_ref[c:c + pad, :]

    dt = _softplus(dtr_ref[...] + dtb_ref[...])
    a = -jnp.exp(alog_ref[...])
    row = lax.broadcasted_iota(jnp.int32, (c, c), 0)
    col = lax.broadcasted_iota(jnp.int32, (c, c), 1)
    causal = col <= row
    cum = _sel_dot(jnp.where(causal, 1.0, 0.0).astype(BF16), dt * a)
    cum_t = _transpose_f32(cum)
    dt_t = _transpose_f32(dt)
    last = cum[c - 1:c, :]
    e_cum = jnp.exp(cum)
    dec = jnp.exp(last - cum) * dt
    e_last_t = jnp.exp(cum_t[:, c - 1:c])
    lo_lanes = lax.broadcasted_iota(jnp.int32, (c, LANES), 1) < SSD_HEAD_DIM
    lo_rows = lax.broadcasted_iota(jnp.int32, (LANES, LANES), 0) < SSD_HEAD_DIM
    hpg = SSD_HEADS // SSD_GROUPS

    for g in range(SSD_GROUPS):
        b_g = xc[:, SSD_INNER + g * SSD_STATE:SSD_INNER + (g + 1) * SSD_STATE]
        c_g = xc[:, SSD_INNER + (SSD_GROUPS + g) * SSD_STATE:SSD_INNER + (SSD_GROUPS + g + 1) * SSD_STATE]
        cb = _dot_nt(c_g, b_g)
        for pp in range(hpg // 2):
            pair = g * (hpg // 2) + pp
            h0, h1 = 2 * pair, 2 * pair + 1
            x_pair = xc[:, pair * LANES:(pair + 1) * LANES]
            hs = hs_ref[pair]
            y_pair = _dot_nt(c_g, hs) * jnp.where(lo_lanes, e_cum[:, h0:h0 + 1], e_cum[:, h1:h1 + 1])
            for hh, keep in ((h0, lo_lanes), (h1, jnp.logical_not(lo_lanes))):
                lmat = jnp.exp(jnp.where(causal, cum[:, hh:hh + 1] - cum_t[hh:hh + 1, :], NEG_INF))
                w = lmat * dt_t[hh:hh + 1, :] * cb
                y_pair = y_pair + _dot(w, jnp.where(keep, x_pair, 0.0))
            xd = x_pair * jnp.where(lo_lanes, dec[:, h0:h0 + 1], dec[:, h1:h1 + 1])
            scale = jnp.where(lo_rows, e_last_t[h0:h0 + 1, :], e_last_t[h1:h1 + 1, :])
            hs_ref[pair] = scale * hs + _dot_tn(xd, b_g)
            y_pair = y_pair + x_pair * dsk_ref[:, pair * LANES:(pair + 1) * LANES]
            zz = z_ref[:, pair * LANES:(pair + 1) * LANES]
            y_ref[:, pair * LANES:(pair + 1) * LANES] = y_pair * _silu(zz)

    gw = SSD_INNER // SSD_GROUPS
    for g in range(SSD_GROUPS):
        gs = slice(g * gw, (g + 1) * gw)
        y_ref[:, gs] = _rms_scale(y_ref[:, gs]) * ng_ref[:, gs]

    @pl.when(ci == pl.num_programs(1) - 1)
    def _():
        hout_ref[0] = hs_ref[...]


def _ssd(zx, side, cw, cb, dtb, alog, dsk, ng, conv0, h0, bsz, length, chunk):
    nc = length // chunk
    m = bsz * length
    row = lambda b, c: b * nc + c
    const = lambda b, c: (0, 0)
    npair = SSD_HEADS // 2
    return pl.pallas_call(
        _ssd_kernel,
        grid=(bsz, nc),
        in_specs=[pl.BlockSpec((chunk, SSD_CONV_DIM), lambda b, c: (row(b, c), 0)),
                  pl.BlockSpec((chunk, SSD_INNER), lambda b, c: (row(b, c), SSD_CONV_DIM // SSD_INNER)),
                  pl.BlockSpec((chunk, SIDE_W), lambda b, c: (row(b, c), 0)),
                  pl.BlockSpec((SSD_CONV, SSD_CONV_DIM), const),
                  pl.BlockSpec((1, SSD_CONV_DIM), const),
                  pl.BlockSpec((1, SIDE_W), const),
                  pl.BlockSpec((1, SIDE_W), const),
                  pl.BlockSpec((1, SSD_INNER), const),
                  pl.BlockSpec((1, SSD_INNER), const),
                  pl.BlockSpec((1, 8, SSD_CONV_DIM), lambda b, c: (b, 0, 0)),
                  pl.BlockSpec((1, npair, LANES, SSD_STATE), lambda b, c: (b, 0, 0, 0))],
        out_specs=[pl.BlockSpec((chunk, SSD_INNER), lambda b, c: (row(b, c), 0)),
                   pl.BlockSpec((1, npair, LANES, SSD_STATE), lambda b, c: (b, 0, 0, 0))],
        out_shape=[jax.ShapeDtypeStruct((m, SSD_INNER), F32),
                   jax.ShapeDtypeStruct((bsz, npair, LANES, SSD_STATE), F32)],
        scratch_shapes=[pltpu.VMEM((chunk + 8, SSD_CONV_DIM), F32),
                        pltpu.VMEM((npair, LANES, SSD_STATE), F32)],
        compiler_params=_cparams(("parallel", "arbitrary")),
        name="ssd",
    )(zx, zx, side, cw, cb, dtb, alog, dsk, ng, conv0, h0)


def _pad_cols(w, width):
    return jnp.pad(w, ((0, 0), (0, width - w.shape[1])))


def _row(v, width=None):
    v = v.reshape(1, -1).astype(F32)
    return v if width is None else _pad_cols(v, width)


def _prep_params(p):
    w_in = p["w_in_even"][0]
    n_gla = 2 * GLA_QK + GLA_V
    q = {}
    q["even_main"] = jnp.concatenate([w_in[:, :n_gla], w_in[:, n_gla + GLA_GATE_RANK:]], axis=1).astype(BF16)
    q["even_side"] = _pad_cols(w_in[:, n_gla:n_gla + GLA_GATE_RANK], SIDE_W).astype(BF16)
    q["gate_w2"] = jnp.pad(p["gla_w_gate2"][0], ((0, SIDE_W - GLA_GATE_RANK), (0, 0))).astype(BF16)
    q["gate_b"] = _row(p["gla_b_gate"][0])
    q["gla_g"] = _row(p["gla_norm_g"][0])
    q["sb_bias"] = p["sb_logit_bias"][0].astype(F32)
    q["even_out"] = p["w_out_even"][0].astype(BF16)
    w_ssd = p["ssd_w_in"][0]
    q["ssd_main"] = jnp.concatenate([w_ssd[:, SSD_INNER:SSD_INNER + SSD_CONV_DIM], w_ssd[:, :SSD_INNER]],
                                    axis=1).astype(BF16)
    q["ssd_side"] = _pad_cols(w_ssd[:, SSD_INNER + SSD_CONV_DIM:], SIDE_W).astype(BF16)
    q["conv_w"] = p["ssd_conv_w"][0].astype(F32)
    q["conv_b"] = _row(p["ssd_conv_b"][0])
    q["dt_bias"] = _row(p["ssd_dt_bias"][0], SIDE_W)
    q["a_log"] = _row(p["ssd_a_log"][0], SIDE_W)
    q["d_skip"] = _row(jnp.repeat(p["ssd_d"][0], SSD_HEAD_DIM))
    q["ssd_g"] = _row(p["ssd_norm_g"][0])
    q["ssd_out"] = p["ssd_w_out"][0].astype(BF16)
    for name in ("norm_mix_pre", "norm_mix_post", "norm_mlp_pre", "norm_mlp_post"):
        q[name] = [_row(p[name][li]) for li in range(2)]
    q["mlp_up"] = [p["mlp_w_up"][li].astype(BF16) for li in range(2)]
    q["mlp_down"] = [p["mlp_w_down"][li].astype(BF16) for li in range(2)]
    return q


def _trunk(x, paged, gla_s0, ssm_h0, conv0, q, cfg):
    bsz, length, d = x.shape
    m = bsz * length
    tm = cfg["tm"]
    x2 = x.reshape(m, d)

    proj, side = _norm_matmul(x2, q["norm_mix_pre"][0], q["even_main"], q["even_side"], tm, cfg["tn"])
    o_gla, gla_new = _gla(proj, side, q["gate_w2"], q["gate_b"], q["gla_g"], gla_s0,
                          bsz, length, cfg["gla_rows"], cfg["gla_sub"])
    if paged is None:
        o_sb = _sb_prompt(proj, q["sb_bias"], bsz, length, cfg["sb_tq"], cfg["sb_tk"], cfg["sb_split"],
                          cfg["sb_unroll"])
    else:
        o_sb = _sb_paged(proj, q["sb_bias"], paged[0], paged[1], paged[2], bsz, length, cfg["sb_pages"])
    kcol = 2 * GLA_QK + 2 * GLA_V + SB_W
    sb_k = proj[:, kcol:kcol + SB_W].reshape(bsz, length, SB_HEADS, SB_HEAD_DIM)
    sb_v = proj[:, kcol + SB_W:kcol + 2 * SB_W].reshape(bsz, length, SB_HEADS, SB_HEAD_DIM)
    x2 = _out_proj([o_gla, o_sb], q["even_out"], x2, q["norm_mix_post"][0], tm)
    x2 = _mlp(x2, q["norm_mlp_pre"][0], q["mlp_up"][0], q["mlp_down"][0], q["norm_mlp_post"][0],
              cfg["mlp_tm"], 1024)

    zx, dtr = _norm_matmul(x2, q["norm_mix_pre"][1], q["ssd_main"], q["ssd_side"], tm, cfg["tn"])
    conv_pad = jnp.pad(conv0, ((0, 0), (8 - (SSD_CONV - 1), 0), (0, 0)))
    h0 = ssm_h0.reshape(bsz, SSD_HEADS // 2, 2 * SSD_HEAD_DIM, SSD_STATE)
    y, h_new = _ssd(zx, dtr, q["conv_w"], q["conv_b"], q["dt_bias"], q["a_log"], q["d_skip"], q["ssd_g"],
                    conv_pad, h0, bsz, length, cfg["ssd_chunk"])
    conv_new = zx.reshape(bsz, length, SSD_MAIN)[:, length - (SSD_CONV - 1):, :SSD_CONV_DIM]
    x2 = _out_proj([y], q["ssd_out"], x2, q["norm_mix_post"][1], tm)
    x2 = _mlp(x2, q["norm_mlp_pre"][1], q["mlp_up"][1], q["mlp_down"][1], q["norm_mlp_post"][1],
              cfg["mlp_tm"], 1024)

    return (x2.reshape(bsz, length, d), sb_k[None], sb_v[None], gla_new[None],
            h_new.reshape(bsz, SSD_HEADS, SSD_HEAD_DIM, SSD_STATE)[None], conv_new[None])


def _prompt_cfg(length):
    return dict(tm=min(512, 2 * length), tn=1024, mlp_tm=min(1024, 2 * length),
                gla_rows=min(256, length), gla_sub=16,
                sb_tq=min(512, length), sb_tk=128, sb_split=2, sb_unroll=2,
                ssd_chunk=min(128, length))


def _sample_cfg(bsz, length, n_pages):
    return dict(tm=bsz * length, tn=1024, mlp_tm=bsz * length, gla_rows=length, gla_sub=length, sb_pages=min(16, n_pages),
                ssd_chunk=length)


def kernel(x_prompt, x_sample, cache_sb_k, cache_sb_v, state_gla, state_ssm, state_conv, page_table, w_in_even, gla_w_gate2, gla_b_gate, gla_norm_g, sb_logit_bias, w_out_even, ssd_w_in, ssd_conv_w, ssd_conv_b, ssd_dt_bias, ssd_a_log, ssd_d, ssd_norm_g, ssd_w_out, norm_mix_pre, norm_mix_post, norm_mlp_pre, norm_mlp_post, mlp_w_up, mlp_w_down):
    q = _prep_params(dict(
        w_in_even=w_in_even, gla_w_gate2=gla_w_gate2, gla_b_gate=gla_b_gate, gla_norm_g=gla_norm_g,
        sb_logit_bias=sb_logit_bias, w_out_even=w_out_even, ssd_w_in=ssd_w_in, ssd_conv_w=ssd_conv_w,
        ssd_conv_b=ssd_conv_b, ssd_dt_bias=ssd_dt_bias, ssd_a_log=ssd_a_log, ssd_d=ssd_d,
        ssd_norm_g=ssd_norm_g, ssd_w_out=ssd_w_out, norm_mix_pre=norm_mix_pre, norm_mix_post=norm_mix_post,
        norm_mlp_pre=norm_mlp_pre, norm_mlp_post=norm_mlp_post, mlp_w_up=mlp_w_up, mlp_w_down=mlp_w_down))
    bp, lp, _ = x_prompt.shape
    bs, ls, _ = x_sample.shape
    zeros = lambda *s: jnp.zeros(s, F32)
    prompt = _trunk(x_prompt, None, zeros(bp, GLA_HEADS, GLA_DK, GLA_DV),
                    zeros(bp, SSD_HEADS, SSD_HEAD_DIM, SSD_STATE), zeros(bp, SSD_CONV - 1, SSD_CONV_DIM),
                    q, _prompt_cfg(lp))
    n_pool = cache_sb_k.shape[1]
    pool_k = cache_sb_k.reshape(n_pool, PAGE_SIZE * SB_HEADS, SB_HEAD_DIM)
    pool_v = cache_sb_v.reshape(n_pool, PAGE_SIZE * SB_HEADS, SB_HEAD_DIM)
    sample = _trunk(x_sample, (pool_k, pool_v, page_table), state_gla[0], state_ssm[0], state_conv[0],
                    q, _sample_cfg(bs, ls, page_table.shape[1]))
    return (prompt[0], sample[0]) + prompt[1:] + sample[1:]
```

```python
import functools

import jax
import jax.numpy as jnp
from jax import lax
from jax.experimental import pallas as pl
from jax.experimental.pallas import tpu as pltpu

F32 = jnp.float32
BF16 = jnp.bfloat16

D_MODEL = 1024
RMS_EPS = 1e-6
D_FF = 4 * D_MODEL
GLA_HEADS = 4
GLA_DK = 64
GLA_DV = 128
GLA_GATE_RANK = 16
GLA_GATE_TEMP = 16.0
GLA_QK = GLA_HEADS * GLA_DK
GLA_V = GLA_HEADS * GLA_DV
SB_HEADS = 4
SB_HEAD_DIM = 128
SB_W = SB_HEADS * SB_HEAD_DIM
PAGE_SIZE = 128
EVEN_MAIN = 2 * GLA_QK + 2 * GLA_V + 3 * SB_W
SSD_INNER = 2048
SSD_HEAD_DIM = 64
SSD_HEADS = 32
SSD_GROUPS = 8
SSD_STATE = 128
SSD_CONV = 4
SSD_CONV_DIM = SSD_INNER + 2 * SSD_GROUPS * SSD_STATE
SSD_MAIN = SSD_CONV_DIM + SSD_INNER
LANES = 128
SIDE_W = LANES
VMEM_LIMIT = 48 * 1024 * 1024

NEG_INF = float("-inf")


def _cparams(sem):
    return pltpu.CompilerParams(dimension_semantics=sem, vmem_limit_bytes=VMEM_LIMIT)


def _dot(a, b):
    return jnp.dot(a.astype(BF16), b.astype(BF16), preferred_element_type=F32)


def _dot_nt(a, b):
    return lax.dot_general(a.astype(BF16), b.astype(BF16), (((1,), (1,)), ((), ())),
                           preferred_element_type=F32)


def _dot_tn(a, b):
    return lax.dot_general(a.astype(BF16), b.astype(BF16), (((0,), (0,)), ((), ())),
                           preferred_element_type=F32)


def _split3(x):
    hi = x.astype(BF16)
    r1 = x - hi.astype(F32)
    mid = r1.astype(BF16)
    lo = (r1 - mid.astype(F32)).astype(BF16)
    return hi, mid, lo


def _sel_dot(sel, x):
    hi, mid, lo = _split3(x)
    d = functools.partial(jnp.dot, preferred_element_type=F32)
    return d(sel, hi) + d(sel, mid) + d(sel, lo)


def _dot_sel(x, sel):
    hi, mid, lo = _split3(x)
    d = functools.partial(jnp.dot, preferred_element_type=F32)
    return d(hi, sel) + d(mid, sel) + d(lo, sel)


def _transpose_f32(x):
    n = x.shape[1]
    eye = (lax.broadcasted_iota(jnp.int32, (n, n), 0)
           == lax.broadcasted_iota(jnp.int32, (n, n), 1)).astype(BF16)
    hi, mid, lo = _split3(x)
    d = lambda p: lax.dot_general(eye, p, (((1,), (1,)), ((), ())), preferred_element_type=F32)
    return d(hi) + d(mid) + d(lo)


def _softplus_neg_abs(x):
    e = jnp.exp(-jnp.abs(x))
    u = 1.0 + e
    return jnp.where(u == 1.0, e, jnp.log(u) * (e / (u - 1.0)))


def _log_sigmoid(x):
    return jnp.minimum(x, 0.0) - _softplus_neg_abs(x)


def _softplus(x):
    return jnp.maximum(x, 0.0) + _softplus_neg_abs(x)


def _silu(x):
    return x / (1.0 + jnp.exp(-x))


def _rms_scale(x):
    return x * lax.rsqrt(jnp.mean(x * x, axis=-1, keepdims=True) + RMS_EPS)


def _norm_matmul_kernel(x_ref, g_ref, w_ref, ws_ref, o_ref, os_ref, xn_ref):
    j = pl.program_id(1)
    tn = o_ref.shape[1]

    @pl.when(j == 0)
    def _():
        xn_ref[...] = (_rms_scale(x_ref[...]) * g_ref[...]).astype(BF16)
        os_ref[...] = jnp.dot(xn_ref[...], ws_ref[...], preferred_element_type=F32)

    cols = pl.ds(pl.multiple_of(j * tn, tn), tn)
    o_ref[...] = jnp.dot(xn_ref[...], w_ref[:, cols], preferred_element_type=F32)


def _norm_matmul(x, g, w, w_side, tm, tn):
    m, d = x.shape
    n = w.shape[1]
    once = pl.Buffered(1)
    return pl.pallas_call(
        _norm_matmul_kernel,
        grid=(m // tm, n // tn),
        in_specs=[pl.BlockSpec((tm, d), lambda i, j: (i, 0)),
                  pl.BlockSpec((1, d), lambda i, j: (0, 0)),
                  pl.BlockSpec((d, n), lambda i, j: (0, 0), pipeline_mode=once),
                  pl.BlockSpec((d, SIDE_W), lambda i, j: (0, 0), pipeline_mode=once)],
        out_specs=[pl.BlockSpec((tm, tn), lambda i, j: (i, j)),
                   pl.BlockSpec((tm, SIDE_W), lambda i, j: (i, 0))],
        out_shape=[jax.ShapeDtypeStruct((m, n), F32),
                   jax.ShapeDtypeStruct((m, SIDE_W), F32)],
        scratch_shapes=[pltpu.VMEM((tm, d), BF16)],
        compiler_params=_cparams(("parallel", "arbitrary")),
        name="norm_matmul",
    )(x, g, w, w_side)


def _out_proj_kernel(*refs, widths):
    n = len(widths)
    part_refs, w_ref, x_ref, g_ref, o_ref = refs[:n], refs[n], refs[n + 1], refs[n + 2], refs[n + 3]
    acc = None
    off = 0
    for p_ref, wd in zip(part_refs, widths):
        t = jnp.dot(p_ref[...].astype(BF16), w_ref[off:off + wd, :], preferred_element_type=F32)
        acc = t if acc is None else acc + t
        off += wd
    o_ref[...] = x_ref[...] + _rms_scale(acc) * g_ref[...]


def _out_proj(parts, w, x, g, tm):
    m, d = x.shape
    widths = tuple(p.shape[1] for p in parts)
    in_specs = [pl.BlockSpec((tm, wd), lambda i: (i, 0)) for wd in widths]
    in_specs += [pl.BlockSpec(w.shape, lambda i: (0, 0)),
                 pl.BlockSpec((tm, d), lambda i: (i, 0)),
                 pl.BlockSpec((1, d), lambda i: (0, 0))]
    return pl.pallas_call(
        functools.partial(_out_proj_kernel, widths=widths),
        grid=(m // tm,),
        in_specs=in_specs,
        out_specs=pl.BlockSpec((tm, d), lambda i: (i, 0)),
        out_shape=jax.ShapeDtypeStruct((m, d), F32),
        compiler_params=_cparams(("parallel",)),
        name="out_proj",
    )(*parts, w, x, g)


def _mlp_kernel(x_ref, g1_ref, wu_ref, wd_ref, g2_ref, o_ref, xn_ref, acc_ref):
    f = pl.program_id(1)

    @pl.when(f == 0)
    def _():
        xn_ref[...] = (_rms_scale(x_ref[...]) * g1_ref[...]).astype(BF16)
        acc_ref[...] = jnp.zeros_like(acc_ref)

    h = jnp.dot(xn_ref[...], wu_ref[...], preferred_element_type=F32)
    h = jnp.square(jnp.maximum(h, 0.0))
    acc_ref[...] += jnp.dot(h.astype(BF16), wd_ref[...], preferred_element_type=F32)

    @pl.when(f == pl.num_programs(1) - 1)
    def _():
        o_ref[...] = x_ref[...] + _rms_scale(acc_ref[...]) * g2_ref[...]


def _mlp(x, g1, wu, wd, g2, tm, tf):
    m, d = x.shape
    ff = wu.shape[1]
    return pl.pallas_call(
        _mlp_kernel,
        grid=(m // tm, ff // tf),
        in_specs=[pl.BlockSpec((tm, d), lambda i, f: (i, 0)),
                  pl.BlockSpec((1, d), lambda i, f: (0, 0)),
                  pl.BlockSpec((d, tf), lambda i, f: (0, f)),
                  pl.BlockSpec((tf, d), lambda i, f: (f, 0)),
                  pl.BlockSpec((1, d), lambda i, f: (0, 0))],
        out_specs=pl.BlockSpec((tm, d), lambda i, f: (i, 0)),
        out_shape=jax.ShapeDtypeStruct((m, d), F32),
        scratch_shapes=[pltpu.VMEM((tm, d), BF16), pltpu.VMEM((tm, d), F32)],
        compiler_params=_cparams(("parallel", "arbitrary")),
        name="mlp",
    )(x, g1, wu, wd, g2)


def _gla_kernel(q_ref, k_ref, v_ref, r_ref, glr_ref, w2_ref, bg_ref, g_ref, s0_ref,
                o_ref, s_ref, st_ref, b_ref, hi_ref, lo_ref, *, sub, unroll):
    tb = pl.program_id(1)
    t_rows = q_ref.shape[0]
    nsub = t_rows // sub
    npair = GLA_HEADS // 2

    @pl.when(tb == 0)
    def _():
        st_ref[...] = s0_ref[0]

    la = _log_sigmoid(_dot(glr_ref[...], w2_ref[...]) + bg_ref[...]) * (1.0 / GLA_GATE_TEMP)
    row = lax.broadcasted_iota(jnp.int32, (t_rows, t_rows), 0)
    col = lax.broadcasted_iota(jnp.int32, (t_rows, t_rows), 1)
    tri = jnp.where(((row // sub) == (col // sub)) & (col <= row), 1.0, 0.0).astype(BF16)
    b_ref[...] = _sel_dot(tri, la)
    la_hi = la.astype(BF16)
    hi_ref[...] = la_hi
    lo_ref[...] = (la - la_hi.astype(F32)).astype(BF16)
    tn = lambda a, b: lax.dot_general(a, b, (((0,), (0,)), ((), ())), preferred_element_type=F32)
    ones_sub = jnp.ones((sub, LANES), BF16)

    er = lax.broadcasted_iota(jnp.int32, (GLA_QK, GLA_V), 0) // GLA_DK
    ec = lax.broadcasted_iota(jnp.int32, (GLA_QK, GLA_V), 1) // GLA_DV
    expand = jnp.where(er == ec, 1.0, 0.0).astype(BF16)
    t_iota = lax.broadcasted_iota(jnp.int32, (sub, GLA_QK), 0)
    zeros_blk = jnp.zeros((GLA_DK, GLA_DV), F32)

    def sub_chunk(i):
        rows = pl.ds(pl.multiple_of(i * sub, sub), sub)
        b_i = b_ref[rows, :]
        q_i = q_ref[rows, :] * (GLA_DK ** -0.5)
        k_i = k_ref[rows, :]
        v_i = v_ref[rows, :]
        dfull = jnp.exp(tn(hi_ref[rows, :], ones_sub) + tn(lo_ref[rows, :], ones_sub))
        qe = (q_i * jnp.exp(b_i)).astype(BF16)
        kd = (k_i * jnp.exp(b_i[sub - 1:sub, :] - b_i)).astype(BF16)
        v_bf = v_i.astype(BF16)
        inter = []
        for p in range(npair):
            ha, hb = 2 * p, 2 * p + 1
            s_a, s_b = st_ref[ha], st_ref[hb]
            lanes = slice(p * LANES, (p + 1) * LANES)
            w_pair = jnp.concatenate([jnp.concatenate([s_a, zeros_blk], axis=1),
                                      jnp.concatenate([zeros_blk, s_b], axis=1)], axis=0)
            inter.append(jnp.dot(qe[:, lanes], w_pair.astype(BF16), preferred_element_type=F32))
            u = tn(kd[:, lanes], v_bf[:, ha * GLA_DV:(hb + 1) * GLA_DV])
            st_ref[ha] = s_a * dfull[ha * GLA_DK:(ha + 1) * GLA_DK, :] + u[:GLA_DK, :GLA_DV]
            st_ref[hb] = s_b * dfull[hb * GLA_DK:(hb + 1) * GLA_DK, :] + u[GLA_DK:, GLA_DV:]
        prods = []
        for j in range(sub):
            e = jnp.exp(jnp.where(t_iota >= j, b_i - b_i[j:j + 1, :], NEG_INF))
            prods.append(q_i * e * k_i[j:j + 1, :])
        p_all = jnp.concatenate(prods, axis=0)
        r_all = _dot(p_all, expand)
        o = jnp.concatenate(inter, axis=1)
        for j in range(sub):
            o = o + r_all[j * sub:(j + 1) * sub, :] * v_i[j:j + 1, :]
        r_i = r_ref[rows, :]
        outs = []
        for h in range(GLA_HEADS):
            hv = slice(h * GLA_DV, (h + 1) * GLA_DV)
            outs.append(_rms_scale(o[:, hv]) * g_ref[...] * _silu(r_i[:, hv]))
        o_ref[rows, :] = jnp.concatenate(outs, axis=1)

    def body(it, carry):
        for u in range(unroll):
            sub_chunk(it * unroll + u)
        return carry

    lax.fori_loop(0, nsub // unroll, body, 0)

    @pl.when(tb == pl.num_programs(1) - 1)
    def _():
        s_ref[0] = st_ref[...]


def _gla(proj, side, w2, bg, g, s0, bsz, length, t_rows, sub):
    nt = length // t_rows
    m = bsz * length
    row = lambda b, t: b * nt + t
    return pl.pallas_call(
        functools.partial(_gla_kernel, sub=sub, unroll=2 if (t_rows // sub) % 2 == 0 else 1),
        grid=(bsz, nt),
        in_specs=[pl.BlockSpec((t_rows, GLA_QK), lambda b, t: (row(b, t), 0)),
                  pl.BlockSpec((t_rows, GLA_QK), lambda b, t: (row(b, t), 1)),
                  pl.BlockSpec((t_rows, GLA_V), lambda b, t: (row(b, t), 1)),
                  pl.BlockSpec((t_rows, GLA_V), lambda b, t: (row(b, t), 2)),
                  pl.BlockSpec((t_rows, SIDE_W), lambda b, t: (row(b, t), 0)),
                  pl.BlockSpec((SIDE_W, GLA_QK), lambda b, t: (0, 0)),
                  pl.BlockSpec((1, GLA_QK), lambda b, t: (0, 0)),
                  pl.BlockSpec((1, GLA_DV), lambda b, t: (0, 0)),
                  pl.BlockSpec((1, GLA_HEADS, GLA_DK, GLA_DV), lambda b, t: (b, 0, 0, 0))],
        out_specs=[pl.BlockSpec((t_rows, GLA_V), lambda b, t: (row(b, t), 0)),
                   pl.BlockSpec((1, GLA_HEADS, GLA_DK, GLA_DV), lambda b, t: (b, 0, 0, 0))],
        out_shape=[jax.ShapeDtypeStruct((m, GLA_V), F32),
                   jax.ShapeDtypeStruct((bsz, GLA_HEADS, GLA_DK, GLA_DV), F32)],
        scratch_shapes=[pltpu.VMEM((GLA_HEADS, GLA_DK, GLA_DV), F32),
                        pltpu.VMEM((t_rows, GLA_QK), F32),
                        pltpu.VMEM((t_rows, GLA_QK), BF16),
                        pltpu.VMEM((t_rows, GLA_QK), BF16)],
        compiler_params=_cparams(("parallel", "arbitrary")),
        name="gla",
    )(proj, proj, proj, proj, side, w2, bg, g, s0)


LOG2_E = 1.4426950408889634
SB_LOGIT_SCALE = SB_HEAD_DIM ** -0.5 * LOG2_E
SB_T_MAX = 64.0


def _suffix_matrix(tk):
    jr = lax.broadcasted_iota(jnp.int32, (tk, tk + LANES), 0)
    jc = lax.broadcasted_iota(jnp.int32, (tk, tk + LANES), 1)
    return jnp.where((jr > jc) | (jc >= tk), 1.0, 0.0).astype(BF16)


def _sb_logits(t, mask):
    n = jnp.where(t > SB_T_MAX, t, jnp.log2(1.0 + jnp.exp2(t)))
    log_b = t - n
    if mask is not None:
        n = jnp.where(mask, n, 0.0)
    return log_b, n.astype(BF16)


def _sb_weights(log_b, n, suffix, carry, mask):
    tk = log_b.shape[1]
    sums = jnp.dot(n, suffix, preferred_element_type=F32)
    a = jnp.exp2(log_b - sums[:, :tk] - jnp.concatenate([carry] * (tk // LANES), axis=1))
    if mask is not None:
        a = jnp.where(mask, a, 0.0)
    return a.astype(BF16), carry + sums[:, tk:]


def _sb_tile(q_bf, k_blk, v_blk, bias2, suffix, carry, acc, mask):
    log_b, n = _sb_logits(_dot_nt(q_bf, k_blk) * SB_LOGIT_SCALE + bias2, mask)
    a, carry = _sb_weights(log_b, n, suffix, carry, mask)
    return carry, acc + jnp.dot(a, v_blk.astype(BF16), preferred_element_type=F32)


def _sb_prompt_kernel(bias_ref, q_ref, k_ref, v_ref, o_ref, lb_ref, n_ref, a_ref, carry_ref, acc_ref,
                      *, tk, nsplit, unroll):
    h = pl.program_id(1)
    qi = pl.program_id(2)
    tq = q_ref.shape[0]
    th = tq // nsplit
    bias2 = bias_ref[h] * LOG2_E
    suffix = _suffix_matrix(tk)
    q0 = qi * tq
    q_bf = [q_ref[s * th:(s + 1) * th, :].astype(BF16) for s in range(nsplit)]
    zero = jnp.zeros((th, LANES), F32)
    carry = [zero] * nsplit
    acc = [zero] * nsplit
    row = lax.broadcasted_iota(jnp.int32, (th, tk), 0)
    col = lax.broadcasted_iota(jnp.int32, (th, tk), 1)

    for d in reversed(range(tq // tk)):
        k0 = pl.multiple_of(q0 + d * tk, tk)
        k_blk = k_ref[pl.ds(k0, tk), :]
        v_blk = v_ref[pl.ds(k0, tk), :]
        for s in range(nsplit):
            if d * tk >= (s + 1) * th:
                continue
            mask = None if (d + 1) * tk <= s * th else (col + d * tk) < (row + s * th)
            carry[s], acc[s] = _sb_tile(q_bf[s], k_blk, v_blk, bias2, suffix, carry[s], acc[s], mask)
    for s in range(nsplit):
        carry_ref[s * th:(s + 1) * th, :] = carry[s]
        acc_ref[s * th:(s + 1) * th, :] = acc[s]

    q_all = q_ref[...].astype(BF16)
    nfull = q0 // tk

    def key_rows(f):
        return pl.ds(pl.multiple_of(jnp.maximum(nfull - 1 - f, 0) * tk, tk), tk)

    def stage_logits(f, slot):
        log_b, n = _sb_logits(_dot_nt(q_all, k_ref[key_rows(f), :]) * SB_LOGIT_SCALE + bias2, None)
        lb_ref[slot] = log_b
        n_ref[slot] = n

    def stage_weights(slot):
        a, carry = _sb_weights(lb_ref[slot], n_ref[slot], suffix, carry_ref[...], None)
        a_ref[slot] = a
        carry_ref[...] = carry

    def stage_values(f, slot):
        acc_ref[...] += jnp.dot(a_ref[slot], v_ref[key_rows(f), :].astype(BF16), preferred_element_type=F32)

    for u in range(2 * unroll):
        stage_logits(u, u)
    for u in range(unroll):
        stage_weights(u)

    def body(it, c):
        for parity in range(2):
            cur = parity * unroll
            nxt = unroll - cur
            f0 = (2 * it + parity) * unroll
            for u in range(unroll):
                stage_logits(f0 + 2 * unroll + u, cur + u)
            for u in range(unroll):
                stage_weights(nxt + u)
            for u in range(unroll):
                stage_values(f0 + u, cur + u)
        return c

    lax.fori_loop(0, nfull // (2 * unroll), body, 0)
    o_ref[...] = acc_ref[...]


def _sb_prompt(proj, bias, bsz, length, tq, tk, nsplit, unroll):
    m = bsz * length
    nq = length // tq
    qcol = (2 * GLA_QK + 2 * GLA_V) // SB_HEAD_DIM
    kcol = qcol + SB_HEADS
    vcol = kcol + SB_HEADS
    return pl.pallas_call(
        functools.partial(_sb_prompt_kernel, tk=tk, nsplit=nsplit, unroll=unroll),
        grid=(bsz, SB_HEADS, nq),
        in_specs=[pl.BlockSpec(memory_space=pltpu.SMEM),
                  pl.BlockSpec((tq, SB_HEAD_DIM), lambda b, h, i: (b * nq + i, qcol + h)),
                  pl.BlockSpec((length, SB_HEAD_DIM), lambda b, h, i: (b, kcol + h)),
                  pl.BlockSpec((length, SB_HEAD_DIM), lambda b, h, i: (b, vcol + h))],
        out_specs=pl.BlockSpec((tq, SB_HEAD_DIM), lambda b, h, i: (b * nq + i, h)),
        out_shape=jax.ShapeDtypeStruct((m, SB_W), F32),
        scratch_shapes=[pltpu.VMEM((2 * unroll, tq, tk), F32),
                        pltpu.VMEM((2 * unroll, tq, tk), BF16),
                        pltpu.VMEM((2 * unroll, tq, tk), BF16),
                        pltpu.VMEM((tq, LANES), F32),
                        pltpu.VMEM((tq, SB_HEAD_DIM), F32)],
        compiler_params=_cparams(("parallel", "parallel", "arbitrary")),
        name="sb_prompt",
    )(bias, proj, proj, proj)


def _sb_paged_kernel(pt_ref, bias_ref, q_ref, kn_ref, vn_ref, *refs, pages_per_step):
    g = pages_per_step
    k_refs, v_refs = refs[:g], refs[g:2 * g]
    o_ref, carry_ref, acc_ref, kpad_ref, vpad_ref = refs[2 * g:]
    j = pl.program_id(1)
    lq = q_ref.shape[0]
    nrow = SB_HEADS * lq
    suffix = _suffix_matrix(PAGE_SIZE)
    rhead = lax.broadcasted_iota(jnp.int32, (nrow, PAGE_SIZE), 0) // lq
    bias2 = jnp.zeros((nrow, PAGE_SIZE), F32)
    for h in range(SB_HEADS):
        bias2 = jnp.where(rhead == h, bias_ref[h] * LOG2_E, bias2)
    q_bf = [q_ref[:, h * SB_HEAD_DIM:(h + 1) * SB_HEAD_DIM].astype(BF16) for h in range(SB_HEADS)]

    def tile(k_heads, v_heads, carry, acc, mask):
        z = jnp.concatenate([_dot_nt(q_bf[h], k_heads[h]) for h in range(SB_HEADS)], axis=0)
        log_b, n = _sb_logits(z * SB_LOGIT_SCALE + bias2, mask)
        a, carry = _sb_weights(log_b, n, suffix, carry, mask)
        acc = [acc[h] + _dot(a[h * lq:(h + 1) * lq, :], v_heads[h]) for h in range(SB_HEADS)]
        return carry, acc

    @pl.when(j == 0)
    def _():
        kpad_ref[...] = jnp.zeros_like(kpad_ref)
        vpad_ref[...] = jnp.zeros_like(vpad_ref)
        for h in range(SB_HEADS):
            hs = slice(h * SB_HEAD_DIM, (h + 1) * SB_HEAD_DIM)
            kpad_ref[h, 0:lq, :] = kn_ref[:, hs]
            vpad_ref[h, 0:lq, :] = vn_ref[:, hs]
        qidx = lax.broadcasted_iota(jnp.int32, (nrow, PAGE_SIZE), 0) % lq
        kidx = lax.broadcasted_iota(jnp.int32, (nrow, PAGE_SIZE), 1)
        carry, acc = tile([kpad_ref[h] for h in range(SB_HEADS)], [vpad_ref[h] for h in range(SB_HEADS)],
                          jnp.zeros((nrow, PAGE_SIZE), F32),
                          [jnp.zeros((lq, SB_HEAD_DIM), F32)] * SB_HEADS, kidx < qidx)
        carry_ref[...] = carry
        acc_ref[...] = jnp.concatenate(acc, axis=0)

    nchunk = (PAGE_SIZE * SB_HEADS) // LANES
    ntile = g * nchunk
    q_stack = jnp.concatenate(q_bf, axis=0)
    own = (lax.broadcasted_iota(jnp.int32, (nrow, LANES), 1) % SB_HEADS) == rhead
    tiles = []
    for p in range(g):
        z = _dot_nt(q_stack, k_refs[p][0])
        tiles += [z[:, c * LANES:(c + 1) * LANES] for c in reversed(range(nchunk))]
    z = jnp.concatenate(tiles, axis=0).reshape(ntile, nrow, LANES)
    log_b, n = _sb_logits(z * SB_LOGIT_SCALE + bias2[None], own[None])
    sums = jnp.dot(n.reshape(ntile * nrow, LANES), suffix, preferred_element_type=F32)
    sums = sums.reshape(ntile, nrow, 2 * LANES)
    carry = carry_ref[...]
    carries = []
    for t in range(ntile):
        carries.append(carry)
        carry = carry + sums[t, :, LANES:]
    carry_ref[...] = carry
    a = jnp.exp2(log_b - sums[:, :, :LANES] - jnp.stack(carries, axis=0))
    a = jnp.where(own[None], a, 0.0).astype(BF16)
    acc = acc_ref[...]
    for p in range(g):
        a_page = jnp.concatenate([a[p * nchunk + (nchunk - 1 - c)] for c in range(nchunk)], axis=1)
        acc = acc + jnp.dot(a_page, v_refs[p][0].astype(BF16), preferred_element_type=F32)
    acc_ref[...] = acc

    @pl.when(j == pl.num_programs(1) - 1)
    def _():
        for h in range(SB_HEADS):
            o_ref[:, h * SB_HEAD_DIM:(h + 1) * SB_HEAD_DIM] = acc[h * lq:(h + 1) * lq, :]


def _sb_paged(proj, bias, pool_k, pool_v, page_table, bsz, lq, pages_per_step):
    n_pages = page_table.shape[1]
    g = pages_per_step
    nsteps = n_pages // g
    qcol = (2 * GLA_QK + 2 * GLA_V) // SB_W

    def page_spec(p):
        return pl.BlockSpec((1, PAGE_SIZE * SB_HEADS, SB_HEAD_DIM),
                            lambda b, j, pt: (pt[b, n_pages - 1 - (j * g + p)], 0, 0))

    grid_spec = pltpu.PrefetchScalarGridSpec(
        num_scalar_prefetch=1,
        grid=(bsz, nsteps),
        in_specs=[pl.BlockSpec(memory_space=pltpu.SMEM),
                  pl.BlockSpec((lq, SB_W), lambda b, j, pt: (b, qcol)),
                  pl.BlockSpec((lq, SB_W), lambda b, j, pt: (b, qcol + 1)),
                  pl.BlockSpec((lq, SB_W), lambda b, j, pt: (b, qcol + 2))]
        + [page_spec(p) for p in range(g)] + [page_spec(p) for p in range(g)],
        out_specs=pl.BlockSpec((lq, SB_W), lambda b, j, pt: (b, 0)),
        scratch_shapes=[pltpu.VMEM((SB_HEADS * lq, PAGE_SIZE), F32),
                        pltpu.VMEM((SB_HEADS * lq, SB_HEAD_DIM), F32),
                        pltpu.VMEM((SB_HEADS, PAGE_SIZE, SB_HEAD_DIM), F32),
                        pltpu.VMEM((SB_HEADS, PAGE_SIZE, SB_HEAD_DIM), F32)])
    return pl.pallas_call(
        functools.partial(_sb_paged_kernel, pages_per_step=g),
        grid_spec=grid_spec,
        out_shape=jax.ShapeDtypeStruct((bsz * lq, SB_W), F32),
        compiler_params=_cparams(("parallel", "arbitrary")),
        name="sb_paged",
    )(page_table, bias, proj, proj, proj, *([pool_k] * g), *([pool_v] * g))


def _ssd_kernel(xbc_ref, z_ref, dtr_ref, cw_ref, cb_ref, dtb_ref, alog_ref, dsk_ref, ng_ref,
                conv0_ref, h0_ref, y_ref, hout_ref, xext_ref, hs_ref):
    ci = pl.program_id(1)
    c = xbc_ref.shape[0]
    tail = SSD_CONV - 1
    pad = 8

    @pl.when(ci == 0)
    def _():
        xext_ref[0:pad, :] = conv0_ref[0]
        hs_ref[...] = h0_ref[0]

    xext_ref[pad:pad + c, :] = xbc_ref[...]
    acc = cb_ref[...]
    for w in range(SSD_CONV):
        acc = acc + xext_ref[pl.ds(pad - tail + w, c), :] * cw_ref[w:w + 1, :]
    xc = _silu(acc)
    xext_ref[0:pad, :] = xext_ref[c:c + pad, :]

    dt = _softplus(dtr_ref[...] + dtb_ref[...])
    a = -jnp.exp(alog_ref[...])
    row = lax.broadcasted_iota(jnp.int32, (c, c), 0)
    col = lax.broadcasted_iota(jnp.int32, (c, c), 1)
    causal = col <= row
    cum = _sel_dot(jnp.where(causal, 1.0, 0.0).astype(BF16), dt * a)
    cum_t = _transpose_f32(cum)
    dt_t = _transpose_f32(dt)
    last = cum[c - 1:c, :]
    e_cum = jnp.exp(cum)
    dec = jnp.exp(last - cum) * dt
    e_last_t = jnp.exp(cum_t[:, c - 1:c])
    lo_lanes = lax.broadcasted_iota(jnp.int32, (c, LANES), 1) < SSD_HEAD_DIM
    lo_rows = lax.broadcasted_iota(jnp.int32, (LANES, LANES), 0) < SSD_HEAD_DIM
    hpg = SSD_HEADS // SSD_GROUPS

    for g in range(SSD_GROUPS):
        b_g = xc[:, SSD_INNER + g * SSD_STATE:SSD_INNER + (g + 1) * SSD_STATE]
        c_g = xc[:, SSD_INNER + (SSD_GROUPS + g) * SSD_STATE:SSD_INNER + (SSD_GROUPS + g + 1) * SSD_STATE]
        cb = _dot_nt(c_g, b_g)
        for pp in range(hpg // 2):
            pair = g * (hpg // 2) + pp
            h0, h1 = 2 * pair, 2 * pair + 1
            x_pair = xc[:, pair * LANES:(pair + 1) * LANES]
            hs = hs_ref[pair]
            y_pair = _dot_nt(c_g, hs) * jnp.where(lo_lanes, e_cum[:, h0:h0 + 1], e_cum[:, h1:h1 + 1])
            x_bf = x_pair.astype(BF16)
            intra = []
            for hh in (h0, h1):
                lmat = jnp.exp(jnp.where(causal, cum[:, hh:hh + 1] - cum_t[hh:hh + 1, :], NEG_INF))
                w = lmat * dt_t[hh:hh + 1, :] * cb
                intra.append(jnp.dot(w.astype(BF16), x_bf, preferred_element_type=F32))
            y_pair = y_pair + jnp.where(lo_lanes, intra[0], intra[1])
            xd = x_pair * jnp.where(lo_lanes, dec[:, h0:h0 + 1], dec[:, h1:h1 + 1])
            scale = jnp.where(lo_rows, e_last_t[h0:h0 + 1, :], e_last_t[h1:h1 + 1, :])
            upd = _dot(xd.T, b_g) if c % LANES == 0 else _dot_tn(xd, b_g)
            hs_ref[pair] = scale * hs + upd
            y_pair = y_pair + x_pair * dsk_ref[:, pair * LANES:(pair + 1) * LANES]
            zz = z_ref[:, pair * LANES:(pair + 1) * LANES]
            y_ref[:, pair * LANES:(pair + 1) * LANES] = y_pair * _silu(zz)

    gw = SSD_INNER // SSD_GROUPS
    for g in range(SSD_GROUPS):
        gs = slice(g * gw, (g + 1) * gw)
        y_ref[:, gs] = _rms_scale(y_ref[:, gs]) * ng_ref[:, gs]

    @pl.when(ci == pl.num_programs(1) - 1)
    def _():
        hout_ref[0] = hs_ref[...]


def _ssd(zx, side, cw, cb, dtb, alog, dsk, ng, conv0, h0, bsz, length, chunk):
    nc = length // chunk
    m = bsz * length
    row = lambda b, c: b * nc + c
    const = lambda b, c: (0, 0)
    npair = SSD_HEADS // 2
    return pl.pallas_call(
        _ssd_kernel,
        grid=(bsz, nc),
        in_specs=[pl.BlockSpec((chunk, SSD_CONV_DIM), lambda b, c: (row(b, c), 0)),
                  pl.BlockSpec((chunk, SSD_INNER), lambda b, c: (row(b, c), SSD_CONV_DIM // SSD_INNER)),
                  pl.BlockSpec((chunk, SIDE_W), lambda b, c: (row(b, c), 0)),
                  pl.BlockSpec((SSD_CONV, SSD_CONV_DIM), const),
                  pl.BlockSpec((1, SSD_CONV_DIM), const),
                  pl.BlockSpec((1, SIDE_W), const),
                  pl.BlockSpec((1, SIDE_W), const),
                  pl.BlockSpec((1, SSD_INNER), const),
                  pl.BlockSpec((1, SSD_INNER), const),
                  pl.BlockSpec((1, 8, SSD_CONV_DIM), lambda b, c: (b, 0, 0)),
                  pl.BlockSpec((1, npair, LANES, SSD_STATE), lambda b, c: (b, 0, 0, 0))],
        out_specs=[pl.BlockSpec((chunk, SSD_INNER), lambda b, c: (row(b, c), 0)),
                   pl.BlockSpec((1, npair, LANES, SSD_STATE), lambda b, c: (b, 0, 0, 0))],
        out_shape=[jax.ShapeDtypeStruct((m, SSD_INNER), F32),
                   jax.ShapeDtypeStruct((bsz, npair, LANES, SSD_STATE), F32)],
        scratch_shapes=[pltpu.VMEM((chunk + 8, SSD_CONV_DIM), F32),
                        pltpu.VMEM((npair, LANES, SSD_STATE), F32)],
        compiler_params=_cparams(("parallel", "arbitrary")),
        name="ssd",
    )(zx, zx, side, cw, cb, dtb, alog, dsk, ng, conv0, h0)


def _pad_cols(w, width):
    return jnp.pad(w, ((0, 0), (0, width - w.shape[1])))


def _row(v, width=None):
    v = v.reshape(1, -1).astype(F32)
    return v if width is None else _pad_cols(v, width)


def _prep_params(p):
    w_in = p["w_in_even"][0]
    n_gla = 2 * GLA_QK + GLA_V
    q = {}
    q["even_main"] = jnp.concatenate([w_in[:, :n_gla], w_in[:, n_gla + GLA_GATE_RANK:]], axis=1).astype(BF16)
    q["even_side"] = _pad_cols(w_in[:, n_gla:n_gla + GLA_GATE_RANK], SIDE_W).astype(BF16)
    q["gate_w2"] = jnp.pad(p["gla_w_gate2"][0], ((0, SIDE_W - GLA_GATE_RANK), (0, 0))).astype(BF16)
    q["gate_b"] = _row(p["gla_b_gate"][0])
    q["gla_g"] = _row(p["gla_norm_g"][0])
    q["sb_bias"] = p["sb_logit_bias"][0].astype(F32)
    q["even_out"] = p["w_out_even"][0].astype(BF16)
    w_ssd = p["ssd_w_in"][0]
    q["ssd_main"] = jnp.concatenate([w_ssd[:, SSD_INNER:SSD_INNER + SSD_CONV_DIM], w_ssd[:, :SSD_INNER]],
                                    axis=1).astype(BF16)
    q["ssd_side"] = _pad_cols(w_ssd[:, SSD_INNER + SSD_CONV_DIM:], SIDE_W).astype(BF16)
    q["conv_w"] = p["ssd_conv_w"][0].astype(F32)
    q["conv_b"] = _row(p["ssd_conv_b"][0])
    q["dt_bias"] = _row(p["ssd_dt_bias"][0], SIDE_W)
    q["a_log"] = _row(p["ssd_a_log"][0], SIDE_W)
    q["d_skip"] = _row(jnp.repeat(p["ssd_d"][0], SSD_HEAD_DIM))
    q["ssd_g"] = _row(p["ssd_norm_g"][0])
    q["ssd_out"] = p["ssd_w_out"][0].astype(BF16)
    for name in ("norm_mix_pre", "norm_mix_post", "norm_mlp_pre", "norm_mlp_post"):
        q[name] = [_row(p[name][li]) for li in range(2)]
    q["mlp_up"] = [p["mlp_w_up"][li].astype(BF16) for li in range(2)]
    q["mlp_down"] = [p["mlp_w_down"][li].astype(BF16) for li in range(2)]
    return q


def _trunk(x, paged, gla_s0, ssm_h0, conv0, q, cfg):
    bsz, length, d = x.shape
    m = bsz * length
    tm = cfg["tm"]
    x2 = x.reshape(m, d)

    proj, side = _norm_matmul(x2, q["norm_mix_pre"][0], q["even_main"], q["even_side"], cfg["tm_mm"], cfg["tn"])
    o_gla, gla_new = _gla(proj, side, q["gate_w2"], q["gate_b"], q["gla_g"], gla_s0,
                          bsz, length, cfg["gla_rows"], cfg["gla_sub"])
    if paged is None:
        o_sb = _sb_prompt(proj, q["sb_bias"], bsz, length, cfg["sb_tq"], cfg["sb_tk"], cfg["sb_split"],
                          cfg["sb_unroll"])
    else:
        o_sb = _sb_paged(proj, q["sb_bias"], paged[0], paged[1], paged[2], bsz, length, cfg["sb_pages"])
    kcol = 2 * GLA_QK + 2 * GLA_V + SB_W
    sb_k = proj[:, kcol:kcol + SB_W].reshape(bsz, length, SB_HEADS, SB_HEAD_DIM)
    sb_v = proj[:, kcol + SB_W:kcol + 2 * SB_W].reshape(bsz, length, SB_HEADS, SB_HEAD_DIM)
    x2 = _out_proj([o_gla, o_sb], q["even_out"], x2, q["norm_mix_post"][0], tm)
    x2 = _mlp(x2, q["norm_mlp_pre"][0], q["mlp_up"][0], q["mlp_down"][0], q["norm_mlp_post"][0],
              cfg["tm_mm"], 1024)

    zx, dtr = _norm_matmul(x2, q["norm_mix_pre"][1], q["ssd_main"], q["ssd_side"], cfg["tm_mm"], cfg["tn"])
    conv_pad = jnp.pad(conv0, ((0, 0), (8 - (SSD_CONV - 1), 0), (0, 0)))
    h0 = ssm_h0.reshape(bsz, SSD_HEADS // 2, 2 * SSD_HEAD_DIM, SSD_STATE)
    y, h_new = _ssd(zx, dtr, q["conv_w"], q["conv_b"], q["dt_bias"], q["a_log"], q["d_skip"], q["ssd_g"],
                    conv_pad, h0, bsz, length, cfg["ssd_chunk"])
    conv_new = zx.reshape(bsz, length, SSD_MAIN)[:, length - (SSD_CONV - 1):, :SSD_CONV_DIM]
    x2 = _out_proj([y], q["ssd_out"], x2, q["norm_mix_post"][1], tm)
    x2 = _mlp(x2, q["norm_mlp_pre"][1], q["mlp_up"][1], q["mlp_down"][1], q["norm_mlp_post"][1],
              cfg["tm_mm"], 1024)

    return (x2.reshape(bsz, length, d), sb_k[None], sb_v[None], gla_new[None],
            h_new.reshape(bsz, SSD_HEADS, SSD_HEAD_DIM, SSD_STATE)[None], conv_new[None])


def _prompt_cfg(length):
    return dict(tm=min(512, 2 * length), tn=1024, tm_mm=min(1024, 2 * length),
                gla_rows=min(256, length), gla_sub=16,
                sb_tq=min(512, length), sb_tk=128, sb_split=2, sb_unroll=2,
                ssd_chunk=min(128, length))


def _sample_cfg(bsz, length, n_pages):
    return dict(tm=bsz * length, tn=1024, tm_mm=bsz * length, gla_rows=length, gla_sub=length, sb_pages=min(16, n_pages),
                ssd_chunk=length)


def kernel(x_prompt, x_sample, cache_sb_k, cache_sb_v, state_gla, state_ssm, state_conv, page_table, w_in_even, gla_w_gate2, gla_b_gate, gla_norm_g, sb_logit_bias, w_out_even, ssd_w_in, ssd_conv_w, ssd_conv_b, ssd_dt_bias, ssd_a_log, ssd_d, ssd_norm_g, ssd_w_out, norm_mix_pre, norm_mix_post, norm_mlp_pre, norm_mlp_post, mlp_w_up, mlp_w_down):
    q = _prep_params(dict(
        w_in_even=w_in_even, gla_w_gate2=gla_w_gate2, gla_b_gate=gla_b_gate, gla_norm_g=gla_norm_g,
        sb_logit_bias=sb_logit_bias, w_out_even=w_out_even, ssd_w_in=ssd_w_in, ssd_conv_w=ssd_conv_w,
        ssd_conv_b=ssd_conv_b, ssd_dt_bias=ssd_dt_bias, ssd_a_log=ssd_a_log, ssd_d=ssd_d,
        ssd_norm_g=ssd_norm_g, ssd_w_out=ssd_w_out, norm_mix_pre=norm_mix_pre, norm_mix_post=norm_mix_post,
        norm_mlp_pre=norm_mlp_pre, norm_mlp_post=norm_mlp_post, mlp_w_up=mlp_w_up, mlp_w_down=mlp_w_down))
    bp, lp, _ = x_prompt.shape
    bs, ls, _ = x_sample.shape
    zeros = lambda *s: jnp.zeros(s, F32)
    prompt = _trunk(x_prompt, None, zeros(bp, GLA_HEADS, GLA_DK, GLA_DV),
                    zeros(bp, SSD_HEADS, SSD_HEAD_DIM, SSD_STATE), zeros(bp, SSD_CONV - 1, SSD_CONV_DIM),
                    q, _prompt_cfg(lp))
    n_pool = cache_sb_k.shape[1]
    pool_k = cache_sb_k.reshape(n_pool, PAGE_SIZE * SB_HEADS, SB_HEAD_DIM)
    pool_v = cache_sb_v.reshape(n_pool, PAGE_SIZE * SB_HEADS, SB_HEAD_DIM)
    sample = _trunk(x_sample, (pool_k, pool_v, page_table), state_gla[0], state_ssm[0], state_conv[0],
                    q, _sample_cfg(bs, ls, page_table.shape[1]))
    return (prompt[0], sample[0]) + prompt[1:] + sample[1:]
```

```python
import functools

import jax
import jax.numpy as jnp
from jax import lax
from jax.experimental import pallas as pl
from jax.experimental.pallas import tpu as pltpu

F32 = jnp.float32
BF16 = jnp.bfloat16

D_MODEL = 1024
RMS_EPS = 1e-6
D_FF = 4 * D_MODEL
GLA_HEADS = 4
GLA_DK = 64
GLA_DV = 128
GLA_GATE_RANK = 16
GLA_GATE_TEMP = 16.0
GLA_QK = GLA_HEADS * GLA_DK
GLA_V = GLA_HEADS * GLA_DV
SB_HEADS = 4
SB_HEAD_DIM = 128
SB_W = SB_HEADS * SB_HEAD_DIM
PAGE_SIZE = 128
EVEN_MAIN = 2 * GLA_QK + 2 * GLA_V + 3 * SB_W
SSD_INNER = 2048
SSD_HEAD_DIM = 64
SSD_HEADS = 32
SSD_GROUPS = 8
SSD_STATE = 128
SSD_CONV = 4
SSD_CONV_DIM = SSD_INNER + 2 * SSD_GROUPS * SSD_STATE
SSD_MAIN = SSD_CONV_DIM + SSD_INNER
LANES = 128
SIDE_W = LANES
VMEM_LIMIT = 48 * 1024 * 1024

NEG_INF = float("-inf")


def _act_dtype(rows):
    return BF16 if rows % 16 == 0 else F32


def _cparams(sem):
    return pltpu.CompilerParams(dimension_semantics=sem, vmem_limit_bytes=VMEM_LIMIT)


def _dot(a, b):
    return jnp.dot(a.astype(BF16), b.astype(BF16), preferred_element_type=F32)


def _dot_nt(a, b):
    return lax.dot_general(a.astype(BF16), b.astype(BF16), (((1,), (1,)), ((), ())),
                           preferred_element_type=F32)


def _dot_tn(a, b):
    return lax.dot_general(a.astype(BF16), b.astype(BF16), (((0,), (0,)), ((), ())),
                           preferred_element_type=F32)


def _split3(x):
    hi = x.astype(BF16)
    r1 = x - hi.astype(F32)
    mid = r1.astype(BF16)
    lo = (r1 - mid.astype(F32)).astype(BF16)
    return hi, mid, lo


def _sel_dot(sel, x):
    hi, mid, lo = _split3(x)
    d = functools.partial(jnp.dot, preferred_element_type=F32)
    return d(sel, hi) + d(sel, mid) + d(sel, lo)


def _dot_sel(x, sel):
    hi, mid, lo = _split3(x)
    d = functools.partial(jnp.dot, preferred_element_type=F32)
    return d(hi, sel) + d(mid, sel) + d(lo, sel)


def _transpose_f32(x):
    n = x.shape[1]
    eye = (lax.broadcasted_iota(jnp.int32, (n, n), 0)
           == lax.broadcasted_iota(jnp.int32, (n, n), 1)).astype(BF16)
    hi, mid, lo = _split3(x)
    d = lambda p: lax.dot_general(eye, p, (((1,), (1,)), ((), ())), preferred_element_type=F32)
    return d(hi) + d(mid) + d(lo)


def _softplus_neg_abs(x):
    e = jnp.exp(-jnp.abs(x))
    u = 1.0 + e
    return jnp.where(u == 1.0, e, jnp.log(u) * (e / (u - 1.0)))


def _log_sigmoid(x):
    return jnp.minimum(x, 0.0) - _softplus_neg_abs(x)


def _softplus(x):
    return jnp.maximum(x, 0.0) + _softplus_neg_abs(x)


def _silu(x):
    return x / (1.0 + jnp.exp(-x))


def _rms_scale(x):
    return x * lax.rsqrt(jnp.mean(x * x, axis=-1, keepdims=True) + RMS_EPS)


def _norm_matmul_kernel(x_ref, g_ref, w_ref, ws_ref, o_ref, os_ref, xn_ref):
    j = pl.program_id(1)
    tn = o_ref.shape[1]

    @pl.when(j == 0)
    def _():
        xn_ref[...] = (_rms_scale(x_ref[...]) * g_ref[...]).astype(BF16)
        os_ref[...] = jnp.dot(xn_ref[...], ws_ref[...], preferred_element_type=F32)

    cols = pl.ds(pl.multiple_of(j * tn, tn), tn)
    o_ref[...] = jnp.dot(xn_ref[...], w_ref[:, cols], preferred_element_type=F32)


def _norm_matmul(x, g, w, w_side, tm, tn):
    m, d = x.shape
    n = w.shape[1]
    once = pl.Buffered(1)
    return pl.pallas_call(
        _norm_matmul_kernel,
        grid=(m // tm, n // tn),
        in_specs=[pl.BlockSpec((tm, d), lambda i, j: (i, 0)),
                  pl.BlockSpec((1, d), lambda i, j: (0, 0)),
                  pl.BlockSpec((d, n), lambda i, j: (0, 0), pipeline_mode=once),
                  pl.BlockSpec((d, SIDE_W), lambda i, j: (0, 0), pipeline_mode=once)],
        out_specs=[pl.BlockSpec((tm, tn), lambda i, j: (i, j)),
                   pl.BlockSpec((tm, SIDE_W), lambda i, j: (i, 0))],
        out_shape=[jax.ShapeDtypeStruct((m, n), F32),
                   jax.ShapeDtypeStruct((m, SIDE_W), F32)],
        scratch_shapes=[pltpu.VMEM((tm, d), BF16)],
        compiler_params=_cparams(("parallel", "arbitrary")),
        name="norm_matmul",
    )(x, g, w, w_side)


def _out_proj_kernel(*refs, widths):
    n = len(widths)
    part_refs, w_ref, x_ref, g_ref, o_ref = refs[:n], refs[n], refs[n + 1], refs[n + 2], refs[n + 3]
    acc = None
    off = 0
    for p_ref, wd in zip(part_refs, widths):
        t = jnp.dot(p_ref[...].astype(BF16), w_ref[off:off + wd, :], preferred_element_type=F32)
        acc = t if acc is None else acc + t
        off += wd
    o_ref[...] = x_ref[...] + _rms_scale(acc) * g_ref[...]


def _out_proj(parts, w, x, g, tm):
    m, d = x.shape
    widths = tuple(p.shape[1] for p in parts)
    in_specs = [pl.BlockSpec((tm, wd), lambda i: (i, 0)) for wd in widths]
    in_specs += [pl.BlockSpec(w.shape, lambda i: (0, 0)),
                 pl.BlockSpec((tm, d), lambda i: (i, 0)),
                 pl.BlockSpec((1, d), lambda i: (0, 0))]
    return pl.pallas_call(
        functools.partial(_out_proj_kernel, widths=widths),
        grid=(m // tm,),
        in_specs=in_specs,
        out_specs=pl.BlockSpec((tm, d), lambda i: (i, 0)),
        out_shape=jax.ShapeDtypeStruct((m, d), F32),
        compiler_params=_cparams(("parallel",)),
        name="out_proj",
    )(*parts, w, x, g)


def _mlp_kernel(x_ref, g1_ref, wu_ref, wd_ref, g2_ref, o_ref, xn_ref, acc_ref):
    f = pl.program_id(1)

    @pl.when(f == 0)
    def _():
        xn_ref[...] = (_rms_scale(x_ref[...]) * g1_ref[...]).astype(BF16)
        acc_ref[...] = jnp.zeros_like(acc_ref)

    h = jnp.dot(xn_ref[...], wu_ref[...], preferred_element_type=F32)
    h = jnp.square(jnp.maximum(h, 0.0))
    acc_ref[...] += jnp.dot(h.astype(BF16), wd_ref[...], preferred_element_type=F32)

    @pl.when(f == pl.num_programs(1) - 1)
    def _():
        o_ref[...] = x_ref[...] + _rms_scale(acc_ref[...]) * g2_ref[...]


def _mlp(x, g1, wu, wd, g2, tm, tf):
    m, d = x.shape
    ff = wu.shape[1]
    return pl.pallas_call(
        _mlp_kernel,
        grid=(m // tm, ff // tf),
        in_specs=[pl.BlockSpec((tm, d), lambda i, f: (i, 0)),
                  pl.BlockSpec((1, d), lambda i, f: (0, 0)),
                  pl.BlockSpec((d, tf), lambda i, f: (0, f)),
                  pl.BlockSpec((tf, d), lambda i, f: (f, 0)),
                  pl.BlockSpec((1, d), lambda i, f: (0, 0))],
        out_specs=pl.BlockSpec((tm, d), lambda i, f: (i, 0)),
        out_shape=jax.ShapeDtypeStruct((m, d), F32),
        scratch_shapes=[pltpu.VMEM((tm, d), BF16), pltpu.VMEM((tm, d), F32)],
        compiler_params=_cparams(("parallel", "arbitrary")),
        name="mlp",
    )(x, g1, wu, wd, g2)


def _gla_kernel(q_ref, k_ref, v_ref, r_ref, glr_ref, w2_ref, bg_ref, g_ref, s0_ref,
                o_ref, s_ref, st_ref, b_ref, hi_ref, lo_ref, *, sub, unroll):
    tb = pl.program_id(1)
    t_rows = q_ref.shape[0]
    nsub = t_rows // sub
    npair = GLA_HEADS // 2

    @pl.when(tb == 0)
    def _():
        st_ref[...] = s0_ref[0]

    la = _log_sigmoid(_dot(glr_ref[...], w2_ref[...]) + bg_ref[...]) * (1.0 / GLA_GATE_TEMP)
    row = lax.broadcasted_iota(jnp.int32, (t_rows, t_rows), 0)
    col = lax.broadcasted_iota(jnp.int32, (t_rows, t_rows), 1)
    tri = jnp.where(((row // sub) == (col // sub)) & (col <= row), 1.0, 0.0).astype(BF16)
    b_ref[...] = _sel_dot(tri, la)
    la_hi = la.astype(BF16)
    hi_ref[...] = la_hi
    lo_ref[...] = (la - la_hi.astype(F32)).astype(BF16)
    tn = lambda a, b: lax.dot_general(a, b, (((0,), (0,)), ((), ())), preferred_element_type=F32)
    ones_sub = jnp.ones((sub, LANES), BF16)

    er = lax.broadcasted_iota(jnp.int32, (GLA_QK, GLA_V), 0) // GLA_DK
    ec = lax.broadcasted_iota(jnp.int32, (GLA_QK, GLA_V), 1) // GLA_DV
    expand = jnp.where(er == ec, 1.0, 0.0).astype(BF16)
    t_iota = lax.broadcasted_iota(jnp.int32, (sub, GLA_QK), 0)
    zeros_blk = jnp.zeros((GLA_DK, GLA_DV), F32)

    def sub_chunk(i):
        rows = pl.ds(pl.multiple_of(i * sub, sub), sub)
        b_i = b_ref[rows, :]
        q_i = q_ref[rows, :] * (GLA_DK ** -0.5)
        k_i = k_ref[rows, :]
        v_i = v_ref[rows, :]
        dfull = jnp.exp(tn(hi_ref[rows, :], ones_sub) + tn(lo_ref[rows, :], ones_sub))
        qe = (q_i * jnp.exp(b_i)).astype(BF16)
        kd = (k_i * jnp.exp(b_i[sub - 1:sub, :] - b_i)).astype(BF16)
        v_bf = v_i.astype(BF16)
        inter = []
        for p in range(npair):
            ha, hb = 2 * p, 2 * p + 1
            s_a, s_b = st_ref[ha], st_ref[hb]
            lanes = slice(p * LANES, (p + 1) * LANES)
            w_pair = jnp.concatenate([jnp.concatenate([s_a, zeros_blk], axis=1),
                                      jnp.concatenate([zeros_blk, s_b], axis=1)], axis=0)
            inter.append(jnp.dot(qe[:, lanes], w_pair.astype(BF16), preferred_element_type=F32))
            u = tn(kd[:, lanes], v_bf[:, ha * GLA_DV:(hb + 1) * GLA_DV])
            st_ref[ha] = s_a * dfull[ha * GLA_DK:(ha + 1) * GLA_DK, :] + u[:GLA_DK, :GLA_DV]
            st_ref[hb] = s_b * dfull[hb * GLA_DK:(hb + 1) * GLA_DK, :] + u[GLA_DK:, GLA_DV:]
        prods = []
        for j in range(sub):
            e = jnp.exp(jnp.where(t_iota >= j, b_i - b_i[j:j + 1, :], NEG_INF))
            prods.append(q_i * e * k_i[j:j + 1, :])
        p_all = jnp.concatenate(prods, axis=0)
        r_all = _dot(p_all, expand)
        o = jnp.concatenate(inter, axis=1)
        for j in range(sub):
            o = o + r_all[j * sub:(j + 1) * sub, :] * v_i[j:j + 1, :]
        r_i = r_ref[rows, :]
        outs = []
        for h in range(GLA_HEADS):
            hv = slice(h * GLA_DV, (h + 1) * GLA_DV)
            outs.append(_rms_scale(o[:, hv]) * g_ref[...] * _silu(r_i[:, hv]))
        o_ref[rows, :] = jnp.concatenate(outs, axis=1).astype(o_ref.dtype)

    def body(it, carry):
        for u in range(unroll):
            sub_chunk(it * unroll + u)
        return carry

    lax.fori_loop(0, nsub // unroll, body, 0)

    @pl.when(tb == pl.num_programs(1) - 1)
    def _():
        s_ref[0] = st_ref[...]


def _gla(proj, side, w2, bg, g, s0, bsz, length, t_rows, sub):
    nt = length // t_rows
    m = bsz * length
    row = lambda b, t: b * nt + t
    return pl.pallas_call(
        functools.partial(_gla_kernel, sub=sub, unroll=16 if (t_rows // sub) % 16 == 0 else 1),
        grid=(bsz, nt),
        in_specs=[pl.BlockSpec((t_rows, GLA_QK), lambda b, t: (row(b, t), 0)),
                  pl.BlockSpec((t_rows, GLA_QK), lambda b, t: (row(b, t), 1)),
                  pl.BlockSpec((t_rows, GLA_V), lambda b, t: (row(b, t), 1)),
                  pl.BlockSpec((t_rows, GLA_V), lambda b, t: (row(b, t), 2)),
                  pl.BlockSpec((t_rows, SIDE_W), lambda b, t: (row(b, t), 0)),
                  pl.BlockSpec((SIDE_W, GLA_QK), lambda b, t: (0, 0)),
                  pl.BlockSpec((1, GLA_QK), lambda b, t: (0, 0)),
                  pl.BlockSpec((1, GLA_DV), lambda b, t: (0, 0)),
                  pl.BlockSpec((1, GLA_HEADS, GLA_DK, GLA_DV), lambda b, t: (b, 0, 0, 0))],
        out_specs=[pl.BlockSpec((t_rows, GLA_V), lambda b, t: (row(b, t), 0)),
                   pl.BlockSpec((1, GLA_HEADS, GLA_DK, GLA_DV), lambda b, t: (b, 0, 0, 0))],
        out_shape=[jax.ShapeDtypeStruct((m, GLA_V), _act_dtype(sub)),
                   jax.ShapeDtypeStruct((bsz, GLA_HEADS, GLA_DK, GLA_DV), F32)],
        scratch_shapes=[pltpu.VMEM((GLA_HEADS, GLA_DK, GLA_DV), F32),
                        pltpu.VMEM((t_rows, GLA_QK), F32),
                        pltpu.VMEM((t_rows, GLA_QK), BF16),
                        pltpu.VMEM((t_rows, GLA_QK), BF16)],
        compiler_params=_cparams(("parallel", "arbitrary")),
        name="gla",
    )(proj, proj, proj, proj, side, w2, bg, g, s0)


LOG2_E = 1.4426950408889634
SB_LOGIT_SCALE = SB_HEAD_DIM ** -0.5 * LOG2_E
SB_T_MAX = 64.0


def _suffix_matrix(tk):
    jr = lax.broadcasted_iota(jnp.int32, (tk, tk + LANES), 0)
    jc = lax.broadcasted_iota(jnp.int32, (tk, tk + LANES), 1)
    return jnp.where((jr > jc) | (jc >= tk), 1.0, 0.0).astype(BF16)


def _sb_logits(t, mask):
    n = jnp.where(t > SB_T_MAX, t, jnp.log2(1.0 + jnp.exp2(t)))
    log_b = t - n
    if mask is not None:
        n = jnp.where(mask, n, 0.0)
    return log_b, n.astype(BF16)


def _sb_weights(log_b, n, suffix, carry, mask):
    tk = log_b.shape[1]
    sums = jnp.dot(n, suffix, preferred_element_type=F32)
    a = jnp.exp2(log_b - sums[:, :tk] - jnp.concatenate([carry] * (tk // LANES), axis=1))
    if mask is not None:
        a = jnp.where(mask, a, 0.0)
    return a.astype(BF16), carry + sums[:, tk:]


def _sb_tile(q_bf, k_blk, v_blk, bias2, suffix, carry, acc, mask):
    log_b, n = _sb_logits(_dot_nt(q_bf, k_blk) * SB_LOGIT_SCALE + bias2, mask)
    a, carry = _sb_weights(log_b, n, suffix, carry, mask)
    return carry, acc + jnp.dot(a, v_blk.astype(BF16), preferred_element_type=F32)


def _sb_prompt_kernel(bias_ref, q_ref, k_ref, v_ref, o_ref, lb_ref, n_ref, a_ref, carry_ref, acc_ref,
                      *, tk, nsplit, unroll):
    h = pl.program_id(1)
    qi = pl.program_id(2)
    tq = q_ref.shape[0]
    th = tq // nsplit
    bias2 = bias_ref[h] * LOG2_E
    suffix = _suffix_matrix(tk)
    q0 = qi * tq
    q_bf = [q_ref[s * th:(s + 1) * th, :].astype(BF16) for s in range(nsplit)]
    zero = jnp.zeros((th, LANES), F32)
    carry = [zero] * nsplit
    acc = [zero] * nsplit
    row = lax.broadcasted_iota(jnp.int32, (th, tk), 0)
    col = lax.broadcasted_iota(jnp.int32, (th, tk), 1)

    for d in reversed(range(tq // tk)):
        k0 = pl.multiple_of(q0 + d * tk, tk)
        k_blk = k_ref[pl.ds(k0, tk), :]
        v_blk = v_ref[pl.ds(k0, tk), :]
        for s in range(nsplit):
            if d * tk >= (s + 1) * th:
                continue
            mask = None if (d + 1) * tk <= s * th else (col + d * tk) < (row + s * th)
            carry[s], acc[s] = _sb_tile(q_bf[s], k_blk, v_blk, bias2, suffix, carry[s], acc[s], mask)
    for s in range(nsplit):
        carry_ref[s * th:(s + 1) * th, :] = carry[s]
        acc_ref[s * th:(s + 1) * th, :] = acc[s]

    q_all = q_ref[...].astype(BF16)
    nfull = q0 // tk

    def key_rows(f):
        return pl.ds(pl.multiple_of(jnp.maximum(nfull - 1 - f, 0) * tk, tk), tk)

    def stage_logits(f, slot):
        log_b, n = _sb_logits(_dot_nt(q_all, k_ref[key_rows(f), :]) * SB_LOGIT_SCALE + bias2, None)
        lb_ref[slot] = log_b
        n_ref[slot] = n

    def stage_weights(slot):
        a, carry = _sb_weights(lb_ref[slot], n_ref[slot], suffix, carry_ref[...], None)
        a_ref[slot] = a
        carry_ref[...] = carry

    def stage_values(f, slot):
        acc_ref[...] += jnp.dot(a_ref[slot], v_ref[key_rows(f), :].astype(BF16), preferred_element_type=F32)

    for u in range(2 * unroll):
        stage_logits(u, u)
    for u in range(unroll):
        stage_weights(u)

    def body(it, c):
        for parity in range(2):
            cur = parity * unroll
            nxt = unroll - cur
            f0 = (2 * it + parity) * unroll
            for u in range(unroll):
                stage_logits(f0 + 2 * unroll + u, cur + u)
            for u in range(unroll):
                stage_weights(nxt + u)
            for u in range(unroll):
                stage_values(f0 + u, cur + u)
        return c

    lax.fori_loop(0, nfull // (2 * unroll), body, 0)
    o_ref[...] = acc_ref[...].astype(o_ref.dtype)


def _sb_prompt(proj, bias, bsz, length, tq, tk, nsplit, unroll):
    m = bsz * length
    nq = length // tq
    qcol = (2 * GLA_QK + 2 * GLA_V) // SB_HEAD_DIM
    kcol = qcol + SB_HEADS
    vcol = kcol + SB_HEADS
    return pl.pallas_call(
        functools.partial(_sb_prompt_kernel, tk=tk, nsplit=nsplit, unroll=unroll),
        grid=(bsz, SB_HEADS, nq),
        in_specs=[pl.BlockSpec(memory_space=pltpu.SMEM),
                  pl.BlockSpec((tq, SB_HEAD_DIM), lambda b, h, i: (b * nq + i, qcol + h)),
                  pl.BlockSpec((length, SB_HEAD_DIM), lambda b, h, i: (b, kcol + h)),
                  pl.BlockSpec((length, SB_HEAD_DIM), lambda b, h, i: (b, vcol + h))],
        out_specs=pl.BlockSpec((tq, SB_HEAD_DIM), lambda b, h, i: (b * nq + i, h)),
        out_shape=jax.ShapeDtypeStruct((m, SB_W), _act_dtype(tq)),
        scratch_shapes=[pltpu.VMEM((2 * unroll, tq, tk), F32),
                        pltpu.VMEM((2 * unroll, tq, tk), BF16),
                        pltpu.VMEM((2 * unroll, tq, tk), BF16),
                        pltpu.VMEM((tq, LANES), F32),
                        pltpu.VMEM((tq, SB_HEAD_DIM), F32)],
        compiler_params=_cparams(("parallel", "parallel", "arbitrary")),
        name="sb_prompt",
    )(bias, proj, proj, proj)


def _sb_paged_kernel(pt_ref, bias_ref, q_ref, kn_ref, vn_ref, *refs, pages_per_step):
    g = pages_per_step
    k_refs, v_refs = refs[:g], refs[g:2 * g]
    o_ref, carry_ref, acc_ref, kpad_ref, vpad_ref = refs[2 * g:]
    j = pl.program_id(1)
    lq = q_ref.shape[0]
    nrow = SB_HEADS * lq
    suffix = _suffix_matrix(PAGE_SIZE)
    rhead = lax.broadcasted_iota(jnp.int32, (nrow, PAGE_SIZE), 0) // lq
    bias2 = jnp.zeros((nrow, PAGE_SIZE), F32)
    for h in range(SB_HEADS):
        bias2 = jnp.where(rhead == h, bias_ref[h] * LOG2_E, bias2)
    q_bf = [q_ref[:, h * SB_HEAD_DIM:(h + 1) * SB_HEAD_DIM].astype(BF16) for h in range(SB_HEADS)]

    def tile(k_heads, v_heads, carry, acc, mask):
        z = jnp.concatenate([_dot_nt(q_bf[h], k_heads[h]) for h in range(SB_HEADS)], axis=0)
        log_b, n = _sb_logits(z * SB_LOGIT_SCALE + bias2, mask)
        a, carry = _sb_weights(log_b, n, suffix, carry, mask)
        acc = [acc[h] + _dot(a[h * lq:(h + 1) * lq, :], v_heads[h]) for h in range(SB_HEADS)]
        return carry, acc

    @pl.when(j == 0)
    def _():
        kpad_ref[...] = jnp.zeros_like(kpad_ref)
        vpad_ref[...] = jnp.zeros_like(vpad_ref)
        for h in range(SB_HEADS):
            hs = slice(h * SB_HEAD_DIM, (h + 1) * SB_HEAD_DIM)
            kpad_ref[h, 0:lq, :] = kn_ref[:, hs]
            vpad_ref[h, 0:lq, :] = vn_ref[:, hs]
        qidx = lax.broadcasted_iota(jnp.int32, (nrow, PAGE_SIZE), 0) % lq
        kidx = lax.broadcasted_iota(jnp.int32, (nrow, PAGE_SIZE), 1)
        carry, acc = tile([kpad_ref[h] for h in range(SB_HEADS)], [vpad_ref[h] for h in range(SB_HEADS)],
                          jnp.zeros((nrow, PAGE_SIZE), F32),
                          [jnp.zeros((lq, SB_HEAD_DIM), F32)] * SB_HEADS, kidx < qidx)
        carry_ref[...] = carry
        acc_ref[...] = jnp.concatenate(acc, axis=0)

    nchunk = (PAGE_SIZE * SB_HEADS) // LANES
    ntile = g * nchunk
    q_stack = jnp.concatenate(q_bf, axis=0)
    own = (lax.broadcasted_iota(jnp.int32, (nrow, LANES), 1) % SB_HEADS) == rhead
    tiles = []
    for p in range(g):
        z = _dot_nt(q_stack, k_refs[p][0])
        tiles += [z[:, c * LANES:(c + 1) * LANES] for c in reversed(range(nchunk))]
    z = jnp.concatenate(tiles, axis=0).reshape(ntile, nrow, LANES)
    log_b, n = _sb_logits(z * SB_LOGIT_SCALE + bias2[None], own[None])
    sums = jnp.dot(n.reshape(ntile * nrow, LANES), suffix, preferred_element_type=F32)
    sums = sums.reshape(ntile, nrow, 2 * LANES)
    carry = carry_ref[...]
    carries = []
    for t in range(ntile):
        carries.append(carry)
        carry = carry + sums[t, :, LANES:]
    carry_ref[...] = carry
    a = jnp.exp2(log_b - sums[:, :, :LANES] - jnp.stack(carries, axis=0))
    a = jnp.where(own[None], a, 0.0).astype(BF16)
    acc = acc_ref[...]
    for p in range(g):
        a_page = jnp.concatenate([a[p * nchunk + (nchunk - 1 - c)] for c in range(nchunk)], axis=1)
        acc = acc + jnp.dot(a_page, v_refs[p][0].astype(BF16), preferred_element_type=F32)
    acc_ref[...] = acc

    @pl.when(j == pl.num_programs(1) - 1)
    def _():
        for h in range(SB_HEADS):
            o_ref[:, h * SB_HEAD_DIM:(h + 1) * SB_HEAD_DIM] = acc[h * lq:(h + 1) * lq, :].astype(o_ref.dtype)


def _sb_paged(proj, bias, pool_k, pool_v, page_table, bsz, lq, pages_per_step):
    n_pages = page_table.shape[1]
    g = pages_per_step
    nsteps = n_pages // g
    qcol = (2 * GLA_QK + 2 * GLA_V) // SB_W

    def page_spec(p):
        return pl.BlockSpec((1, PAGE_SIZE * SB_HEADS, SB_HEAD_DIM),
                            lambda b, j, pt: (pt[b, n_pages - 1 - (j * g + p)], 0, 0))

    grid_spec = pltpu.PrefetchScalarGridSpec(
        num_scalar_prefetch=1,
        grid=(bsz, nsteps),
        in_specs=[pl.BlockSpec(memory_space=pltpu.SMEM),
                  pl.BlockSpec((lq, SB_W), lambda b, j, pt: (b, qcol)),
                  pl.BlockSpec((lq, SB_W), lambda b, j, pt: (b, qcol + 1)),
                  pl.BlockSpec((lq, SB_W), lambda b, j, pt: (b, qcol + 2))]
        + [page_spec(p) for p in range(g)] + [page_spec(p) for p in range(g)],
        out_specs=pl.BlockSpec((lq, SB_W), lambda b, j, pt: (b, 0)),
        scratch_shapes=[pltpu.VMEM((SB_HEADS * lq, PAGE_SIZE), F32),
                        pltpu.VMEM((SB_HEADS * lq, SB_HEAD_DIM), F32),
                        pltpu.VMEM((SB_HEADS, PAGE_SIZE, SB_HEAD_DIM), F32),
                        pltpu.VMEM((SB_HEADS, PAGE_SIZE, SB_HEAD_DIM), F32)])
    return pl.pallas_call(
        functools.partial(_sb_paged_kernel, pages_per_step=g),
        grid_spec=grid_spec,
        out_shape=jax.ShapeDtypeStruct((bsz * lq, SB_W), _act_dtype(lq)),
        compiler_params=_cparams(("parallel", "arbitrary")),
        name="sb_paged",
    )(page_table, bias, proj, proj, proj, *([pool_k] * g), *([pool_v] * g))


def _ssd_kernel(xbc_ref, z_ref, dtr_ref, cw_ref, cb_ref, dtb_ref, alog_ref, dsk_ref, ng_ref,
                conv0_ref, h0_ref, y_ref, hout_ref, xext_ref, hs_ref, ypre_ref):
    ci = pl.program_id(1)
    c = xbc_ref.shape[0]
    tail = SSD_CONV - 1
    pad = 8

    @pl.when(ci == 0)
    def _():
        xext_ref[0:pad, :] = conv0_ref[0]
        hs_ref[...] = h0_ref[0]

    xext_ref[pad:pad + c, :] = xbc_ref[...]
    acc = cb_ref[...]
    for w in range(SSD_CONV):
        acc = acc + xext_ref[pl.ds(pad - tail + w, c), :] * cw_ref[w:w + 1, :]
    xc = _silu(acc)
    xext_ref[0:pad, :] = xext_ref[c:c + pad, :]

    dt = _softplus(dtr_ref[...] + dtb_ref[...])
    a = -jnp.exp(alog_ref[...])
    row = lax.broadcasted_iota(jnp.int32, (c, c), 0)
    col = lax.broadcasted_iota(jnp.int32, (c, c), 1)
    causal = col <= row
    cum = _sel_dot(jnp.where(causal, 1.0, 0.0).astype(BF16), dt * a)
    cum_t = _transpose_f32(cum)
    dt_t = _transpose_f32(dt)
    last = cum[c - 1:c, :]
    e_cum = jnp.exp(cum)
    dec = jnp.exp(last - cum) * dt
    e_last_t = jnp.exp(cum_t[:, c - 1:c])
    lo_lanes = lax.broadcasted_iota(jnp.int32, (c, LANES), 1) < SSD_HEAD_DIM
    lo_rows = lax.broadcasted_iota(jnp.int32, (LANES, LANES), 0) < SSD_HEAD_DIM
    hpg = SSD_HEADS // SSD_GROUPS

    for g in range(SSD_GROUPS):
        b_g = xc[:, SSD_INNER + g * SSD_STATE:SSD_INNER + (g + 1) * SSD_STATE]
        c_g = xc[:, SSD_INNER + (SSD_GROUPS + g) * SSD_STATE:SSD_INNER + (SSD_GROUPS + g + 1) * SSD_STATE]
        cb = _dot_nt(c_g, b_g)
        for pp in range(hpg // 2):
            pair = g * (hpg // 2) + pp
            h0, h1 = 2 * pair, 2 * pair + 1
            x_pair = xc[:, pair * LANES:(pair + 1) * LANES]
            hs = hs_ref[pair]
            y_pair = _dot_nt(c_g, hs) * jnp.where(lo_lanes, e_cum[:, h0:h0 + 1], e_cum[:, h1:h1 + 1])
            x_bf = x_pair.astype(BF16)
            intra = []
            for hh in (h0, h1):
                lmat = jnp.exp(jnp.where(causal, cum[:, hh:hh + 1] - cum_t[hh:hh + 1, :], NEG_INF))
                w = lmat * dt_t[hh:hh + 1, :] * cb
                intra.append(jnp.dot(w.astype(BF16), x_bf, preferred_element_type=F32))
            y_pair = y_pair + jnp.where(lo_lanes, intra[0], intra[1])
            xd = x_pair * jnp.where(lo_lanes, dec[:, h0:h0 + 1], dec[:, h1:h1 + 1])
            scale = jnp.where(lo_rows, e_last_t[h0:h0 + 1, :], e_last_t[h1:h1 + 1, :])
            upd = _dot(xd.T, b_g) if c % LANES == 0 else _dot_tn(xd, b_g)
            hs_ref[pair] = scale * hs + upd
            y_pair = y_pair + x_pair * dsk_ref[:, pair * LANES:(pair + 1) * LANES]
            zz = z_ref[:, pair * LANES:(pair + 1) * LANES]
            ypre_ref[:, pair * LANES:(pair + 1) * LANES] = y_pair * _silu(zz)

    gw = SSD_INNER // SSD_GROUPS
    for g in range(SSD_GROUPS):
        gs = slice(g * gw, (g + 1) * gw)
        y_ref[:, gs] = (_rms_scale(ypre_ref[:, gs]) * ng_ref[:, gs]).astype(y_ref.dtype)

    @pl.when(ci == pl.num_programs(1) - 1)
    def _():
        hout_ref[0] = hs_ref[...]


def _ssd(zx, side, cw, cb, dtb, alog, dsk, ng, conv0, h0, bsz, length, chunk):
    nc = length // chunk
    m = bsz * length
    row = lambda b, c: b * nc + c
    const = lambda b, c: (0, 0)
    npair = SSD_HEADS // 2
    return pl.pallas_call(
        _ssd_kernel,
        grid=(bsz, nc),
        in_specs=[pl.BlockSpec((chunk, SSD_CONV_DIM), lambda b, c: (row(b, c), 0)),
                  pl.BlockSpec((chunk, SSD_INNER), lambda b, c: (row(b, c), SSD_CONV_DIM // SSD_INNER)),
                  pl.BlockSpec((chunk, SIDE_W), lambda b, c: (row(b, c), 0)),
                  pl.BlockSpec((SSD_CONV, SSD_CONV_DIM), const),
                  pl.BlockSpec((1, SSD_CONV_DIM), const),
                  pl.BlockSpec((1, SIDE_W), const),
                  pl.BlockSpec((1, SIDE_W), const),
                  pl.BlockSpec((1, SSD_INNER), const),
                  pl.BlockSpec((1, SSD_INNER), const),
                  pl.BlockSpec((1, 8, SSD_CONV_DIM), lambda b, c: (b, 0, 0)),
                  pl.BlockSpec((1, npair, LANES, SSD_STATE), lambda b, c: (b, 0, 0, 0))],
        out_specs=[pl.BlockSpec((chunk, SSD_INNER), lambda b, c: (row(b, c), 0)),
                   pl.BlockSpec((1, npair, LANES, SSD_STATE), lambda b, c: (b, 0, 0, 0))],
        out_shape=[jax.ShapeDtypeStruct((m, SSD_INNER), _act_dtype(chunk)),
                   jax.ShapeDtypeStruct((bsz, npair, LANES, SSD_STATE), F32)],
        scratch_shapes=[pltpu.VMEM((chunk + 8, SSD_CONV_DIM), F32),
                        pltpu.VMEM((npair, LANES, SSD_STATE), F32),
                        pltpu.VMEM((chunk, SSD_INNER), F32)],
        compiler_params=_cparams(("parallel", "arbitrary")),
        name="ssd",
    )(zx, zx, side, cw, cb, dtb, alog, dsk, ng, conv0, h0)


def _pad_cols(w, width):
    return jnp.pad(w, ((0, 0), (0, width - w.shape[1])))


def _row(v, width=None):
    v = v.reshape(1, -1).astype(F32)
    return v if width is None else _pad_cols(v, width)


def _prep_params(p):
    w_in = p["w_in_even"][0]
    n_gla = 2 * GLA_QK + GLA_V
    q = {}
    q["even_main"] = jnp.concatenate([w_in[:, :n_gla], w_in[:, n_gla + GLA_GATE_RANK:]], axis=1).astype(BF16)
    q["even_side"] = _pad_cols(w_in[:, n_gla:n_gla + GLA_GATE_RANK], SIDE_W).astype(BF16)
    q["gate_w2"] = jnp.pad(p["gla_w_gate2"][0], ((0, SIDE_W - GLA_GATE_RANK), (0, 0))).astype(BF16)
    q["gate_b"] = _row(p["gla_b_gate"][0])
    q["gla_g"] = _row(p["gla_norm_g"][0])
    q["sb_bias"] = p["sb_logit_bias"][0].astype(F32)
    q["even_out"] = p["w_out_even"][0].astype(BF16)
    w_ssd = p["ssd_w_in"][0]
    q["ssd_main"] = jnp.concatenate([w_ssd[:, SSD_INNER:SSD_INNER + SSD_CONV_DIM], w_ssd[:, :SSD_INNER]],
                                    axis=1).astype(BF16)
    q["ssd_side"] = _pad_cols(w_ssd[:, SSD_INNER + SSD_CONV_DIM:], SIDE_W).astype(BF16)
    q["conv_w"] = p["ssd_conv_w"][0].astype(F32)
    q["conv_b"] = _row(p["ssd_conv_b"][0])
    q["dt_bias"] = _row(p["ssd_dt_bias"][0], SIDE_W)
    q["a_log"] = _row(p["ssd_a_log"][0], SIDE_W)
    q["d_skip"] = _row(jnp.repeat(p["ssd_d"][0], SSD_HEAD_DIM))
    q["ssd_g"] = _row(p["ssd_norm_g"][0])
    q["ssd_out"] = p["ssd_w_out"][0].astype(BF16)
    for name in ("norm_mix_pre", "norm_mix_post", "norm_mlp_pre", "norm_mlp_post"):
        q[name] = [_row(p[name][li]) for li in range(2)]
    q["mlp_up"] = [p["mlp_w_up"][li].astype(BF16) for li in range(2)]
    q["mlp_down"] = [p["mlp_w_down"][li].astype(BF16) for li in range(2)]
    return q


def _trunk(x, paged, gla_s0, ssm_h0, conv0, q, cfg):
    bsz, length, d = x.shape
    m = bsz * length
    x2 = x.reshape(m, d)

    proj, side = _norm_matmul(x2, q["norm_mix_pre"][0], q["even_main"], q["even_side"], cfg["tm_mm"], cfg["tn_even"])
    o_gla, gla_new = _gla(proj, side, q["gate_w2"], q["gate_b"], q["gla_g"], gla_s0,
                          bsz, length, cfg["gla_rows"], cfg["gla_sub"])
    if paged is None:
        o_sb = _sb_prompt(proj, q["sb_bias"], bsz, length, cfg["sb_tq"], cfg["sb_tk"], cfg["sb_split"],
                          cfg["sb_unroll"])
    else:
        o_sb = _sb_paged(proj, q["sb_bias"], paged[0], paged[1], paged[2], bsz, length, cfg["sb_pages"])
    kcol = 2 * GLA_QK + 2 * GLA_V + SB_W
    sb_k = proj[:, kcol:kcol + SB_W].reshape(bsz, length, SB_HEADS, SB_HEAD_DIM)
    sb_v = proj[:, kcol + SB_W:kcol + 2 * SB_W].reshape(bsz, length, SB_HEADS, SB_HEAD_DIM)
    x2 = _out_proj([o_gla, o_sb], q["even_out"], x2, q["norm_mix_post"][0], cfg["tm_mm"])
    x2 = _mlp(x2, q["norm_mlp_pre"][0], q["mlp_up"][0], q["mlp_down"][0], q["norm_mlp_post"][0],
              cfg["tm_mm"], 1024)

    zx, dtr = _norm_matmul(x2, q["norm_mix_pre"][1], q["ssd_main"], q["ssd_side"], cfg["tm_mm"], cfg["tn_ssd"])
    conv_pad = jnp.pad(conv0, ((0, 0), (8 - (SSD_CONV - 1), 0), (0, 0)))
    h0 = ssm_h0.reshape(bsz, SSD_HEADS // 2, 2 * SSD_HEAD_DIM, SSD_STATE)
    y, h_new = _ssd(zx, dtr, q["conv_w"], q["conv_b"], q["dt_bias"], q["a_log"], q["d_skip"], q["ssd_g"],
                    conv_pad, h0, bsz, length, cfg["ssd_chunk"])
    conv_new = zx.reshape(bsz, length, SSD_MAIN)[:, length - (SSD_CONV - 1):, :SSD_CONV_DIM]
    x2 = _out_proj([y], q["ssd_out"], x2, q["norm_mix_post"][1], cfg["tm_mm"])
    x2 = _mlp(x2, q["norm_mlp_pre"][1], q["mlp_up"][1], q["mlp_down"][1], q["norm_mlp_post"][1],
              cfg["tm_mm"], 1024)

    return (x2.reshape(bsz, length, d), sb_k[None], sb_v[None], gla_new[None],
            h_new.reshape(bsz, SSD_HEADS, SSD_HEAD_DIM, SSD_STATE)[None], conv_new[None])


def _prompt_cfg(length):
    return dict(tn_even=EVEN_MAIN // 2, tn_ssd=SSD_MAIN // 3, tm_mm=min(1024, 2 * length),
                gla_rows=min(256, length), gla_sub=16,
                sb_tq=min(512, length), sb_tk=128, sb_split=2, sb_unroll=2,
                ssd_chunk=min(128, length))


def _sample_cfg(bsz, length, n_pages):
    return dict(tn_even=EVEN_MAIN // 2, tn_ssd=SSD_MAIN // 3, tm_mm=bsz * length, gla_rows=length, gla_sub=length, sb_pages=min(16, n_pages),
                ssd_chunk=length)


def kernel(x_prompt, x_sample, cache_sb_k, cache_sb_v, state_gla, state_ssm, state_conv, page_table, w_in_even, gla_w_gate2, gla_b_gate, gla_norm_g, sb_logit_bias, w_out_even, ssd_w_in, ssd_conv_w, ssd_conv_b, ssd_dt_bias, ssd_a_log, ssd_d, ssd_norm_g, ssd_w_out, norm_mix_pre, norm_mix_post, norm_mlp_pre, norm_mlp_post, mlp_w_up, mlp_w_down):
    q = _prep_params(dict(
        w_in_even=w_in_even, gla_w_gate2=gla_w_gate2, gla_b_gate=gla_b_gate, gla_norm_g=gla_norm_g,
        sb_logit_bias=sb_logit_bias, w_out_even=w_out_even, ssd_w_in=ssd_w_in, ssd_conv_w=ssd_conv_w,
        ssd_conv_b=ssd_conv_b, ssd_dt_bias=ssd_dt_bias, ssd_a_log=ssd_a_log, ssd_d=ssd_d,
        ssd_norm_g=ssd_norm_g, ssd_w_out=ssd_w_out, norm_mix_pre=norm_mix_pre, norm_mix_post=norm_mix_post,
        norm_mlp_pre=norm_mlp_pre, norm_mlp_post=norm_mlp_post, mlp_w_up=mlp_w_up, mlp_w_down=mlp_w_down))
    bp, lp, _ = x_prompt.shape
    bs, ls, _ = x_sample.shape
    zeros = lambda *s: jnp.zeros(s, F32)
    prompt = _trunk(x_prompt, None, zeros(bp, GLA_HEADS, GLA_DK, GLA_DV),
                    zeros(bp, SSD_HEADS, SSD_HEAD_DIM, SSD_STATE), zeros(bp, SSD_CONV - 1, SSD_CONV_DIM),
                    q, _prompt_cfg(lp))
    n_pool = cache_sb_k.shape[1]
    pool_k = cache_sb_k.reshape(n_pool, PAGE_SIZE * SB_HEADS, SB_HEAD_DIM)
    pool_v = cache_sb_v.reshape(n_pool, PAGE_SIZE * SB_HEADS, SB_HEAD_DIM)
    sample = _trunk(x_sample, (pool_k, pool_v, page_table), state_gla[0], state_ssm[0], state_conv[0],
                    q, _sample_cfg(bs, ls, page_table.shape[1]))
    return (prompt[0], sample[0]) + prompt[1:] + sample[1:]
```

```python
import functools

import jax
import jax.numpy as jnp
from jax import lax
from jax.experimental import pallas as pl
from jax.experimental.pallas import tpu as pltpu

F32 = jnp.float32
BF16 = jnp.bfloat16

D_MODEL = 1024
RMS_EPS = 1e-6
D_FF = 4 * D_MODEL
GLA_HEADS = 4
GLA_DK = 64
GLA_DV = 128
GLA_GATE_RANK = 16
GLA_GATE_TEMP = 16.0
GLA_QK = GLA_HEADS * GLA_DK
GLA_V = GLA_HEADS * GLA_DV
SB_HEADS = 4
SB_HEAD_DIM = 128
SB_W = SB_HEADS * SB_HEAD_DIM
PAGE_SIZE = 128
EVEN_MAIN = 2 * GLA_QK + 2 * GLA_V + 3 * SB_W
SSD_INNER = 2048
SSD_HEAD_DIM = 64
SSD_HEADS = 32
SSD_GROUPS = 8
SSD_STATE = 128
SSD_CONV = 4
SSD_CONV_DIM = SSD_INNER + 2 * SSD_GROUPS * SSD_STATE
SSD_MAIN = SSD_CONV_DIM + SSD_INNER
LANES = 128
SIDE_W = LANES
VMEM_LIMIT = 48 * 1024 * 1024

NEG_INF = float("-inf")


def _act_dtype(rows):
    return BF16 if rows % 16 == 0 else F32


def _cparams(sem):
    return pltpu.CompilerParams(dimension_semantics=sem, vmem_limit_bytes=VMEM_LIMIT)


def _dot(a, b):
    return jnp.dot(a.astype(BF16), b.astype(BF16), preferred_element_type=F32)


def _dot_nt(a, b):
    return lax.dot_general(a.astype(BF16), b.astype(BF16), (((1,), (1,)), ((), ())),
                           preferred_element_type=F32)


def _dot_tn(a, b):
    return lax.dot_general(a.astype(BF16), b.astype(BF16), (((0,), (0,)), ((), ())),
                           preferred_element_type=F32)


def _split3(x):
    hi = x.astype(BF16)
    r1 = x - hi.astype(F32)
    mid = r1.astype(BF16)
    lo = (r1 - mid.astype(F32)).astype(BF16)
    return hi, mid, lo


def _sel_dot(sel, x):
    hi, mid, lo = _split3(x)
    d = functools.partial(jnp.dot, preferred_element_type=F32)
    return d(sel, hi) + d(sel, mid) + d(sel, lo)


def _dot_sel(x, sel):
    hi, mid, lo = _split3(x)
    d = functools.partial(jnp.dot, preferred_element_type=F32)
    return d(hi, sel) + d(mid, sel) + d(lo, sel)


def _transpose_f32(x):
    n = x.shape[1]
    eye = (lax.broadcasted_iota(jnp.int32, (n, n), 0)
           == lax.broadcasted_iota(jnp.int32, (n, n), 1)).astype(BF16)
    hi, mid, lo = _split3(x)
    d = lambda p: lax.dot_general(eye, p, (((1,), (1,)), ((), ())), preferred_element_type=F32)
    return d(hi) + d(mid) + d(lo)


def _softplus_neg_abs(x):
    e = jnp.exp(-jnp.abs(x))
    u = 1.0 + e
    return jnp.where(u == 1.0, e, jnp.log(u) * (e / (u - 1.0)))


def _log_sigmoid(x):
    return jnp.minimum(x, 0.0) - _softplus_neg_abs(x)


def _softplus(x):
    return jnp.maximum(x, 0.0) + _softplus_neg_abs(x)


def _silu(x):
    return x / (1.0 + jnp.exp(-x))


def _rms_scale(x):
    return x * lax.rsqrt(jnp.mean(x * x, axis=-1, keepdims=True) + RMS_EPS)


def _norm_matmul_kernel(x_ref, g_ref, w_ref, ws_ref, o_ref, os_ref, *rest, head_major):
    hm_refs, xn_ref = rest[:-1], rest[-1]
    j = pl.program_id(1)
    tm, tn = o_ref.shape

    @pl.when(j == 0)
    def _():
        xn_ref[...] = (_rms_scale(x_ref[...]) * g_ref[...]).astype(BF16)
        os_ref[...] = jnp.dot(xn_ref[...], ws_ref[...], preferred_element_type=F32)

    cols = pl.ds(pl.multiple_of(j * tn, tn), tn)
    res = jnp.dot(xn_ref[...], w_ref[:, cols], preferred_element_type=F32)
    o_ref[...] = res
    for ref, start in zip(hm_refs, head_major):
        tile, off = divmod(start, tn)

        @pl.when(j == tile)
        def _():
            for h in range(SB_HEADS):
                ref[pl.ds(h, tm, stride=SB_HEADS), :] = res[:, off + h * SB_HEAD_DIM:off + (h + 1) * SB_HEAD_DIM]


def _norm_matmul(x, g, w, w_side, tm, tn, head_major=()):
    m, d = x.shape
    n = w.shape[1]
    once = pl.Buffered(1)
    return pl.pallas_call(
        functools.partial(_norm_matmul_kernel, head_major=tuple(head_major)),
        grid=(m // tm, n // tn),
        in_specs=[pl.BlockSpec((tm, d), lambda i, j: (i, 0)),
                  pl.BlockSpec((1, d), lambda i, j: (0, 0)),
                  pl.BlockSpec((d, n), lambda i, j: (0, 0), pipeline_mode=once),
                  pl.BlockSpec((d, SIDE_W), lambda i, j: (0, 0), pipeline_mode=once)],
        out_specs=[pl.BlockSpec((tm, tn), lambda i, j: (i, j)),
                   pl.BlockSpec((tm, SIDE_W), lambda i, j: (i, 0))]
        + [pl.BlockSpec((tm * SB_HEADS, SB_HEAD_DIM), lambda i, j: (i, 0)) for _ in head_major],
        out_shape=[jax.ShapeDtypeStruct((m, n), F32),
                   jax.ShapeDtypeStruct((m, SIDE_W), F32)]
        + [jax.ShapeDtypeStruct((m * SB_HEADS, SB_HEAD_DIM), F32) for _ in head_major],
        scratch_shapes=[pltpu.VMEM((tm, d), BF16)],
        compiler_params=_cparams(("parallel", "arbitrary")),
        name="norm_matmul",
    )(x, g, w, w_side)


def _out_proj_kernel(*refs, widths):
    n = len(widths)
    part_refs, w_ref, x_ref, g_ref, o_ref = refs[:n], refs[n], refs[n + 1], refs[n + 2], refs[n + 3]
    acc = None
    off = 0
    for p_ref, wd in zip(part_refs, widths):
        t = jnp.dot(p_ref[...].astype(BF16), w_ref[off:off + wd, :], preferred_element_type=F32)
        acc = t if acc is None else acc + t
        off += wd
    o_ref[...] = x_ref[...] + _rms_scale(acc) * g_ref[...]


def _out_proj(parts, w, x, g, tm):
    m, d = x.shape
    widths = tuple(p.shape[1] for p in parts)
    in_specs = [pl.BlockSpec((tm, wd), lambda i: (i, 0)) for wd in widths]
    in_specs += [pl.BlockSpec(w.shape, lambda i: (0, 0)),
                 pl.BlockSpec((tm, d), lambda i: (i, 0)),
                 pl.BlockSpec((1, d), lambda i: (0, 0))]
    return pl.pallas_call(
        functools.partial(_out_proj_kernel, widths=widths),
        grid=(m // tm,),
        in_specs=in_specs,
        out_specs=pl.BlockSpec((tm, d), lambda i: (i, 0)),
        out_shape=jax.ShapeDtypeStruct((m, d), F32),
        compiler_params=_cparams(("parallel",)),
        name="out_proj",
    )(*parts, w, x, g)


def _mlp_kernel(x_ref, g1_ref, wu_ref, wd_ref, g2_ref, o_ref, xn_ref, acc_ref):
    f = pl.program_id(1)

    @pl.when(f == 0)
    def _():
        xn_ref[...] = (_rms_scale(x_ref[...]) * g1_ref[...]).astype(BF16)
        acc_ref[...] = jnp.zeros_like(acc_ref)

    h = jnp.dot(xn_ref[...], wu_ref[...], preferred_element_type=F32)
    h = jnp.square(jnp.maximum(h, 0.0))
    acc_ref[...] += jnp.dot(h.astype(BF16), wd_ref[...], preferred_element_type=F32)

    @pl.when(f == pl.num_programs(1) - 1)
    def _():
        o_ref[...] = x_ref[...] + _rms_scale(acc_ref[...]) * g2_ref[...]


def _mlp(x, g1, wu, wd, g2, tm, tf):
    m, d = x.shape
    ff = wu.shape[1]
    return pl.pallas_call(
        _mlp_kernel,
        grid=(m // tm, ff // tf),
        in_specs=[pl.BlockSpec((tm, d), lambda i, f: (i, 0)),
                  pl.BlockSpec((1, d), lambda i, f: (0, 0)),
                  pl.BlockSpec((d, tf), lambda i, f: (0, f)),
                  pl.BlockSpec((tf, d), lambda i, f: (f, 0)),
                  pl.BlockSpec((1, d), lambda i, f: (0, 0))],
        out_specs=pl.BlockSpec((tm, d), lambda i, f: (i, 0)),
        out_shape=jax.ShapeDtypeStruct((m, d), F32),
        scratch_shapes=[pltpu.VMEM((tm, d), BF16), pltpu.VMEM((tm, d), F32)],
        compiler_params=_cparams(("parallel", "arbitrary")),
        name="mlp",
    )(x, g1, wu, wd, g2)


def _gla_kernel(q_ref, k_ref, v_ref, r_ref, glr_ref, w2_ref, bg_ref, g_ref, s0_ref,
                o_ref, s_ref, st_ref, b_ref, hi_ref, lo_ref, *, sub, unroll):
    tb = pl.program_id(1)
    t_rows = q_ref.shape[0]
    nsub = t_rows // sub
    npair = GLA_HEADS // 2

    @pl.when(tb == 0)
    def _():
        st_ref[...] = s0_ref[0]

    la = _log_sigmoid(_dot(glr_ref[...], w2_ref[...]) + bg_ref[...]) * (1.0 / GLA_GATE_TEMP)
    row = lax.broadcasted_iota(jnp.int32, (t_rows, t_rows), 0)
    col = lax.broadcasted_iota(jnp.int32, (t_rows, t_rows), 1)
    tri = jnp.where(((row // sub) == (col // sub)) & (col <= row), 1.0, 0.0).astype(BF16)
    b_ref[...] = _sel_dot(tri, la)
    la_hi = la.astype(BF16)
    hi_ref[...] = la_hi
    lo_ref[...] = (la - la_hi.astype(F32)).astype(BF16)
    tn = lambda a, b: lax.dot_general(a, b, (((0,), (0,)), ((), ())), preferred_element_type=F32)
    ones_sub = jnp.ones((sub, LANES), BF16)

    er = lax.broadcasted_iota(jnp.int32, (GLA_QK, GLA_V), 0) // GLA_DK
    ec = lax.broadcasted_iota(jnp.int32, (GLA_QK, GLA_V), 1) // GLA_DV
    expand = jnp.where(er == ec, 1.0, 0.0).astype(BF16)
    t_iota = lax.broadcasted_iota(jnp.int32, (sub, GLA_QK), 0)
    zeros_blk = jnp.zeros((GLA_DK, GLA_DV), F32)

    def sub_chunk(i):
        rows = pl.ds(pl.multiple_of(i * sub, sub), sub)
        b_i = b_ref[rows, :]
        q_i = q_ref[rows, :] * (GLA_DK ** -0.5)
        k_i = k_ref[rows, :]
        v_i = v_ref[rows, :]
        dfull = jnp.exp(tn(hi_ref[rows, :], ones_sub) + tn(lo_ref[rows, :], ones_sub))
        qe = (q_i * jnp.exp(b_i)).astype(BF16)
        kd = (k_i * jnp.exp(b_i[sub - 1:sub, :] - b_i)).astype(BF16)
        v_bf = v_i.astype(BF16)
        inter = []
        for p in range(npair):
            ha, hb = 2 * p, 2 * p + 1
            s_a, s_b = st_ref[ha], st_ref[hb]
            lanes = slice(p * LANES, (p + 1) * LANES)
            w_pair = jnp.concatenate([jnp.concatenate([s_a, zeros_blk], axis=1),
                                      jnp.concatenate([zeros_blk, s_b], axis=1)], axis=0)
            inter.append(jnp.dot(qe[:, lanes], w_pair.astype(BF16), preferred_element_type=F32))
            u = tn(kd[:, lanes], v_bf[:, ha * GLA_DV:(hb + 1) * GLA_DV])
            st_ref[ha] = s_a * dfull[ha * GLA_DK:(ha + 1) * GLA_DK, :] + u[:GLA_DK, :GLA_DV]
            st_ref[hb] = s_b * dfull[hb * GLA_DK:(hb + 1) * GLA_DK, :] + u[GLA_DK:, GLA_DV:]
        prods = []
        for j in range(sub):
            e = jnp.exp(jnp.where(t_iota >= j, b_i - b_i[j:j + 1, :], NEG_INF))
            prods.append(q_i * e * k_i[j:j + 1, :])
        p_all = jnp.concatenate(prods, axis=0)
        r_all = _dot(p_all, expand)
        o = jnp.concatenate(inter, axis=1)
        for j in range(sub):
            o = o + r_all[j * sub:(j + 1) * sub, :] * v_i[j:j + 1, :]
        r_i = r_ref[rows, :]
        outs = []
        for h in range(GLA_HEADS):
            hv = slice(h * GLA_DV, (h + 1) * GLA_DV)
            outs.append(_rms_scale(o[:, hv]) * g_ref[...] * _silu(r_i[:, hv]))
        o_ref[rows, :] = jnp.concatenate(outs, axis=1).astype(o_ref.dtype)

    def body(it, carry):
        for u in range(unroll):
            sub_chunk(it * unroll + u)
        return carry

    lax.fori_loop(0, nsub // unroll, body, 0)

    @pl.when(tb == pl.num_programs(1) - 1)
    def _():
        s_ref[0] = st_ref[...]


def _gla(proj, side, w2, bg, g, s0, bsz, length, t_rows, sub):
    nt = length // t_rows
    m = bsz * length
    row = lambda b, t: b * nt + t
    return pl.pallas_call(
        functools.partial(_gla_kernel, sub=sub, unroll=16 if (t_rows // sub) % 16 == 0 else 1),
        grid=(bsz, nt),
        in_specs=[pl.BlockSpec((t_rows, GLA_QK), lambda b, t: (row(b, t), 0)),
                  pl.BlockSpec((t_rows, GLA_QK), lambda b, t: (row(b, t), 1)),
                  pl.BlockSpec((t_rows, GLA_V), lambda b, t: (row(b, t), 1)),
                  pl.BlockSpec((t_rows, GLA_V), lambda b, t: (row(b, t), 2)),
                  pl.BlockSpec((t_rows, SIDE_W), lambda b, t: (row(b, t), 0)),
                  pl.BlockSpec((SIDE_W, GLA_QK), lambda b, t: (0, 0)),
                  pl.BlockSpec((1, GLA_QK), lambda b, t: (0, 0)),
                  pl.BlockSpec((1, GLA_DV), lambda b, t: (0, 0)),
                  pl.BlockSpec((1, GLA_HEADS, GLA_DK, GLA_DV), lambda b, t: (b, 0, 0, 0))],
        out_specs=[pl.BlockSpec((t_rows, GLA_V), lambda b, t: (row(b, t), 0)),
                   pl.BlockSpec((1, GLA_HEADS, GLA_DK, GLA_DV), lambda b, t: (b, 0, 0, 0))],
        out_shape=[jax.ShapeDtypeStruct((m, GLA_V), _act_dtype(sub)),
                   jax.ShapeDtypeStruct((bsz, GLA_HEADS, GLA_DK, GLA_DV), F32)],
        scratch_shapes=[pltpu.VMEM((GLA_HEADS, GLA_DK, GLA_DV), F32),
                        pltpu.VMEM((t_rows, GLA_QK), F32),
                        pltpu.VMEM((t_rows, GLA_QK), BF16),
                        pltpu.VMEM((t_rows, GLA_QK), BF16)],
        compiler_params=_cparams(("parallel", "arbitrary")),
        name="gla",
    )(proj, proj, proj, proj, side, w2, bg, g, s0)


LOG2_E = 1.4426950408889634
SB_LOGIT_SCALE = SB_HEAD_DIM ** -0.5 * LOG2_E
SB_T_MAX = 64.0


def _suffix_matrix(tk):
    jr = lax.broadcasted_iota(jnp.int32, (tk, tk + LANES), 0)
    jc = lax.broadcasted_iota(jnp.int32, (tk, tk + LANES), 1)
    return jnp.where((jr > jc) | (jc >= tk), 1.0, 0.0).astype(BF16)


def _sb_logits(t, mask):
    n = jnp.where(t > SB_T_MAX, t, jnp.log2(1.0 + jnp.exp2(t)))
    log_b = t - n
    if mask is not None:
        n = jnp.where(mask, n, 0.0)
    return log_b, n.astype(BF16)


def _sb_weights(log_b, n, suffix, carry, mask):
    tk = log_b.shape[1]
    sums = jnp.dot(n, suffix, preferred_element_type=F32)
    a = jnp.exp2(log_b - sums[:, :tk] - jnp.concatenate([carry] * (tk // LANES), axis=1))
    if mask is not None:
        a = jnp.where(mask, a, 0.0)
    return a.astype(BF16), carry + sums[:, tk:]


def _sb_tile(q_bf, k_blk, v_blk, bias2, suffix, carry, acc, mask):
    log_b, n = _sb_logits(_dot_nt(q_bf, k_blk) * SB_LOGIT_SCALE + bias2, mask)
    a, carry = _sb_weights(log_b, n, suffix, carry, mask)
    return carry, acc + jnp.dot(a, v_blk.astype(BF16), preferred_element_type=F32)


def _sb_prompt_kernel(bias_ref, q_ref, k_ref, v_ref, o_ref, lb_ref, n_ref, a_ref, carry_ref, acc_ref,
                      *, tk, nsplit, unroll):
    h = pl.program_id(1)
    qi = pl.program_id(2)
    tq = q_ref.shape[0]
    th = tq // nsplit
    bias2 = bias_ref[h] * LOG2_E
    suffix = _suffix_matrix(tk)
    q0 = qi * tq
    q_bf = [q_ref[s * th:(s + 1) * th, :].astype(BF16) for s in range(nsplit)]
    zero = jnp.zeros((th, LANES), F32)
    carry = [zero] * nsplit
    acc = [zero] * nsplit
    row = lax.broadcasted_iota(jnp.int32, (th, tk), 0)
    col = lax.broadcasted_iota(jnp.int32, (th, tk), 1)

    for d in reversed(range(tq // tk)):
        k0 = pl.multiple_of(q0 + d * tk, tk)
        k_blk = k_ref[pl.ds(k0, tk), :]
        v_blk = v_ref[pl.ds(k0, tk), :]
        for s in range(nsplit):
            if d * tk >= (s + 1) * th:
                continue
            mask = None if (d + 1) * tk <= s * th else (col + d * tk) < (row + s * th)
            carry[s], acc[s] = _sb_tile(q_bf[s], k_blk, v_blk, bias2, suffix, carry[s], acc[s], mask)
    for s in range(nsplit):
        carry_ref[s * th:(s + 1) * th, :] = carry[s]
        acc_ref[s * th:(s + 1) * th, :] = acc[s]

    q_all = q_ref[...].astype(BF16)
    nfull = q0 // tk

    def key_rows(f):
        return pl.ds(pl.multiple_of(jnp.maximum(nfull - 1 - f, 0) * tk, tk), tk)

    def stage_logits(f, slot):
        log_b, n = _sb_logits(_dot_nt(q_all, k_ref[key_rows(f), :]) * SB_LOGIT_SCALE + bias2, None)
        lb_ref[slot] = log_b
        n_ref[slot] = n

    def stage_weights(slot):
        a, carry = _sb_weights(lb_ref[slot], n_ref[slot], suffix, carry_ref[...], None)
        a_ref[slot] = a
        carry_ref[...] = carry

    def stage_values(f, slot):
        acc_ref[...] += jnp.dot(a_ref[slot], v_ref[key_rows(f), :].astype(BF16), preferred_element_type=F32)

    for u in range(2 * unroll):
        stage_logits(u, u)
    for u in range(unroll):
        stage_weights(u)

    def body(it, c):
        for parity in range(2):
            cur = parity * unroll
            nxt = unroll - cur
            f0 = (2 * it + parity) * unroll
            for u in range(unroll):
                stage_logits(f0 + 2 * unroll + u, cur + u)
            for u in range(unroll):
                stage_weights(nxt + u)
            for u in range(unroll):
                stage_values(f0 + u, cur + u)
        return c

    lax.fori_loop(0, nfull // (2 * unroll), body, 0)
    o_ref[...] = acc_ref[...].astype(o_ref.dtype)


def _sb_prompt(proj, bias, bsz, length, tq, tk, nsplit, unroll):
    m = bsz * length
    nq = length // tq
    qcol = (2 * GLA_QK + 2 * GLA_V) // SB_HEAD_DIM
    kcol = qcol + SB_HEADS
    vcol = kcol + SB_HEADS
    return pl.pallas_call(
        functools.partial(_sb_prompt_kernel, tk=tk, nsplit=nsplit, unroll=unroll),
        grid=(bsz, SB_HEADS, nq),
        in_specs=[pl.BlockSpec(memory_space=pltpu.SMEM),
                  pl.BlockSpec((tq, SB_HEAD_DIM), lambda b, h, i: (b * nq + i, qcol + h)),
                  pl.BlockSpec((length, SB_HEAD_DIM), lambda b, h, i: (b, kcol + h)),
                  pl.BlockSpec((length, SB_HEAD_DIM), lambda b, h, i: (b, vcol + h))],
        out_specs=pl.BlockSpec((tq, SB_HEAD_DIM), lambda b, h, i: (b * nq + i, h)),
        out_shape=jax.ShapeDtypeStruct((m, SB_W), _act_dtype(tq)),
        scratch_shapes=[pltpu.VMEM((2 * unroll, tq, tk), F32),
                        pltpu.VMEM((2 * unroll, tq, tk), BF16),
                        pltpu.VMEM((2 * unroll, tq, tk), BF16),
                        pltpu.VMEM((tq, LANES), F32),
                        pltpu.VMEM((tq, SB_HEAD_DIM), F32)],
        compiler_params=_cparams(("parallel", "parallel", "arbitrary")),
        name="sb_prompt",
    )(bias, proj, proj, proj)


def _sb_paged_kernel(pt_ref, bias_ref, q_ref, kn_ref, vn_ref, *refs, pages_per_step):
    g = pages_per_step
    k_refs, v_refs = refs[:g], refs[g:2 * g]
    o_ref, carry_ref, acc_ref, kpad_ref, vpad_ref = refs[2 * g:]
    j = pl.program_id(1)
    lq = q_ref.shape[0]
    nrow = SB_HEADS * lq
    suffix = _suffix_matrix(PAGE_SIZE)
    rhead = lax.broadcasted_iota(jnp.int32, (nrow, PAGE_SIZE), 0) // lq
    bias2 = jnp.zeros((nrow, PAGE_SIZE), F32)
    for h in range(SB_HEADS):
        bias2 = jnp.where(rhead == h, bias_ref[h] * LOG2_E, bias2)
    q_bf = [q_ref[:, h * SB_HEAD_DIM:(h + 1) * SB_HEAD_DIM].astype(BF16) for h in range(SB_HEADS)]

    def tile(k_heads, v_heads, carry, acc, mask):
        z = jnp.concatenate([_dot_nt(q_bf[h], k_heads[h]) for h in range(SB_HEADS)], axis=0)
        log_b, n = _sb_logits(z * SB_LOGIT_SCALE + bias2, mask)
        a, carry = _sb_weights(log_b, n, suffix, carry, mask)
        acc = [acc[h] + _dot(a[h * lq:(h + 1) * lq, :], v_heads[h]) for h in range(SB_HEADS)]
        return carry, acc

    @pl.when(j == 0)
    def _():
        kpad_ref[...] = jnp.zeros_like(kpad_ref)
        vpad_ref[...] = jnp.zeros_like(vpad_ref)
        for h in range(SB_HEADS):
            hs = slice(h * SB_HEAD_DIM, (h + 1) * SB_HEAD_DIM)
            kpad_ref[h, 0:lq, :] = kn_ref[:, hs]
            vpad_ref[h, 0:lq, :] = vn_ref[:, hs]
        qidx = lax.broadcasted_iota(jnp.int32, (nrow, PAGE_SIZE), 0) % lq
        kidx = lax.broadcasted_iota(jnp.int32, (nrow, PAGE_SIZE), 1)
        carry, acc = tile([kpad_ref[h] for h in range(SB_HEADS)], [vpad_ref[h] for h in range(SB_HEADS)],
                          jnp.zeros((nrow, PAGE_SIZE), F32),
                          [jnp.zeros((lq, SB_HEAD_DIM), F32)] * SB_HEADS, kidx < qidx)
        carry_ref[...] = carry
        acc_ref[...] = jnp.concatenate(acc, axis=0)

    nchunk = (PAGE_SIZE * SB_HEADS) // LANES
    ntile = g * nchunk
    q_stack = jnp.concatenate(q_bf, axis=0)
    own = (lax.broadcasted_iota(jnp.int32, (nrow, LANES), 1) % SB_HEADS) == rhead
    tiles = []
    for p in range(g):
        z = _dot_nt(q_stack, k_refs[p][0])
        tiles += [z[:, c * LANES:(c + 1) * LANES] for c in reversed(range(nchunk))]
    z = jnp.concatenate(tiles, axis=0).reshape(ntile, nrow, LANES)
    log_b, n = _sb_logits(z * SB_LOGIT_SCALE + bias2[None], own[None])
    sums = jnp.dot(n.reshape(ntile * nrow, LANES), suffix, preferred_element_type=F32)
    sums = sums.reshape(ntile, nrow, 2 * LANES)
    carry = carry_ref[...]
    carries = []
    for t in range(ntile):
        carries.append(carry)
        carry = carry + sums[t, :, LANES:]
    carry_ref[...] = carry
    a = jnp.exp2(log_b - sums[:, :, :LANES] - jnp.stack(carries, axis=0))
    a = jnp.where(own[None], a, 0.0).astype(BF16)
    acc = acc_ref[...]
    for p in range(g):
        a_page = jnp.concatenate([a[p * nchunk + (nchunk - 1 - c)] for c in range(nchunk)], axis=1)
        acc = acc + jnp.dot(a_page, v_refs[p][0].astype(BF16), preferred_element_type=F32)
    acc_ref[...] = acc

    @pl.when(j == pl.num_programs(1) - 1)
    def _():
        for h in range(SB_HEADS):
            o_ref[:, h * SB_HEAD_DIM:(h + 1) * SB_HEAD_DIM] = acc[h * lq:(h + 1) * lq, :].astype(o_ref.dtype)


def _sb_paged(proj, bias, pool_k, pool_v, page_table, bsz, lq, pages_per_step):
    n_pages = page_table.shape[1]
    g = pages_per_step
    nsteps = n_pages // g
    qcol = (2 * GLA_QK + 2 * GLA_V) // SB_W

    def page_spec(p):
        return pl.BlockSpec((1, PAGE_SIZE * SB_HEADS, SB_HEAD_DIM),
                            lambda b, j, pt: (pt[b, n_pages - 1 - (j * g + p)], 0, 0))

    grid_spec = pltpu.PrefetchScalarGridSpec(
        num_scalar_prefetch=1,
        grid=(bsz, nsteps),
        in_specs=[pl.BlockSpec(memory_space=pltpu.SMEM),
                  pl.BlockSpec((lq, SB_W), lambda b, j, pt: (b, qcol)),
                  pl.BlockSpec((lq, SB_W), lambda b, j, pt: (b, qcol + 1)),
                  pl.BlockSpec((lq, SB_W), lambda b, j, pt: (b, qcol + 2))]
        + [page_spec(p) for p in range(g)] + [page_spec(p) for p in range(g)],
        out_specs=pl.BlockSpec((lq, SB_W), lambda b, j, pt: (b, 0)),
        scratch_shapes=[pltpu.VMEM((SB_HEADS * lq, PAGE_SIZE), F32),
                        pltpu.VMEM((SB_HEADS * lq, SB_HEAD_DIM), F32),
                        pltpu.VMEM((SB_HEADS, PAGE_SIZE, SB_HEAD_DIM), F32),
                        pltpu.VMEM((SB_HEADS, PAGE_SIZE, SB_HEAD_DIM), F32)])
    return pl.pallas_call(
        functools.partial(_sb_paged_kernel, pages_per_step=g),
        grid_spec=grid_spec,
        out_shape=jax.ShapeDtypeStruct((bsz * lq, SB_W), _act_dtype(lq)),
        compiler_params=_cparams(("parallel", "arbitrary")),
        name="sb_paged",
    )(page_table, bias, proj, proj, proj, *([pool_k] * g), *([pool_v] * g))


def _ssd_chunk(r, xc_ref, z_ref, dtr_ref, dtb_ref, dsk_ref, hs_ref, ypre_ref, a, causal, causal_bf,
               lo_lanes, lo_rows):
    c = r.stop - r.start
    hpg = SSD_HEADS // SSD_GROUPS
    xc = xc_ref[r, :]
    dt = _softplus(dtr_ref[r, :] + dtb_ref[...])
    cum = _sel_dot(causal_bf, dt * a)
    cum_t = _transpose_f32(cum)
    dt_t = _transpose_f32(dt)
    last = cum[c - 1:c, :]
    e_cum = jnp.exp(cum)
    dec = jnp.exp(last - cum) * dt
    e_last_t = jnp.exp(cum_t[:, c - 1:c])

    for g in range(SSD_GROUPS):
        b_g = xc[:, SSD_INNER + g * SSD_STATE:SSD_INNER + (g + 1) * SSD_STATE]
        c_g = xc[:, SSD_INNER + (SSD_GROUPS + g) * SSD_STATE:SSD_INNER + (SSD_GROUPS + g + 1) * SSD_STATE]
        cb = _dot_nt(c_g, b_g)
        for pp in range(hpg // 2):
            pair = g * (hpg // 2) + pp
            h0, h1 = 2 * pair, 2 * pair + 1
            x_pair = xc[:, pair * LANES:(pair + 1) * LANES]
            hs = hs_ref[pair]
            y_pair = _dot_nt(c_g, hs) * jnp.where(lo_lanes, e_cum[:, h0:h0 + 1], e_cum[:, h1:h1 + 1])
            x_bf = x_pair.astype(BF16)
            intra = []
            for hh in (h0, h1):
                lmat = jnp.exp(jnp.where(causal, cum[:, hh:hh + 1] - cum_t[hh:hh + 1, :], NEG_INF))
                w = lmat * dt_t[hh:hh + 1, :] * cb
                intra.append(jnp.dot(w.astype(BF16), x_bf, preferred_element_type=F32))
            y_pair = y_pair + jnp.where(lo_lanes, intra[0], intra[1])
            xd = x_pair * jnp.where(lo_lanes, dec[:, h0:h0 + 1], dec[:, h1:h1 + 1])
            scale = jnp.where(lo_rows, e_last_t[h0:h0 + 1, :], e_last_t[h1:h1 + 1, :])
            upd = _dot(xd.T, b_g) if c % LANES == 0 else _dot_tn(xd, b_g)
            hs_ref[pair] = scale * hs + upd
            y_pair = y_pair + x_pair * dsk_ref[:, pair * LANES:(pair + 1) * LANES]
            zz = z_ref[r, pair * LANES:(pair + 1) * LANES]
            ypre_ref[r, pair * LANES:(pair + 1) * LANES] = y_pair * _silu(zz)


def _ssd_kernel(xbc_ref, z_ref, dtr_ref, cw_ref, cb_ref, dtb_ref, alog_ref, dsk_ref, ng_ref,
                conv0_ref, h0_ref, y_ref, hout_ref, xext_ref, hs_ref, ypre_ref, xc_ref, *, chunk):
    ci = pl.program_id(1)
    rows = xbc_ref.shape[0]
    c = chunk
    tail = SSD_CONV - 1
    pad = 8

    @pl.when(ci == 0)
    def _():
        xext_ref[0:pad, :] = conv0_ref[0]
        hs_ref[...] = h0_ref[0]

    xext_ref[pad:pad + rows, :] = xbc_ref[...]
    xfull = xext_ref[...]
    acc = cb_ref[...] + xfull[pad:, :] * cw_ref[tail:tail + 1, :]
    for w in range(tail):
        acc = acc + pltpu.roll(xfull, tail - w, axis=0)[pad:, :] * cw_ref[w:w + 1, :]
    xc_ref[...] = _silu(acc)
    xext_ref[0:pad, :] = xext_ref[rows:rows + pad, :]

    a = -jnp.exp(alog_ref[...])
    row = lax.broadcasted_iota(jnp.int32, (c, c), 0)
    col = lax.broadcasted_iota(jnp.int32, (c, c), 1)
    causal = col <= row
    causal_bf = jnp.where(causal, 1.0, 0.0).astype(BF16)
    lo_lanes = lax.broadcasted_iota(jnp.int32, (c, LANES), 1) < SSD_HEAD_DIM
    lo_rows = lax.broadcasted_iota(jnp.int32, (LANES, LANES), 0) < SSD_HEAD_DIM
    for sc in range(rows // c):
        _ssd_chunk(slice(sc * c, (sc + 1) * c), xc_ref, z_ref, dtr_ref, dtb_ref, dsk_ref, hs_ref, ypre_ref,
                   a, causal, causal_bf, lo_lanes, lo_rows)
    gw = SSD_INNER // SSD_GROUPS
    for g in range(SSD_GROUPS):
        gs = slice(g * gw, (g + 1) * gw)
        y_ref[:, gs] = (_rms_scale(ypre_ref[:, gs]) * ng_ref[:, gs]).astype(y_ref.dtype)

    @pl.when(ci == pl.num_programs(1) - 1)
    def _():
        hout_ref[0] = hs_ref[...]


def _ssd(zx, side, cw, cb, dtb, alog, dsk, ng, conv0, h0, bsz, length, chunk, rows):
    nc = length // rows
    m = bsz * length
    row = lambda b, c: b * nc + c
    const = lambda b, c: (0, 0)
    npair = SSD_HEADS // 2
    return pl.pallas_call(
        functools.partial(_ssd_kernel, chunk=chunk),
        grid=(bsz, nc),
        in_specs=[pl.BlockSpec((rows, SSD_CONV_DIM), lambda b, c: (row(b, c), 0)),
                  pl.BlockSpec((rows, SSD_INNER), lambda b, c: (row(b, c), SSD_CONV_DIM // SSD_INNER)),
                  pl.BlockSpec((rows, SIDE_W), lambda b, c: (row(b, c), 0)),
                  pl.BlockSpec((SSD_CONV, SSD_CONV_DIM), const),
                  pl.BlockSpec((1, SSD_CONV_DIM), const),
                  pl.BlockSpec((1, SIDE_W), const),
                  pl.BlockSpec((1, SIDE_W), const),
                  pl.BlockSpec((1, SSD_INNER), const),
                  pl.BlockSpec((1, SSD_INNER), const),
                  pl.BlockSpec((1, 8, SSD_CONV_DIM), lambda b, c: (b, 0, 0)),
                  pl.BlockSpec((1, npair, LANES, SSD_STATE), lambda b, c: (b, 0, 0, 0))],
        out_specs=[pl.BlockSpec((rows, SSD_INNER), lambda b, c: (row(b, c), 0)),
                   pl.BlockSpec((1, npair, LANES, SSD_STATE), lambda b, c: (b, 0, 0, 0))],
        out_shape=[jax.ShapeDtypeStruct((m, SSD_INNER), _act_dtype(chunk)),
                   jax.ShapeDtypeStruct((bsz, npair, LANES, SSD_STATE), F32)],
        scratch_shapes=[pltpu.VMEM((rows + 8, SSD_CONV_DIM), F32),
                        pltpu.VMEM((npair, LANES, SSD_STATE), F32),
                        pltpu.VMEM((rows, SSD_INNER), F32),
                        pltpu.VMEM((rows, SSD_CONV_DIM), F32)],
        compiler_params=_cparams(("parallel", "arbitrary")),
        name="ssd",
    )(zx, zx, side, cw, cb, dtb, alog, dsk, ng, conv0, h0)


def _pad_cols(w, width):
    return jnp.pad(w, ((0, 0), (0, width - w.shape[1])))


def _row(v, width=None):
    v = v.reshape(1, -1).astype(F32)
    return v if width is None else _pad_cols(v, width)


def _prep_params(p):
    w_in = p["w_in_even"][0]
    n_gla = 2 * GLA_QK + GLA_V
    q = {}
    q["even_main"] = jnp.concatenate([w_in[:, :n_gla], w_in[:, n_gla + GLA_GATE_RANK:]], axis=1).astype(BF16)
    q["even_side"] = _pad_cols(w_in[:, n_gla:n_gla + GLA_GATE_RANK], SIDE_W).astype(BF16)
    q["gate_w2"] = jnp.pad(p["gla_w_gate2"][0], ((0, SIDE_W - GLA_GATE_RANK), (0, 0))).astype(BF16)
    q["gate_b"] = _row(p["gla_b_gate"][0])
    q["gla_g"] = _row(p["gla_norm_g"][0])
    q["sb_bias"] = p["sb_logit_bias"][0].astype(F32)
    q["even_out"] = p["w_out_even"][0].astype(BF16)
    w_ssd = p["ssd_w_in"][0]
    q["ssd_main"] = jnp.concatenate([w_ssd[:, SSD_INNER:SSD_INNER + SSD_CONV_DIM], w_ssd[:, :SSD_INNER]],
                                    axis=1).astype(BF16)
    q["ssd_side"] = _pad_cols(w_ssd[:, SSD_INNER + SSD_CONV_DIM:], SIDE_W).astype(BF16)
    q["conv_w"] = p["ssd_conv_w"][0].astype(F32)
    q["conv_b"] = _row(p["ssd_conv_b"][0])
    q["dt_bias"] = _row(p["ssd_dt_bias"][0], SIDE_W)
    q["a_log"] = _row(p["ssd_a_log"][0], SIDE_W)
    q["d_skip"] = _row(jnp.repeat(p["ssd_d"][0], SSD_HEAD_DIM))
    q["ssd_g"] = _row(p["ssd_norm_g"][0])
    q["ssd_out"] = p["ssd_w_out"][0].astype(BF16)
    for name in ("norm_mix_pre", "norm_mix_post", "norm_mlp_pre", "norm_mlp_post"):
        q[name] = [_row(p[name][li]) for li in range(2)]
    q["mlp_up"] = [p["mlp_w_up"][li].astype(BF16) for li in range(2)]
    q["mlp_down"] = [p["mlp_w_down"][li].astype(BF16) for li in range(2)]
    return q


def _trunk(x, paged, gla_s0, ssm_h0, conv0, q, cfg):
    bsz, length, d = x.shape
    m = bsz * length
    x2 = x.reshape(m, d)

    kcol = 2 * GLA_QK + 2 * GLA_V + SB_W
    proj, side, sb_k, sb_v = _norm_matmul(x2, q["norm_mix_pre"][0], q["even_main"], q["even_side"],
                                          cfg["tm_mm"], cfg["tn_even"], head_major=(kcol, kcol + SB_W))
    o_gla, gla_new = _gla(proj, side, q["gate_w2"], q["gate_b"], q["gla_g"], gla_s0,
                          bsz, length, cfg["gla_rows"], cfg["gla_sub"])
    if paged is None:
        o_sb = _sb_prompt(proj, q["sb_bias"], bsz, length, cfg["sb_tq"], cfg["sb_tk"], cfg["sb_split"],
                          cfg["sb_unroll"])
    else:
        o_sb = _sb_paged(proj, q["sb_bias"], paged[0], paged[1], paged[2], bsz, length, cfg["sb_pages"])
    sb_k = sb_k.reshape(bsz, length, SB_HEADS, SB_HEAD_DIM)
    sb_v = sb_v.reshape(bsz, length, SB_HEADS, SB_HEAD_DIM)
    x2 = _out_proj([o_gla, o_sb], q["even_out"], x2, q["norm_mix_post"][0], cfg["tm_mm"])
    x2 = _mlp(x2, q["norm_mlp_pre"][0], q["mlp_up"][0], q["mlp_down"][0], q["norm_mlp_post"][0],
              cfg["tm_mm"], 1024)

    zx, dtr = _norm_matmul(x2, q["norm_mix_pre"][1], q["ssd_main"], q["ssd_side"], cfg["tm_mm"], cfg["tn_ssd"])
    conv_pad = jnp.pad(conv0, ((0, 0), (8 - (SSD_CONV - 1), 0), (0, 0)))
    h0 = ssm_h0.reshape(bsz, SSD_HEADS // 2, 2 * SSD_HEAD_DIM, SSD_STATE)
    y, h_new = _ssd(zx, dtr, q["conv_w"], q["conv_b"], q["dt_bias"], q["a_log"], q["d_skip"], q["ssd_g"],
                    conv_pad, h0, bsz, length, cfg["ssd_chunk"], cfg["ssd_rows"])
    conv_new = zx.reshape(bsz, length, SSD_MAIN)[:, length - (SSD_CONV - 1):, :SSD_CONV_DIM]
    x2 = _out_proj([y], q["ssd_out"], x2, q["norm_mix_post"][1], cfg["tm_mm"])
    x2 = _mlp(x2, q["norm_mlp_pre"][1], q["mlp_up"][1], q["mlp_down"][1], q["norm_mlp_post"][1],
              cfg["tm_mm"], 1024)

    return (x2.reshape(bsz, length, d), sb_k[None], sb_v[None], gla_new[None],
            h_new.reshape(bsz, SSD_HEADS, SSD_HEAD_DIM, SSD_STATE)[None], conv_new[None])


def _prompt_cfg(length):
    return dict(tn_even=EVEN_MAIN // 2, tn_ssd=SSD_MAIN // 3, tm_mm=min(1024, 2 * length),
                gla_rows=min(256, length), gla_sub=16,
                sb_tq=min(512, length), sb_tk=128, sb_split=2, sb_unroll=2,
                ssd_chunk=min(128, length), ssd_rows=min(256, length))


def _sample_cfg(bsz, length, n_pages):
    return dict(tn_even=EVEN_MAIN // 2, tn_ssd=SSD_MAIN // 3, tm_mm=bsz * length, gla_rows=length, gla_sub=length, sb_pages=min(16, n_pages),
                ssd_chunk=length, ssd_rows=length)


def kernel(x_prompt, x_sample, cache_sb_k, cache_sb_v, state_gla, state_ssm, state_conv, page_table, w_in_even, gla_w_gate2, gla_b_gate, gla_norm_g, sb_logit_bias, w_out_even, ssd_w_in, ssd_conv_w, ssd_conv_b, ssd_dt_bias, ssd_a_log, ssd_d, ssd_norm_g, ssd_w_out, norm_mix_pre, norm_mix_post, norm_mlp_pre, norm_mlp_post, mlp_w_up, mlp_w_down):
    q = _prep_params(dict(
        w_in_even=w_in_even, gla_w_gate2=gla_w_gate2, gla_b_gate=gla_b_gate, gla_norm_g=gla_norm_g,
        sb_logit_bias=sb_logit_bias, w_out_even=w_out_even, ssd_w_in=ssd_w_in, ssd_conv_w=ssd_conv_w,
        ssd_conv_b=ssd_conv_b, ssd_dt_bias=ssd_dt_bias, ssd_a_log=ssd_a_log, ssd_d=ssd_d,
        ssd_norm_g=ssd_norm_g, ssd_w_out=ssd_w_out, norm_mix_pre=norm_mix_pre, norm_mix_post=norm_mix_post,
        norm_mlp_pre=norm_mlp_pre, norm_mlp_post=norm_mlp_post, mlp_w_up=mlp_w_up, mlp_w_down=mlp_w_down))
    bp, lp, _ = x_prompt.shape
    bs, ls, _ = x_sample.shape
    zeros = lambda *s: jnp.zeros(s, F32)
    prompt = _trunk(x_prompt, None, zeros(bp, GLA_HEADS, GLA_DK, GLA_DV),
                    zeros(bp, SSD_HEADS, SSD_HEAD_DIM, SSD_STATE), zeros(bp, SSD_CONV - 1, SSD_CONV_DIM),
                    q, _prompt_cfg(lp))
    n_pool = cache_sb_k.shape[1]
    pool_k = cache_sb_k.reshape(n_pool, PAGE_SIZE * SB_HEADS, SB_HEAD_DIM)
    pool_v = cache_sb_v.reshape(n_pool, PAGE_SIZE * SB_HEADS, SB_HEAD_DIM)
    sample = _trunk(x_sample, (pool_k, pool_v, page_table), state_gla[0], state_ssm[0], state_conv[0],
                    q, _sample_cfg(bs, ls, page_table.shape[1]))
    return (prompt[0], sample[0]) + prompt[1:] + sample[1:]
```

```python
import functools

import jax
import jax.numpy as jnp
from jax import lax
from jax.experimental import pallas as pl
from jax.experimental.pallas import tpu as pltpu

F32 = jnp.float32
BF16 = jnp.bfloat16

D_MODEL = 1024
RMS_EPS = 1e-6
D_FF = 4 * D_MODEL
GLA_HEADS = 4
GLA_DK = 64
GLA_DV = 128
GLA_GATE_RANK = 16
GLA_GATE_TEMP = 16.0
GLA_QK = GLA_HEADS * GLA_DK
GLA_V = GLA_HEADS * GLA_DV
SB_HEADS = 4
SB_HEAD_DIM = 128
SB_W = SB_HEADS * SB_HEAD_DIM
PAGE_SIZE = 128
EVEN_MAIN = 2 * GLA_QK + 2 * GLA_V + 3 * SB_W
SSD_INNER = 2048
SSD_HEAD_DIM = 64
SSD_HEADS = 32
SSD_GROUPS = 8
SSD_STATE = 128
SSD_CONV = 4
SSD_CONV_DIM = SSD_INNER + 2 * SSD_GROUPS * SSD_STATE
SSD_MAIN = SSD_CONV_DIM + SSD_INNER
LANES = 128
SIDE_W = LANES
VMEM_LIMIT = 48 * 1024 * 1024

NEG_INF = float("-inf")


def _act_dtype(rows):
    return BF16 if rows % 16 == 0 else F32


def _cparams(sem):
    return pltpu.CompilerParams(dimension_semantics=sem, vmem_limit_bytes=VMEM_LIMIT)


def _dot(a, b):
    return jnp.dot(a.astype(BF16), b.astype(BF16), preferred_element_type=F32)


def _dot_nt(a, b):
    return lax.dot_general(a.astype(BF16), b.astype(BF16), (((1,), (1,)), ((), ())),
                           preferred_element_type=F32)


def _dot_tn(a, b):
    return lax.dot_general(a.astype(BF16), b.astype(BF16), (((0,), (0,)), ((), ())),
                           preferred_element_type=F32)


def _split3(x):
    hi = x.astype(BF16)
    r1 = x - hi.astype(F32)
    mid = r1.astype(BF16)
    lo = (r1 - mid.astype(F32)).astype(BF16)
    return hi, mid, lo


def _sel_dot(sel, x):
    hi, mid, lo = _split3(x)
    d = functools.partial(jnp.dot, preferred_element_type=F32)
    return d(sel, hi) + d(sel, mid) + d(sel, lo)


def _dot_sel(x, sel):
    hi, mid, lo = _split3(x)
    d = functools.partial(jnp.dot, preferred_element_type=F32)
    return d(hi, sel) + d(mid, sel) + d(lo, sel)


def _transpose_f32(x):
    n = x.shape[1]
    eye = (lax.broadcasted_iota(jnp.int32, (n, n), 0)
           == lax.broadcasted_iota(jnp.int32, (n, n), 1)).astype(BF16)
    hi, mid, lo = _split3(x)
    d = lambda p: lax.dot_general(eye, p, (((1,), (1,)), ((), ())), preferred_element_type=F32)
    return d(hi) + d(mid) + d(lo)


def _softplus_neg_abs(x):
    e = jnp.exp(-jnp.abs(x))
    u = 1.0 + e
    return jnp.where(u == 1.0, e, jnp.log(u) * (e / (u - 1.0)))


def _log_sigmoid(x):
    return jnp.minimum(x, 0.0) - _softplus_neg_abs(x)


def _softplus(x):
    return jnp.maximum(x, 0.0) + _softplus_neg_abs(x)


def _silu(x):
    h = 0.5 * x
    return h + h * jnp.tanh(h)


def _rms_scale(x):
    return x * lax.rsqrt(jnp.mean(x * x, axis=-1, keepdims=True) + RMS_EPS)


def _norm_matmul_kernel(x_ref, g_ref, w_ref, ws_ref, o_ref, os_ref, *rest, head_major):
    hm_refs, xn_ref = rest[:-1], rest[-1]
    j = pl.program_id(1)
    tm, tn = o_ref.shape

    @pl.when(j == 0)
    def _():
        xn_ref[...] = (_rms_scale(x_ref[...]) * g_ref[...]).astype(BF16)
        os_ref[...] = jnp.dot(xn_ref[...], ws_ref[...], preferred_element_type=F32)

    cols = pl.ds(pl.multiple_of(j * tn, tn), tn)
    res = jnp.dot(xn_ref[...], w_ref[:, cols], preferred_element_type=F32)
    o_ref[...] = res
    for ref, start in zip(hm_refs, head_major):
        tile, off = divmod(start, tn)

        @pl.when(j == tile)
        def _():
            for h in range(SB_HEADS):
                ref[pl.ds(h, tm, stride=SB_HEADS), :] = res[:, off + h * SB_HEAD_DIM:off + (h + 1) * SB_HEAD_DIM]


def _norm_matmul(x, g, w, w_side, tm, tn, head_major=()):
    m, d = x.shape
    n = w.shape[1]
    once = pl.Buffered(1)
    return pl.pallas_call(
        functools.partial(_norm_matmul_kernel, head_major=tuple(head_major)),
        grid=(m // tm, n // tn),
        in_specs=[pl.BlockSpec((tm, d), lambda i, j: (i, 0)),
                  pl.BlockSpec((1, d), lambda i, j: (0, 0)),
                  pl.BlockSpec((d, n), lambda i, j: (0, 0), pipeline_mode=once),
                  pl.BlockSpec((d, SIDE_W), lambda i, j: (0, 0), pipeline_mode=once)],
        out_specs=[pl.BlockSpec((tm, tn), lambda i, j: (i, j)),
                   pl.BlockSpec((tm, SIDE_W), lambda i, j: (i, 0))]
        + [pl.BlockSpec((tm * SB_HEADS, SB_HEAD_DIM), lambda i, j: (i, 0)) for _ in head_major],
        out_shape=[jax.ShapeDtypeStruct((m, n), F32),
                   jax.ShapeDtypeStruct((m, SIDE_W), F32)]
        + [jax.ShapeDtypeStruct((m * SB_HEADS, SB_HEAD_DIM), F32) for _ in head_major],
        scratch_shapes=[pltpu.VMEM((tm, d), BF16)],
        compiler_params=_cparams(("parallel", "arbitrary")),
        name="norm_matmul",
    )(x, g, w, w_side)


def _out_proj_kernel(*refs, widths):
    n = len(widths)
    part_refs, w_ref, x_ref, g_ref, o_ref = refs[:n], refs[n], refs[n + 1], refs[n + 2], refs[n + 3]
    acc = None
    off = 0
    for p_ref, wd in zip(part_refs, widths):
        t = jnp.dot(p_ref[...].astype(BF16), w_ref[off:off + wd, :], preferred_element_type=F32)
        acc = t if acc is None else acc + t
        off += wd
    o_ref[...] = x_ref[...] + _rms_scale(acc) * g_ref[...]


def _out_proj(parts, w, x, g, tm):
    m, d = x.shape
    widths = tuple(p.shape[1] for p in parts)
    in_specs = [pl.BlockSpec((tm, wd), lambda i: (i, 0)) for wd in widths]
    in_specs += [pl.BlockSpec(w.shape, lambda i: (0, 0)),
                 pl.BlockSpec((tm, d), lambda i: (i, 0)),
                 pl.BlockSpec((1, d), lambda i: (0, 0))]
    return pl.pallas_call(
        functools.partial(_out_proj_kernel, widths=widths),
        grid=(m // tm,),
        in_specs=in_specs,
        out_specs=pl.BlockSpec((tm, d), lambda i: (i, 0)),
        out_shape=jax.ShapeDtypeStruct((m, d), F32),
        compiler_params=_cparams(("parallel",)),
        name="out_proj",
    )(*parts, w, x, g)


def _mlp_kernel(x_ref, g1_ref, wu_ref, wd_ref, g2_ref, o_ref, xn_ref, acc_ref):
    f = pl.program_id(1)

    @pl.when(f == 0)
    def _():
        xn_ref[...] = (_rms_scale(x_ref[...]) * g1_ref[...]).astype(BF16)
        acc_ref[...] = jnp.zeros_like(acc_ref)

    h = jnp.dot(xn_ref[...], wu_ref[...], preferred_element_type=F32)
    h = jnp.square(jnp.maximum(h, 0.0))
    acc_ref[...] += jnp.dot(h.astype(BF16), wd_ref[...], preferred_element_type=F32)

    @pl.when(f == pl.num_programs(1) - 1)
    def _():
        o_ref[...] = x_ref[...] + _rms_scale(acc_ref[...]) * g2_ref[...]


def _mlp(x, g1, wu, wd, g2, tm, tf):
    m, d = x.shape
    ff = wu.shape[1]
    return pl.pallas_call(
        _mlp_kernel,
        grid=(m // tm, ff // tf),
        in_specs=[pl.BlockSpec((tm, d), lambda i, f: (i, 0)),
                  pl.BlockSpec((1, d), lambda i, f: (0, 0)),
                  pl.BlockSpec((d, tf), lambda i, f: (0, f)),
                  pl.BlockSpec((tf, d), lambda i, f: (f, 0)),
                  pl.BlockSpec((1, d), lambda i, f: (0, 0))],
        out_specs=pl.BlockSpec((tm, d), lambda i, f: (i, 0)),
        out_shape=jax.ShapeDtypeStruct((m, d), F32),
        scratch_shapes=[pltpu.VMEM((tm, d), BF16), pltpu.VMEM((tm, d), F32)],
        compiler_params=_cparams(("parallel", "arbitrary")),
        name="mlp",
    )(x, g1, wu, wd, g2)


def _gla_kernel(q_ref, k_ref, v_ref, r_ref, glr_ref, w2_ref, bg_ref, g_ref, s0_ref,
                o_ref, s_ref, st_ref, b_ref, hi_ref, lo_ref, *, sub, unroll):
    tb = pl.program_id(1)
    t_rows = q_ref.shape[0]
    nsub = t_rows // sub
    npair = GLA_HEADS // 2

    @pl.when(tb == 0)
    def _():
        st_ref[...] = s0_ref[0]

    la = _log_sigmoid(_dot(glr_ref[...], w2_ref[...]) + bg_ref[...]) * (1.0 / GLA_GATE_TEMP)
    row = lax.broadcasted_iota(jnp.int32, (t_rows, t_rows), 0)
    col = lax.broadcasted_iota(jnp.int32, (t_rows, t_rows), 1)
    tri = jnp.where(((row // sub) == (col // sub)) & (col <= row), 1.0, 0.0).astype(BF16)
    b_ref[...] = _sel_dot(tri, la)
    la_hi = la.astype(BF16)
    hi_ref[...] = la_hi
    lo_ref[...] = (la - la_hi.astype(F32)).astype(BF16)
    tn = lambda a, b: lax.dot_general(a, b, (((0,), (0,)), ((), ())), preferred_element_type=F32)
    ones_sub = jnp.ones((sub, LANES), BF16)

    er = lax.broadcasted_iota(jnp.int32, (GLA_QK, GLA_V), 0) // GLA_DK
    ec = lax.broadcasted_iota(jnp.int32, (GLA_QK, GLA_V), 1) // GLA_DV
    expand = jnp.where(er == ec, 1.0, 0.0).astype(BF16)
    t_iota = lax.broadcasted_iota(jnp.int32, (sub, GLA_QK), 0)
    zeros_blk = jnp.zeros((GLA_DK, GLA_DV), F32)

    def sub_chunk(i):
        rows = pl.ds(pl.multiple_of(i * sub, sub), sub)
        b_i = b_ref[rows, :]
        q_i = q_ref[rows, :] * (GLA_DK ** -0.5)
        k_i = k_ref[rows, :]
        v_i = v_ref[rows, :]
        dfull = jnp.exp(tn(hi_ref[rows, :], ones_sub) + tn(lo_ref[rows, :], ones_sub))
        qe = (q_i * jnp.exp(b_i)).astype(BF16)
        kd = (k_i * jnp.exp(b_i[sub - 1:sub, :] - b_i)).astype(BF16)
        v_bf = v_i.astype(BF16)
        inter = []
        for p in range(npair):
            ha, hb = 2 * p, 2 * p + 1
            s_a, s_b = st_ref[ha], st_ref[hb]
            lanes = slice(p * LANES, (p + 1) * LANES)
            w_pair = jnp.concatenate([jnp.concatenate([s_a, zeros_blk], axis=1),
                                      jnp.concatenate([zeros_blk, s_b], axis=1)], axis=0)
            inter.append(jnp.dot(qe[:, lanes], w_pair.astype(BF16), preferred_element_type=F32))
            u = tn(kd[:, lanes], v_bf[:, ha * GLA_DV:(hb + 1) * GLA_DV])
            st_ref[ha] = s_a * dfull[ha * GLA_DK:(ha + 1) * GLA_DK, :] + u[:GLA_DK, :GLA_DV]
            st_ref[hb] = s_b * dfull[hb * GLA_DK:(hb + 1) * GLA_DK, :] + u[GLA_DK:, GLA_DV:]
        prods = []
        for j in range(sub):
            e = jnp.exp(jnp.where(t_iota >= j, b_i - b_i[j:j + 1, :], NEG_INF))
            prods.append(q_i * e * k_i[j:j + 1, :])
        p_all = jnp.concatenate(prods, axis=0)
        r_all = _dot(p_all, expand)
        o = jnp.concatenate(inter, axis=1)
        for j in range(sub):
            o = o + r_all[j * sub:(j + 1) * sub, :] * v_i[j:j + 1, :]
        r_i = r_ref[rows, :]
        outs = []
        for h in range(GLA_HEADS):
            hv = slice(h * GLA_DV, (h + 1) * GLA_DV)
            outs.append(_rms_scale(o[:, hv]) * g_ref[...] * _silu(r_i[:, hv]))
        o_ref[rows, :] = jnp.concatenate(outs, axis=1).astype(o_ref.dtype)

    def body(it, carry):
        for u in range(unroll):
            sub_chunk(it * unroll + u)
        return carry

    lax.fori_loop(0, nsub // unroll, body, 0)

    @pl.when(tb == pl.num_programs(1) - 1)
    def _():
        s_ref[0] = st_ref[...]


def _gla(proj, side, w2, bg, g, s0, bsz, length, t_rows, sub):
    nt = length // t_rows
    m = bsz * length
    row = lambda b, t: b * nt + t
    return pl.pallas_call(
        functools.partial(_gla_kernel, sub=sub, unroll=16 if (t_rows // sub) % 16 == 0 else 1),
        grid=(bsz, nt),
        in_specs=[pl.BlockSpec((t_rows, GLA_QK), lambda b, t: (row(b, t), 0)),
                  pl.BlockSpec((t_rows, GLA_QK), lambda b, t: (row(b, t), 1)),
                  pl.BlockSpec((t_rows, GLA_V), lambda b, t: (row(b, t), 1)),
                  pl.BlockSpec((t_rows, GLA_V), lambda b, t: (row(b, t), 2)),
                  pl.BlockSpec((t_rows, SIDE_W), lambda b, t: (row(b, t), 0)),
                  pl.BlockSpec((SIDE_W, GLA_QK), lambda b, t: (0, 0)),
                  pl.BlockSpec((1, GLA_QK), lambda b, t: (0, 0)),
                  pl.BlockSpec((1, GLA_DV), lambda b, t: (0, 0)),
                  pl.BlockSpec((1, GLA_HEADS, GLA_DK, GLA_DV), lambda b, t: (b, 0, 0, 0))],
        out_specs=[pl.BlockSpec((t_rows, GLA_V), lambda b, t: (row(b, t), 0)),
                   pl.BlockSpec((1, GLA_HEADS, GLA_DK, GLA_DV), lambda b, t: (b, 0, 0, 0))],
        out_shape=[jax.ShapeDtypeStruct((m, GLA_V), _act_dtype(sub)),
                   jax.ShapeDtypeStruct((bsz, GLA_HEADS, GLA_DK, GLA_DV), F32)],
        scratch_shapes=[pltpu.VMEM((GLA_HEADS, GLA_DK, GLA_DV), F32),
                        pltpu.VMEM((t_rows, GLA_QK), F32),
                        pltpu.VMEM((t_rows, GLA_QK), BF16),
                        pltpu.VMEM((t_rows, GLA_QK), BF16)],
        compiler_params=_cparams(("parallel", "arbitrary")),
        name="gla",
    )(proj, proj, proj, proj, side, w2, bg, g, s0)


LOG2_E = 1.4426950408889634
SB_LOGIT_SCALE = SB_HEAD_DIM ** -0.5 * LOG2_E
SB_T_MAX = 64.0


def _suffix_matrix(tk):
    jr = lax.broadcasted_iota(jnp.int32, (tk, tk + LANES), 0)
    jc = lax.broadcasted_iota(jnp.int32, (tk, tk + LANES), 1)
    return jnp.where((jr > jc) | (jc >= tk), 1.0, 0.0).astype(BF16)


def _sb_logits(t, mask):
    n = jnp.where(t > SB_T_MAX, t, jnp.log2(1.0 + jnp.exp2(t)))
    log_b = t - n
    if mask is not None:
        n = jnp.where(mask, n, 0.0)
    return log_b, n.astype(BF16)


def _sb_weights(log_b, n, suffix, carry, mask):
    tk = log_b.shape[1]
    sums = jnp.dot(n, suffix, preferred_element_type=F32)
    a = jnp.exp2(log_b - sums[:, :tk] - jnp.concatenate([carry] * (tk // LANES), axis=1))
    if mask is not None:
        a = jnp.where(mask, a, 0.0)
    return a.astype(BF16), carry + sums[:, tk:]


def _sb_tile(q_bf, k_blk, v_blk, bias2, suffix, carry, acc, mask):
    log_b, n = _sb_logits(_dot_nt(q_bf, k_blk) * SB_LOGIT_SCALE + bias2, mask)
    a, carry = _sb_weights(log_b, n, suffix, carry, mask)
    return carry, acc + jnp.dot(a, v_blk.astype(BF16), preferred_element_type=F32)


def _sb_prompt_kernel(bias_ref, q_ref, k_ref, v_ref, o_ref, lb_ref, n_ref, a_ref, carry_ref, acc_ref,
                      *, tk, nsplit, unroll):
    h = pl.program_id(1)
    qi = pl.program_id(2)
    tq = q_ref.shape[0]
    th = tq // nsplit
    bias2 = bias_ref[h] * LOG2_E
    suffix = _suffix_matrix(tk)
    q0 = qi * tq
    q_bf = [q_ref[s * th:(s + 1) * th, :].astype(BF16) for s in range(nsplit)]
    zero = jnp.zeros((th, LANES), F32)
    carry = [zero] * nsplit
    acc = [zero] * nsplit
    row = lax.broadcasted_iota(jnp.int32, (th, tk), 0)
    col = lax.broadcasted_iota(jnp.int32, (th, tk), 1)

    for d in reversed(range(tq // tk)):
        k0 = pl.multiple_of(q0 + d * tk, tk)
        k_blk = k_ref[pl.ds(k0, tk), :]
        v_blk = v_ref[pl.ds(k0, tk), :]
        for s in range(nsplit):
            if d * tk >= (s + 1) * th:
                continue
            mask = None if (d + 1) * tk <= s * th else (col + d * tk) < (row + s * th)
            carry[s], acc[s] = _sb_tile(q_bf[s], k_blk, v_blk, bias2, suffix, carry[s], acc[s], mask)
    for s in range(nsplit):
        carry_ref[s * th:(s + 1) * th, :] = carry[s]
        acc_ref[s * th:(s + 1) * th, :] = acc[s]

    q_all = q_ref[...].astype(BF16)
    nfull = q0 // tk

    def key_rows(f):
        return pl.ds(pl.multiple_of(jnp.maximum(nfull - 1 - f, 0) * tk, tk), tk)

    def stage_logits(f, slot):
        log_b, n = _sb_logits(_dot_nt(q_all, k_ref[key_rows(f), :]) * SB_LOGIT_SCALE + bias2, None)
        lb_ref[slot] = log_b
        n_ref[slot] = n

    def stage_weights(slot):
        a, carry = _sb_weights(lb_ref[slot], n_ref[slot], suffix, carry_ref[...], None)
        a_ref[slot] = a
        carry_ref[...] = carry

    def stage_values(f, slot):
        acc_ref[...] += jnp.dot(a_ref[slot], v_ref[key_rows(f), :].astype(BF16), preferred_element_type=F32)

    for u in range(2 * unroll):
        stage_logits(u, u)
    for u in range(unroll):
        stage_weights(u)

    def body(it, c):
        for parity in range(2):
            cur = parity * unroll
            nxt = unroll - cur
            f0 = (2 * it + parity) * unroll
            for u in range(unroll):
                stage_logits(f0 + 2 * unroll + u, cur + u)
            for u in range(unroll):
                stage_weights(nxt + u)
            for u in range(unroll):
                stage_values(f0 + u, cur + u)
        return c

    lax.fori_loop(0, nfull // (2 * unroll), body, 0)
    o_ref[...] = acc_ref[...].astype(o_ref.dtype)


def _sb_prompt(proj, bias, bsz, length, tq, tk, nsplit, unroll):
    m = bsz * length
    nq = length // tq
    qcol = (2 * GLA_QK + 2 * GLA_V) // SB_HEAD_DIM
    kcol = qcol + SB_HEADS
    vcol = kcol + SB_HEADS
    return pl.pallas_call(
        functools.partial(_sb_prompt_kernel, tk=tk, nsplit=nsplit, unroll=unroll),
        grid=(bsz, SB_HEADS, nq),
        in_specs=[pl.BlockSpec(memory_space=pltpu.SMEM),
                  pl.BlockSpec((tq, SB_HEAD_DIM), lambda b, h, i: (b * nq + i, qcol + h)),
                  pl.BlockSpec((length, SB_HEAD_DIM), lambda b, h, i: (b, kcol + h)),
                  pl.BlockSpec((length, SB_HEAD_DIM), lambda b, h, i: (b, vcol + h))],
        out_specs=pl.BlockSpec((tq, SB_HEAD_DIM), lambda b, h, i: (b * nq + i, h)),
        out_shape=jax.ShapeDtypeStruct((m, SB_W), _act_dtype(tq)),
        scratch_shapes=[pltpu.VMEM((2 * unroll, tq, tk), F32),
                        pltpu.VMEM((2 * unroll, tq, tk), BF16),
                        pltpu.VMEM((2 * unroll, tq, tk), BF16),
                        pltpu.VMEM((tq, LANES), F32),
                        pltpu.VMEM((tq, SB_HEAD_DIM), F32)],
        compiler_params=_cparams(("parallel", "parallel", "arbitrary")),
        name="sb_prompt",
    )(bias, proj, proj, proj)


def _sb_paged_kernel(pt_ref, bias_ref, q_ref, kn_ref, vn_ref, *refs, pages_per_step):
    g = pages_per_step
    k_refs, v_refs = refs[:g], refs[g:2 * g]
    o_ref, carry_ref, acc_ref, kpad_ref, vpad_ref = refs[2 * g:]
    j = pl.program_id(1)
    lq = q_ref.shape[0]
    nrow = SB_HEADS * lq
    suffix = _suffix_matrix(PAGE_SIZE)
    rhead = lax.broadcasted_iota(jnp.int32, (nrow, PAGE_SIZE), 0) // lq
    bias2 = jnp.zeros((nrow, PAGE_SIZE), F32)
    for h in range(SB_HEADS):
        bias2 = jnp.where(rhead == h, bias_ref[h] * LOG2_E, bias2)
    q_bf = [q_ref[:, h * SB_HEAD_DIM:(h + 1) * SB_HEAD_DIM].astype(BF16) for h in range(SB_HEADS)]

    def tile(k_heads, v_heads, carry, acc, mask):
        z = jnp.concatenate([_dot_nt(q_bf[h], k_heads[h]) for h in range(SB_HEADS)], axis=0)
        log_b, n = _sb_logits(z * SB_LOGIT_SCALE + bias2, mask)
        a, carry = _sb_weights(log_b, n, suffix, carry, mask)
        acc = [acc[h] + _dot(a[h * lq:(h + 1) * lq, :], v_heads[h]) for h in range(SB_HEADS)]
        return carry, acc

    @pl.when(j == 0)
    def _():
        kpad_ref[...] = jnp.zeros_like(kpad_ref)
        vpad_ref[...] = jnp.zeros_like(vpad_ref)
        for h in range(SB_HEADS):
            hs = slice(h * SB_HEAD_DIM, (h + 1) * SB_HEAD_DIM)
            kpad_ref[h, 0:lq, :] = kn_ref[:, hs]
            vpad_ref[h, 0:lq, :] = vn_ref[:, hs]
        qidx = lax.broadcasted_iota(jnp.int32, (nrow, PAGE_SIZE), 0) % lq
        kidx = lax.broadcasted_iota(jnp.int32, (nrow, PAGE_SIZE), 1)
        carry, acc = tile([kpad_ref[h] for h in range(SB_HEADS)], [vpad_ref[h] for h in range(SB_HEADS)],
                          jnp.zeros((nrow, PAGE_SIZE), F32),
                          [jnp.zeros((lq, SB_HEAD_DIM), F32)] * SB_HEADS, kidx < qidx)
        carry_ref[...] = carry
        acc_ref[...] = jnp.concatenate(acc, axis=0)

    nchunk = (PAGE_SIZE * SB_HEADS) // LANES
    ntile = g * nchunk
    q_stack = jnp.concatenate(q_bf, axis=0)
    own = (lax.broadcasted_iota(jnp.int32, (nrow, LANES), 1) % SB_HEADS) == rhead
    tiles = []
    for p in range(g):
        z = _dot_nt(q_stack, k_refs[p][0])
        tiles += [z[:, c * LANES:(c + 1) * LANES] for c in reversed(range(nchunk))]
    z = jnp.concatenate(tiles, axis=0).reshape(ntile, nrow, LANES)
    log_b, n = _sb_logits(z * SB_LOGIT_SCALE + bias2[None], own[None])
    sums = jnp.dot(n.reshape(ntile * nrow, LANES), suffix, preferred_element_type=F32)
    sums = sums.reshape(ntile, nrow, 2 * LANES)
    carry = carry_ref[...]
    carries = []
    for t in range(ntile):
        carries.append(carry)
        carry = carry + sums[t, :, LANES:]
    carry_ref[...] = carry
    a = jnp.exp2(log_b - sums[:, :, :LANES] - jnp.stack(carries, axis=0))
    a = jnp.where(own[None], a, 0.0).astype(BF16)
    acc = acc_ref[...]
    for p in range(g):
        a_page = jnp.concatenate([a[p * nchunk + (nchunk - 1 - c)] for c in range(nchunk)], axis=1)
        acc = acc + jnp.dot(a_page, v_refs[p][0].astype(BF16), preferred_element_type=F32)
    acc_ref[...] = acc

    @pl.when(j == pl.num_programs(1) - 1)
    def _():
        for h in range(SB_HEADS):
            o_ref[:, h * SB_HEAD_DIM:(h + 1) * SB_HEAD_DIM] = acc[h * lq:(h + 1) * lq, :].astype(o_ref.dtype)


def _sb_paged(proj, bias, pool_k, pool_v, page_table, bsz, lq, pages_per_step):
    n_pages = page_table.shape[1]
    g = pages_per_step
    nsteps = n_pages // g
    qcol = (2 * GLA_QK + 2 * GLA_V) // SB_W

    def page_spec(p):
        return pl.BlockSpec((1, PAGE_SIZE * SB_HEADS, SB_HEAD_DIM),
                            lambda b, j, pt: (pt[b, n_pages - 1 - (j * g + p)], 0, 0))

    grid_spec = pltpu.PrefetchScalarGridSpec(
        num_scalar_prefetch=1,
        grid=(bsz, nsteps),
        in_specs=[pl.BlockSpec(memory_space=pltpu.SMEM),
                  pl.BlockSpec((lq, SB_W), lambda b, j, pt: (b, qcol)),
                  pl.BlockSpec((lq, SB_W), lambda b, j, pt: (b, qcol + 1)),
                  pl.BlockSpec((lq, SB_W), lambda b, j, pt: (b, qcol + 2))]
        + [page_spec(p) for p in range(g)] + [page_spec(p) for p in range(g)],
        out_specs=pl.BlockSpec((lq, SB_W), lambda b, j, pt: (b, 0)),
        scratch_shapes=[pltpu.VMEM((SB_HEADS * lq, PAGE_SIZE), F32),
                        pltpu.VMEM((SB_HEADS * lq, SB_HEAD_DIM), F32),
                        pltpu.VMEM((SB_HEADS, PAGE_SIZE, SB_HEAD_DIM), F32),
                        pltpu.VMEM((SB_HEADS, PAGE_SIZE, SB_HEAD_DIM), F32)])
    return pl.pallas_call(
        functools.partial(_sb_paged_kernel, pages_per_step=g),
        grid_spec=grid_spec,
        out_shape=jax.ShapeDtypeStruct((bsz * lq, SB_W), _act_dtype(lq)),
        compiler_params=_cparams(("parallel", "arbitrary")),
        name="sb_paged",
    )(page_table, bias, proj, proj, proj, *([pool_k] * g), *([pool_v] * g))


def _ssd_chunk(r, xc_ref, z_ref, dtr_ref, dtb_ref, dsk_ref, hs_ref, ypre_ref, a, causal, causal_bf,
               lo_lanes, lo_rows):
    c = r.stop - r.start
    hpg = SSD_HEADS // SSD_GROUPS
    xc = xc_ref[r, :]
    dt = _softplus(dtr_ref[r, :] + dtb_ref[...])
    cum = _sel_dot(causal_bf, dt * a)
    cum_t = _transpose_f32(cum)
    dt_t = _transpose_f32(dt)
    last = cum[c - 1:c, :]
    e_cum = jnp.exp(cum)
    dec = jnp.exp(last - cum) * dt
    e_last_t = jnp.exp(cum_t[:, c - 1:c])

    for g in range(SSD_GROUPS):
        b_g = xc[:, SSD_INNER + g * SSD_STATE:SSD_INNER + (g + 1) * SSD_STATE]
        c_g = xc[:, SSD_INNER + (SSD_GROUPS + g) * SSD_STATE:SSD_INNER + (SSD_GROUPS + g + 1) * SSD_STATE]
        cb = _dot_nt(c_g, b_g)
        for pp in range(hpg // 2):
            pair = g * (hpg // 2) + pp
            h0, h1 = 2 * pair, 2 * pair + 1
            x_pair = xc[:, pair * LANES:(pair + 1) * LANES]
            hs = hs_ref[pair]
            y_pair = _dot_nt(c_g, hs) * jnp.where(lo_lanes, e_cum[:, h0:h0 + 1], e_cum[:, h1:h1 + 1])
            x_bf = x_pair.astype(BF16)
            intra = []
            for hh in (h0, h1):
                lmat = jnp.exp(jnp.where(causal, cum[:, hh:hh + 1] - cum_t[hh:hh + 1, :], NEG_INF))
                w = lmat * dt_t[hh:hh + 1, :] * cb
                intra.append(jnp.dot(w.astype(BF16), x_bf, preferred_element_type=F32))
            y_pair = y_pair + jnp.where(lo_lanes, intra[0], intra[1])
            xd = x_pair * jnp.where(lo_lanes, dec[:, h0:h0 + 1], dec[:, h1:h1 + 1])
            scale = jnp.where(lo_rows, e_last_t[h0:h0 + 1, :], e_last_t[h1:h1 + 1, :])
            upd = _dot(xd.T, b_g) if c % LANES == 0 else _dot_tn(xd, b_g)
            hs_ref[pair] = scale * hs + upd
            y_pair = y_pair + x_pair * dsk_ref[:, pair * LANES:(pair + 1) * LANES]
            zz = z_ref[r, pair * LANES:(pair + 1) * LANES]
            ypre_ref[r, pair * LANES:(pair + 1) * LANES] = y_pair * _silu(zz)


def _ssd_kernel(xbc_ref, z_ref, dtr_ref, cw_ref, cb_ref, dtb_ref, alog_ref, dsk_ref, ng_ref,
                conv0_ref, h0_ref, y_ref, hout_ref, xext_ref, hs_ref, ypre_ref, xc_ref, *, chunk):
    ci = pl.program_id(1)
    rows = xbc_ref.shape[0]
    c = chunk
    tail = SSD_CONV - 1
    pad = 8

    @pl.when(ci == 0)
    def _():
        xext_ref[0:pad, :] = conv0_ref[0]
        hs_ref[...] = h0_ref[0]

    xext_ref[pad:pad + rows, :] = xbc_ref[...]
    xfull = xext_ref[...]
    acc = cb_ref[...] + xfull[pad:, :] * cw_ref[tail:tail + 1, :]
    for w in range(tail):
        acc = acc + pltpu.roll(xfull, tail - w, axis=0)[pad:, :] * cw_ref[w:w + 1, :]
    xc_ref[...] = _silu(acc)
    xext_ref[0:pad, :] = xext_ref[rows:rows + pad, :]

    a = -jnp.exp(alog_ref[...])
    row = lax.broadcasted_iota(jnp.int32, (c, c), 0)
    col = lax.broadcasted_iota(jnp.int32, (c, c), 1)
    causal = col <= row
    causal_bf = jnp.where(causal, 1.0, 0.0).astype(BF16)
    lo_lanes = lax.broadcasted_iota(jnp.int32, (c, LANES), 1) < SSD_HEAD_DIM
    lo_rows = lax.broadcasted_iota(jnp.int32, (LANES, LANES), 0) < SSD_HEAD_DIM
    for sc in range(rows // c):
        _ssd_chunk(slice(sc * c, (sc + 1) * c), xc_ref, z_ref, dtr_ref, dtb_ref, dsk_ref, hs_ref, ypre_ref,
                   a, causal, causal_bf, lo_lanes, lo_rows)
    gw = SSD_INNER // SSD_GROUPS
    for g in range(SSD_GROUPS):
        gs = slice(g * gw, (g + 1) * gw)
        y_ref[:, gs] = (_rms_scale(ypre_ref[:, gs]) * ng_ref[:, gs]).astype(y_ref.dtype)

    @pl.when(ci == pl.num_programs(1) - 1)
    def _():
        hout_ref[0] = hs_ref[...]


def _ssd(zx, side, cw, cb, dtb, alog, dsk, ng, conv0, h0, bsz, length, chunk, rows):
    nc = length // rows
    m = bsz * length
    row = lambda b, c: b * nc + c
    const = lambda b, c: (0, 0)
    npair = SSD_HEADS // 2
    return pl.pallas_call(
        functools.partial(_ssd_kernel, chunk=chunk),
        grid=(bsz, nc),
        in_specs=[pl.BlockSpec((rows, SSD_CONV_DIM), lambda b, c: (row(b, c), 0)),
                  pl.BlockSpec((rows, SSD_INNER), lambda b, c: (row(b, c), SSD_CONV_DIM // SSD_INNER)),
                  pl.BlockSpec((rows, SIDE_W), lambda b, c: (row(b, c), 0)),
                  pl.BlockSpec((SSD_CONV, SSD_CONV_DIM), const),
                  pl.BlockSpec((1, SSD_CONV_DIM), const),
                  pl.BlockSpec((1, SIDE_W), const),
                  pl.BlockSpec((1, SIDE_W), const),
                  pl.BlockSpec((1, SSD_INNER), const),
                  pl.BlockSpec((1, SSD_INNER), const),
                  pl.BlockSpec((1, 8, SSD_CONV_DIM), lambda b, c: (b, 0, 0)),
                  pl.BlockSpec((1, npair, LANES, SSD_STATE), lambda b, c: (b, 0, 0, 0))],
        out_specs=[pl.BlockSpec((rows, SSD_INNER), lambda b, c: (row(b, c), 0)),
                   pl.BlockSpec((1, npair, LANES, SSD_STATE), lambda b, c: (b, 0, 0, 0))],
        out_shape=[jax.ShapeDtypeStruct((m, SSD_INNER), _act_dtype(chunk)),
                   jax.ShapeDtypeStruct((bsz, npair, LANES, SSD_STATE), F32)],
        scratch_shapes=[pltpu.VMEM((rows + 8, SSD_CONV_DIM), F32),
                        pltpu.VMEM((npair, LANES, SSD_STATE), F32),
                        pltpu.VMEM((rows, SSD_INNER), F32),
                        pltpu.VMEM((rows, SSD_CONV_DIM), F32)],
        compiler_params=_cparams(("parallel", "arbitrary")),
        name="ssd",
    )(zx, zx, side, cw, cb, dtb, alog, dsk, ng, conv0, h0)


def _pad_cols(w, width):
    return jnp.pad(w, ((0, 0), (0, width - w.shape[1])))


def _row(v, width=None):
    v = v.reshape(1, -1).astype(F32)
    return v if width is None else _pad_cols(v, width)


def _prep_params(p):
    w_in = p["w_in_even"][0]
    n_gla = 2 * GLA_QK + GLA_V
    q = {}
    q["even_main"] = jnp.concatenate([w_in[:, :n_gla], w_in[:, n_gla + GLA_GATE_RANK:]], axis=1).astype(BF16)
    q["even_side"] = _pad_cols(w_in[:, n_gla:n_gla + GLA_GATE_RANK], SIDE_W).astype(BF16)
    q["gate_w2"] = jnp.pad(p["gla_w_gate2"][0], ((0, SIDE_W - GLA_GATE_RANK), (0, 0))).astype(BF16)
    q["gate_b"] = _row(p["gla_b_gate"][0])
    q["gla_g"] = _row(p["gla_norm_g"][0])
    q["sb_bias"] = p["sb_logit_bias"][0].astype(F32)
    q["even_out"] = p["w_out_even"][0].astype(BF16)
    w_ssd = p["ssd_w_in"][0]
    q["ssd_main"] = jnp.concatenate([w_ssd[:, SSD_INNER:SSD_INNER + SSD_CONV_DIM], w_ssd[:, :SSD_INNER]],
                                    axis=1).astype(BF16)
    q["ssd_side"] = _pad_cols(w_ssd[:, SSD_INNER + SSD_CONV_DIM:], SIDE_W).astype(BF16)
    q["conv_w"] = p["ssd_conv_w"][0].astype(F32)
    q["conv_b"] = _row(p["ssd_conv_b"][0])
    q["dt_bias"] = _row(p["ssd_dt_bias"][0], SIDE_W)
    q["a_log"] = _row(p["ssd_a_log"][0], SIDE_W)
    q["d_skip"] = _row(jnp.repeat(p["ssd_d"][0], SSD_HEAD_DIM))
    q["ssd_g"] = _row(p["ssd_norm_g"][0])
    q["ssd_out"] = p["ssd_w_out"][0].astype(BF16)
    for name in ("norm_mix_pre", "norm_mix_post", "norm_mlp_pre", "norm_mlp_post"):
        q[name] = [_row(p[name][li]) for li in range(2)]
    q["mlp_up"] = [p["mlp_w_up"][li].astype(BF16) for li in range(2)]
    q["mlp_down"] = [p["mlp_w_down"][li].astype(BF16) for li in range(2)]
    return q


def _trunk(x, paged, gla_s0, ssm_h0, conv0, q, cfg):
    bsz, length, d = x.shape
    m = bsz * length
    x2 = x.reshape(m, d)

    kcol = 2 * GLA_QK + 2 * GLA_V + SB_W
    proj, side, sb_k, sb_v = _norm_matmul(x2, q["norm_mix_pre"][0], q["even_main"], q["even_side"],
                                          cfg["tm_mm"], cfg["tn_even"], head_major=(kcol, kcol + SB_W))
    o_gla, gla_new = _gla(proj, side, q["gate_w2"], q["gate_b"], q["gla_g"], gla_s0,
                          bsz, length, cfg["gla_rows"], cfg["gla_sub"])
    if paged is None:
        o_sb = _sb_prompt(proj, q["sb_bias"], bsz, length, cfg["sb_tq"], cfg["sb_tk"], cfg["sb_split"],
                          cfg["sb_unroll"])
    else:
        o_sb = _sb_paged(proj, q["sb_bias"], paged[0], paged[1], paged[2], bsz, length, cfg["sb_pages"])
    sb_k = sb_k.reshape(bsz, length, SB_HEADS, SB_HEAD_DIM)
    sb_v = sb_v.reshape(bsz, length, SB_HEADS, SB_HEAD_DIM)
    x2 = _out_proj([o_gla, o_sb], q["even_out"], x2, q["norm_mix_post"][0], cfg["tm_mm"])
    x2 = _mlp(x2, q["norm_mlp_pre"][0], q["mlp_up"][0], q["mlp_down"][0], q["norm_mlp_post"][0],
              cfg["tm_mm"], 1024)

    zx, dtr = _norm_matmul(x2, q["norm_mix_pre"][1], q["ssd_main"], q["ssd_side"], cfg["tm_mm"], cfg["tn_ssd"])
    conv_pad = jnp.pad(conv0, ((0, 0), (8 - (SSD_CONV - 1), 0), (0, 0)))
    h0 = ssm_h0.reshape(bsz, SSD_HEADS // 2, 2 * SSD_HEAD_DIM, SSD_STATE)
    y, h_new = _ssd(zx, dtr, q["conv_w"], q["conv_b"], q["dt_bias"], q["a_log"], q["d_skip"], q["ssd_g"],
                    conv_pad, h0, bsz, length, cfg["ssd_chunk"], cfg["ssd_rows"])
    conv_new = zx.reshape(bsz, length, SSD_MAIN)[:, length - (SSD_CONV - 1):, :SSD_CONV_DIM]
    x2 = _out_proj([y], q["ssd_out"], x2, q["norm_mix_post"][1], cfg["tm_mm"])
    x2 = _mlp(x2, q["norm_mlp_pre"][1], q["mlp_up"][1], q["mlp_down"][1], q["norm_mlp_post"][1],
              cfg["tm_mm"], 1024)

    return (x2.reshape(bsz, length, d), sb_k[None], sb_v[None], gla_new[None],
            h_new.reshape(bsz, SSD_HEADS, SSD_HEAD_DIM, SSD_STATE)[None], conv_new[None])


def _prompt_cfg(length):
    return dict(tn_even=EVEN_MAIN // 2, tn_ssd=SSD_MAIN // 3, tm_mm=min(1024, 2 * length),
                gla_rows=min(256, length), gla_sub=16,
                sb_tq=min(512, length), sb_tk=128, sb_split=2, sb_unroll=2,
                ssd_chunk=min(128, length), ssd_rows=min(256, length))


def _sample_cfg(bsz, length, n_pages):
    return dict(tn_even=EVEN_MAIN // 2, tn_ssd=SSD_MAIN // 3, tm_mm=bsz * length, gla_rows=length, gla_sub=length, sb_pages=min(32, n_pages),
                ssd_chunk=length, ssd_rows=length)


def kernel(x_prompt, x_sample, cache_sb_k, cache_sb_v, state_gla, state_ssm, state_conv, page_table, w_in_even, gla_w_gate2, gla_b_gate, gla_norm_g, sb_logit_bias, w_out_even, ssd_w_in, ssd_conv_w, ssd_conv_b, ssd_dt_bias, ssd_a_log, ssd_d, ssd_norm_g, ssd_w_out, norm_mix_pre, norm_mix_post, norm_mlp_pre, norm_mlp_post, mlp_w_up, mlp_w_down):
    q = _prep_params(dict(
        w_in_even=w_in_even, gla_w_gate2=gla_w_gate2, gla_b_gate=gla_b_gate, gla_norm_g=gla_norm_g,
        sb_logit_bias=sb_logit_bias, w_out_even=w_out_even, ssd_w_in=ssd_w_in, ssd_conv_w=ssd_conv_w,
        ssd_conv_b=ssd_conv_b, ssd_dt_bias=ssd_dt_bias, ssd_a_log=ssd_a_log, ssd_d=ssd_d,
        ssd_norm_g=ssd_norm_g, ssd_w_out=ssd_w_out, norm_mix_pre=norm_mix_pre, norm_mix_post=norm_mix_post,
        norm_mlp_pre=norm_mlp_pre, norm_mlp_post=norm_mlp_post, mlp_w_up=mlp_w_up, mlp_w_down=mlp_w_down))
    bp, lp, _ = x_prompt.shape
    bs, ls, _ = x_sample.shape
    zeros = lambda *s: jnp.zeros(s, F32)
    prompt = _trunk(x_prompt, None, zeros(bp, GLA_HEADS, GLA_DK, GLA_DV),
                    zeros(bp, SSD_HEADS, SSD_HEAD_DIM, SSD_STATE), zeros(bp, SSD_CONV - 1, SSD_CONV_DIM),
                    q, _prompt_cfg(lp))
    n_pool = cache_sb_k.shape[1]
    pool_k = cache_sb_k.reshape(n_pool, PAGE_SIZE * SB_HEADS, SB_HEAD_DIM)
    pool_v = cache_sb_v.reshape(n_pool, PAGE_SIZE * SB_HEADS, SB_HEAD_DIM)
    sample = _trunk(x_sample, (pool_k, pool_v, page_table), state_gla[0], state_ssm[0], state_conv[0],
                    q, _sample_cfg(bs, ls, page_table.shape[1]))
    return (prompt[0], sample[0]) + prompt[1:] + sample[1:]
```

```python
import functools

import jax
import jax.numpy as jnp
from jax import lax
from jax.experimental import pallas as pl
from jax.experimental.pallas import tpu as pltpu

F32 = jnp.float32
BF16 = jnp.bfloat16

D_MODEL = 1024
RMS_EPS = 1e-6
D_FF = 4 * D_MODEL
GLA_HEADS = 4
GLA_DK = 64
GLA_DV = 128
GLA_GATE_RANK = 16
GLA_GATE_TEMP = 16.0
GLA_QK = GLA_HEADS * GLA_DK
GLA_V = GLA_HEADS * GLA_DV
SB_HEADS = 4
SB_HEAD_DIM = 128
SB_W = SB_HEADS * SB_HEAD_DIM
PAGE_SIZE = 128
EVEN_MAIN = 2 * GLA_QK + 2 * GLA_V + 3 * SB_W
SSD_INNER = 2048
SSD_HEAD_DIM = 64
SSD_HEADS = 32
SSD_GROUPS = 8
SSD_STATE = 128
SSD_CONV = 4
SSD_CONV_DIM = SSD_INNER + 2 * SSD_GROUPS * SSD_STATE
SSD_MAIN = SSD_CONV_DIM + SSD_INNER
LANES = 128
SIDE_W = LANES
VMEM_LIMIT = 48 * 1024 * 1024

NEG_INF = float("-inf")


def _act_dtype(rows):
    return BF16 if rows % 16 == 0 else F32


def _cparams(sem):
    return pltpu.CompilerParams(dimension_semantics=sem, vmem_limit_bytes=VMEM_LIMIT)


def _dot(a, b):
    return jnp.dot(a.astype(BF16), b.astype(BF16), preferred_element_type=F32)


def _dot_nt(a, b):
    return lax.dot_general(a.astype(BF16), b.astype(BF16), (((1,), (1,)), ((), ())),
                           preferred_element_type=F32)


def _dot_tn(a, b):
    return lax.dot_general(a.astype(BF16), b.astype(BF16), (((0,), (0,)), ((), ())),
                           preferred_element_type=F32)


def _split3(x):
    hi = x.astype(BF16)
    r1 = x - hi.astype(F32)
    mid = r1.astype(BF16)
    lo = (r1 - mid.astype(F32)).astype(BF16)
    return hi, mid, lo


def _sel_dot(sel, x):
    hi, mid, lo = _split3(x)
    d = functools.partial(jnp.dot, preferred_element_type=F32)
    return d(sel, hi) + d(sel, mid) + d(sel, lo)


def _transpose_f32(x):
    n = x.shape[1]
    eye = (lax.broadcasted_iota(jnp.int32, (n, n), 0)
           == lax.broadcasted_iota(jnp.int32, (n, n), 1)).astype(BF16)
    hi, mid, lo = _split3(x)
    d = lambda p: lax.dot_general(eye, p, (((1,), (1,)), ((), ())), preferred_element_type=F32)
    return d(hi) + d(mid) + d(lo)


def _softplus_neg_abs(x):
    e = jnp.exp(-jnp.abs(x))
    u = 1.0 + e
    return jnp.where(u == 1.0, e, jnp.log(u) * (e / (u - 1.0)))


def _log_sigmoid(x):
    return jnp.minimum(x, 0.0) - _softplus_neg_abs(x)


def _softplus(x):
    return jnp.maximum(x, 0.0) + _softplus_neg_abs(x)


def _silu(x):
    h = 0.5 * x
    return h + h * jnp.tanh(h)


def _rms_scale(x):
    return x * lax.rsqrt(jnp.mean(x * x, axis=-1, keepdims=True) + RMS_EPS)


def _norm_matmul_kernel(x_ref, g_ref, w_ref, ws_ref, o_ref, os_ref, *rest, head_major):
    hm_refs, xn_ref = rest[:-1], rest[-1]
    j = pl.program_id(1)
    tm, tn = o_ref.shape

    @pl.when(j == 0)
    def _():
        xn_ref[...] = (_rms_scale(x_ref[...]) * g_ref[...]).astype(BF16)
        os_ref[...] = jnp.dot(xn_ref[...], ws_ref[...], preferred_element_type=F32)

    cols = pl.ds(pl.multiple_of(j * tn, tn), tn)
    res = jnp.dot(xn_ref[...], w_ref[:, cols], preferred_element_type=F32)
    o_ref[...] = res
    for ref, start in zip(hm_refs, head_major):
        tile, off = divmod(start, tn)

        @pl.when(j == tile)
        def _():
            for h in range(SB_HEADS):
                ref[pl.ds(h, tm, stride=SB_HEADS), :] = res[:, off + h * SB_HEAD_DIM:off + (h + 1) * SB_HEAD_DIM]


def _norm_matmul(x, g, w, w_side, tm, tn, head_major=()):
    m, d = x.shape
    n = w.shape[1]
    once = pl.Buffered(1)
    return pl.pallas_call(
        functools.partial(_norm_matmul_kernel, head_major=tuple(head_major)),
        grid=(m // tm, n // tn),
        in_specs=[pl.BlockSpec((tm, d), lambda i, j: (i, 0)),
                  pl.BlockSpec((1, d), lambda i, j: (0, 0)),
                  pl.BlockSpec((d, n), lambda i, j: (0, 0), pipeline_mode=once),
                  pl.BlockSpec((d, SIDE_W), lambda i, j: (0, 0), pipeline_mode=once)],
        out_specs=[pl.BlockSpec((tm, tn), lambda i, j: (i, j)),
                   pl.BlockSpec((tm, SIDE_W), lambda i, j: (i, 0))]
        + [pl.BlockSpec((tm * SB_HEADS, SB_HEAD_DIM), lambda i, j: (i, 0)) for _ in head_major],
        out_shape=[jax.ShapeDtypeStruct((m, n), F32),
                   jax.ShapeDtypeStruct((m, SIDE_W), F32)]
        + [jax.ShapeDtypeStruct((m * SB_HEADS, SB_HEAD_DIM), F32) for _ in head_major],
        scratch_shapes=[pltpu.VMEM((tm, d), BF16)],
        compiler_params=_cparams(("parallel", "arbitrary")),
        name="norm_matmul",
    )(x, g, w, w_side)


def _out_proj_kernel(*refs, widths):
    n = len(widths)
    part_refs, w_ref, x_ref, g_ref, o_ref = refs[:n], refs[n], refs[n + 1], refs[n + 2], refs[n + 3]
    acc = None
    off = 0
    for p_ref, wd in zip(part_refs, widths):
        t = jnp.dot(p_ref[...].astype(BF16), w_ref[off:off + wd, :], preferred_element_type=F32)
        acc = t if acc is None else acc + t
        off += wd
    o_ref[...] = x_ref[...] + _rms_scale(acc) * g_ref[...]


def _out_proj(parts, w, x, g, tm):
    m, d = x.shape
    widths = tuple(p.shape[1] for p in parts)
    in_specs = [pl.BlockSpec((tm, wd), lambda i: (i, 0)) for wd in widths]
    in_specs += [pl.BlockSpec(w.shape, lambda i: (0, 0)),
                 pl.BlockSpec((tm, d), lambda i: (i, 0)),
                 pl.BlockSpec((1, d), lambda i: (0, 0))]
    return pl.pallas_call(
        functools.partial(_out_proj_kernel, widths=widths),
        grid=(m // tm,),
        in_specs=in_specs,
        out_specs=pl.BlockSpec((tm, d), lambda i: (i, 0)),
        out_shape=jax.ShapeDtypeStruct((m, d), F32),
        compiler_params=_cparams(("parallel",)),
        name="out_proj",
    )(*parts, w, x, g)


def _mlp_kernel(x_ref, g1_ref, wu_ref, wd_ref, g2_ref, o_ref, xn_ref, acc_ref):
    f = pl.program_id(1)

    @pl.when(f == 0)
    def _():
        xn_ref[...] = (_rms_scale(x_ref[...]) * g1_ref[...]).astype(BF16)
        acc_ref[...] = jnp.zeros_like(acc_ref)

    h = jnp.dot(xn_ref[...], wu_ref[...], preferred_element_type=F32)
    h = jnp.square(jnp.maximum(h, 0.0))
    acc_ref[...] += jnp.dot(h.astype(BF16), wd_ref[...], preferred_element_type=F32)

    @pl.when(f == pl.num_programs(1) - 1)
    def _():
        o_ref[...] = x_ref[...] + _rms_scale(acc_ref[...]) * g2_ref[...]


def _mlp(x, g1, wu, wd, g2, tm, tf):
    m, d = x.shape
    ff = wu.shape[1]
    return pl.pallas_call(
        _mlp_kernel,
        grid=(m // tm, ff // tf),
        in_specs=[pl.BlockSpec((tm, d), lambda i, f: (i, 0)),
                  pl.BlockSpec((1, d), lambda i, f: (0, 0)),
                  pl.BlockSpec((d, tf), lambda i, f: (0, f)),
                  pl.BlockSpec((tf, d), lambda i, f: (f, 0)),
                  pl.BlockSpec((1, d), lambda i, f: (0, 0))],
        out_specs=pl.BlockSpec((tm, d), lambda i, f: (i, 0)),
        out_shape=jax.ShapeDtypeStruct((m, d), F32),
        scratch_shapes=[pltpu.VMEM((tm, d), BF16), pltpu.VMEM((tm, d), F32)],
        compiler_params=_cparams(("parallel", "arbitrary")),
        name="mlp",
    )(x, g1, wu, wd, g2)


def _gla_kernel(q_ref, k_ref, v_ref, r_ref, glr_ref, w2_ref, bg_ref, g_ref, s0_ref,
                o_ref, s_ref, st_ref, b_ref, hi_ref, lo_ref, *, sub, unroll):
    tb = pl.program_id(1)
    t_rows = q_ref.shape[0]
    nsub = t_rows // sub
    npair = GLA_HEADS // 2

    @pl.when(tb == 0)
    def _():
        st_ref[...] = s0_ref[0]

    la = _log_sigmoid(_dot(glr_ref[...], w2_ref[...]) + bg_ref[...]) * (1.0 / GLA_GATE_TEMP)
    row = lax.broadcasted_iota(jnp.int32, (t_rows, t_rows), 0)
    col = lax.broadcasted_iota(jnp.int32, (t_rows, t_rows), 1)
    tri = jnp.where(((row // sub) == (col // sub)) & (col <= row), 1.0, 0.0).astype(BF16)
    b_ref[...] = _sel_dot(tri, la)
    la_hi = la.astype(BF16)
    hi_ref[...] = la_hi
    lo_ref[...] = (la - la_hi.astype(F32)).astype(BF16)
    tn = lambda a, b: lax.dot_general(a, b, (((0,), (0,)), ((), ())), preferred_element_type=F32)
    ones_sub = jnp.ones((sub, LANES), BF16)

    er = lax.broadcasted_iota(jnp.int32, (GLA_QK, GLA_V), 0) // GLA_DK
    ec = lax.broadcasted_iota(jnp.int32, (GLA_QK, GLA_V), 1) // GLA_DV
    expand = jnp.where(er == ec, 1.0, 0.0).astype(BF16)
    t_iota = lax.broadcasted_iota(jnp.int32, (sub, GLA_QK), 0)
    zeros_blk = jnp.zeros((GLA_DK, GLA_DV), F32)

    def sub_chunk(i):
        rows = pl.ds(pl.multiple_of(i * sub, sub), sub)
        b_i = b_ref[rows, :]
        q_i = q_ref[rows, :] * (GLA_DK ** -0.5)
        k_i = k_ref[rows, :]
        v_i = v_ref[rows, :]
        dfull = jnp.exp(tn(hi_ref[rows, :], ones_sub) + tn(lo_ref[rows, :], ones_sub))
        qe = (q_i * jnp.exp(b_i)).astype(BF16)
        kd = (k_i * jnp.exp(b_i[sub - 1:sub, :] - b_i)).astype(BF16)
        v_bf = v_i.astype(BF16)
        inter = []
        for p in range(npair):
            ha, hb = 2 * p, 2 * p + 1
            s_a, s_b = st_ref[ha], st_ref[hb]
            lanes = slice(p * LANES, (p + 1) * LANES)
            w_pair = jnp.concatenate([jnp.concatenate([s_a, zeros_blk], axis=1),
                                      jnp.concatenate([zeros_blk, s_b], axis=1)], axis=0)
            inter.append(jnp.dot(qe[:, lanes], w_pair.astype(BF16), preferred_element_type=F32))
            u = tn(kd[:, lanes], v_bf[:, ha * GLA_DV:(hb + 1) * GLA_DV])
            st_ref[ha] = s_a * dfull[ha * GLA_DK:(ha + 1) * GLA_DK, :] + u[:GLA_DK, :GLA_DV]
            st_ref[hb] = s_b * dfull[hb * GLA_DK:(hb + 1) * GLA_DK, :] + u[GLA_DK:, GLA_DV:]
        prods = []
        for j in range(sub):
            e = jnp.exp(jnp.where(t_iota >= j, b_i - b_i[j:j + 1, :], NEG_INF))
            prods.append(q_i * e * k_i[j:j + 1, :])
        p_all = jnp.concatenate(prods, axis=0)
        r_all = _dot(p_all, expand)
        o = jnp.concatenate(inter, axis=1)
        for j in range(sub):
            o = o + r_all[j * sub:(j + 1) * sub, :] * v_i[j:j + 1, :]
        r_i = r_ref[rows, :]
        outs = []
        for h in range(GLA_HEADS):
            hv = slice(h * GLA_DV, (h + 1) * GLA_DV)
            outs.append(_rms_scale(o[:, hv]) * g_ref[...] * _silu(r_i[:, hv]))
        o_ref[rows, :] = jnp.concatenate(outs, axis=1).astype(o_ref.dtype)

    def body(it, carry):
        for u in range(unroll):
            sub_chunk(it * unroll + u)
        return carry

    lax.fori_loop(0, nsub // unroll, body, 0)

    @pl.when(tb == pl.num_programs(1) - 1)
    def _():
        s_ref[0] = st_ref[...]


def _gla(proj, side, w2, bg, g, s0, bsz, length, t_rows, sub):
    nt = length // t_rows
    m = bsz * length
    row = lambda b, t: b * nt + t
    return pl.pallas_call(
        functools.partial(_gla_kernel, sub=sub, unroll=16 if (t_rows // sub) % 16 == 0 else 1),
        grid=(bsz, nt),
        in_specs=[pl.BlockSpec((t_rows, GLA_QK), lambda b, t: (row(b, t), 0)),
                  pl.BlockSpec((t_rows, GLA_QK), lambda b, t: (row(b, t), 1)),
                  pl.BlockSpec((t_rows, GLA_V), lambda b, t: (row(b, t), 1)),
                  pl.BlockSpec((t_rows, GLA_V), lambda b, t: (row(b, t), 2)),
                  pl.BlockSpec((t_rows, SIDE_W), lambda b, t: (row(b, t), 0)),
                  pl.BlockSpec((SIDE_W, GLA_QK), lambda b, t: (0, 0)),
                  pl.BlockSpec((1, GLA_QK), lambda b, t: (0, 0)),
                  pl.BlockSpec((1, GLA_DV), lambda b, t: (0, 0)),
                  pl.BlockSpec((1, GLA_HEADS, GLA_DK, GLA_DV), lambda b, t: (b, 0, 0, 0))],
        out_specs=[pl.BlockSpec((t_rows, GLA_V), lambda b, t: (row(b, t), 0)),
                   pl.BlockSpec((1, GLA_HEADS, GLA_DK, GLA_DV), lambda b, t: (b, 0, 0, 0))],
        out_shape=[jax.ShapeDtypeStruct((m, GLA_V), _act_dtype(sub)),
                   jax.ShapeDtypeStruct((bsz, GLA_HEADS, GLA_DK, GLA_DV), F32)],
        scratch_shapes=[pltpu.VMEM((GLA_HEADS, GLA_DK, GLA_DV), F32),
                        pltpu.VMEM((t_rows, GLA_QK), F32),
                        pltpu.VMEM((t_rows, GLA_QK), BF16),
                        pltpu.VMEM((t_rows, GLA_QK), BF16)],
        compiler_params=_cparams(("parallel", "arbitrary")),
        name="gla",
    )(proj, proj, proj, proj, side, w2, bg, g, s0)


LOG2_E = 1.4426950408889634
SB_LOGIT_SCALE = SB_HEAD_DIM ** -0.5 * LOG2_E
SB_T_MAX = 64.0


def _suffix_matrix(tk):
    jr = lax.broadcasted_iota(jnp.int32, (tk, tk + LANES), 0)
    jc = lax.broadcasted_iota(jnp.int32, (tk, tk + LANES), 1)
    return jnp.where((jr > jc) | (jc >= tk), 1.0, 0.0).astype(BF16)


def _sb_logits(t, mask):
    n = jnp.where(t > SB_T_MAX, t, jnp.log2(1.0 + jnp.exp2(t)))
    log_b = t - n
    if mask is not None:
        n = jnp.where(mask, n, 0.0)
    return log_b, n.astype(BF16)


def _sb_weights(log_b, n, suffix, carry, mask):
    tk = log_b.shape[1]
    sums = jnp.dot(n, suffix, preferred_element_type=F32)
    a = jnp.exp2(log_b - sums[:, :tk] - jnp.concatenate([carry] * (tk // LANES), axis=1))
    if mask is not None:
        a = jnp.where(mask, a, 0.0)
    return a.astype(BF16), carry + sums[:, tk:]


def _sb_tile(q_bf, k_blk, v_blk, bias2, suffix, carry, acc, mask):
    log_b, n = _sb_logits(_dot_nt(q_bf, k_blk) * SB_LOGIT_SCALE + bias2, mask)
    a, carry = _sb_weights(log_b, n, suffix, carry, mask)
    return carry, acc + jnp.dot(a, v_blk.astype(BF16), preferred_element_type=F32)


def _sb_prompt_kernel(bias_ref, q_ref, k_ref, v_ref, o_ref, lb_ref, n_ref, a_ref, carry_ref, acc_ref,
                      *, tk, nsplit, unroll):
    h = pl.program_id(1)
    qi = pl.program_id(2)
    tq = q_ref.shape[0]
    th = tq // nsplit
    bias2 = bias_ref[h] * LOG2_E
    suffix = _suffix_matrix(tk)
    q0 = qi * tq
    q_bf = [q_ref[s * th:(s + 1) * th, :].astype(BF16) for s in range(nsplit)]
    zero = jnp.zeros((th, LANES), F32)
    carry = [zero] * nsplit
    acc = [zero] * nsplit
    row = lax.broadcasted_iota(jnp.int32, (th, tk), 0)
    col = lax.broadcasted_iota(jnp.int32, (th, tk), 1)

    for d in reversed(range(tq // tk)):
        k0 = pl.multiple_of(q0 + d * tk, tk)
        k_blk = k_ref[pl.ds(k0, tk), :]
        v_blk = v_ref[pl.ds(k0, tk), :]
        for s in range(nsplit):
            if d * tk >= (s + 1) * th:
                continue
            mask = None if (d + 1) * tk <= s * th else (col + d * tk) < (row + s * th)
            carry[s], acc[s] = _sb_tile(q_bf[s], k_blk, v_blk, bias2, suffix, carry[s], acc[s], mask)
    for s in range(nsplit):
        carry_ref[s * th:(s + 1) * th, :] = carry[s]
        acc_ref[s * th:(s + 1) * th, :] = acc[s]

    q_all = q_ref[...].astype(BF16)
    nfull = q0 // tk

    def key_rows(f):
        return pl.ds(pl.multiple_of(jnp.maximum(nfull - 1 - f, 0) * tk, tk), tk)

    def stage_logits(f, slot):
        log_b, n = _sb_logits(_dot_nt(q_all, k_ref[key_rows(f), :]) * SB_LOGIT_SCALE + bias2, None)
        lb_ref[slot] = log_b
        n_ref[slot] = n

    def stage_weights(slot):
        a, carry = _sb_weights(lb_ref[slot], n_ref[slot], suffix, carry_ref[...], None)
        a_ref[slot] = a
        carry_ref[...] = carry

    def stage_values(f, slot):
        acc_ref[...] += jnp.dot(a_ref[slot], v_ref[key_rows(f), :].astype(BF16), preferred_element_type=F32)

    for u in range(2 * unroll):
        stage_logits(u, u)
    for u in range(unroll):
        stage_weights(u)

    def body(it, c):
        for parity in range(2):
            cur = parity * unroll
            nxt = unroll - cur
            f0 = (2 * it + parity) * unroll
            for u in range(unroll):
                stage_logits(f0 + 2 * unroll + u, cur + u)
            for u in range(unroll):
                stage_weights(nxt + u)
            for u in range(unroll):
                stage_values(f0 + u, cur + u)
        return c

    lax.fori_loop(0, nfull // (2 * unroll), body, 0)
    o_ref[...] = acc_ref[...].astype(o_ref.dtype)


def _sb_prompt(proj, bias, bsz, length, tq, tk, nsplit, unroll):
    m = bsz * length
    nq = length // tq
    qcol = (2 * GLA_QK + 2 * GLA_V) // SB_HEAD_DIM
    kcol = qcol + SB_HEADS
    vcol = kcol + SB_HEADS
    return pl.pallas_call(
        functools.partial(_sb_prompt_kernel, tk=tk, nsplit=nsplit, unroll=unroll),
        grid=(bsz, SB_HEADS, nq),
        in_specs=[pl.BlockSpec(memory_space=pltpu.SMEM),
                  pl.BlockSpec((tq, SB_HEAD_DIM), lambda b, h, i: (b * nq + i, qcol + h)),
                  pl.BlockSpec((length, SB_HEAD_DIM), lambda b, h, i: (b, kcol + h)),
                  pl.BlockSpec((length, SB_HEAD_DIM), lambda b, h, i: (b, vcol + h))],
        out_specs=pl.BlockSpec((tq, SB_HEAD_DIM), lambda b, h, i: (b * nq + i, h)),
        out_shape=jax.ShapeDtypeStruct((m, SB_W), _act_dtype(tq)),
        scratch_shapes=[pltpu.VMEM((2 * unroll, tq, tk), F32),
                        pltpu.VMEM((2 * unroll, tq, tk), BF16),
                        pltpu.VMEM((2 * unroll, tq, tk), BF16),
                        pltpu.VMEM((tq, LANES), F32),
                        pltpu.VMEM((tq, SB_HEAD_DIM), F32)],
        compiler_params=_cparams(("parallel", "parallel", "arbitrary")),
        name="sb_prompt",
    )(bias, proj, proj, proj)


def _sb_paged_kernel(pt_ref, bias_ref, q_ref, kn_ref, vn_ref, *refs, pages_per_step):
    g = pages_per_step
    k_refs, v_refs = refs[:g], refs[g:2 * g]
    o_ref, carry_ref, acc_ref, kpad_ref, vpad_ref = refs[2 * g:]
    j = pl.program_id(1)
    lq = q_ref.shape[0]
    nrow = SB_HEADS * lq
    suffix = _suffix_matrix(PAGE_SIZE)
    rhead = lax.broadcasted_iota(jnp.int32, (nrow, PAGE_SIZE), 0) // lq
    bias2 = jnp.zeros((nrow, PAGE_SIZE), F32)
    for h in range(SB_HEADS):
        bias2 = jnp.where(rhead == h, bias_ref[h] * LOG2_E, bias2)
    q_bf = [q_ref[:, h * SB_HEAD_DIM:(h + 1) * SB_HEAD_DIM].astype(BF16) for h in range(SB_HEADS)]

    def tile(k_heads, v_heads, carry, acc, mask):
        z = jnp.concatenate([_dot_nt(q_bf[h], k_heads[h]) for h in range(SB_HEADS)], axis=0)
        log_b, n = _sb_logits(z * SB_LOGIT_SCALE + bias2, mask)
        a, carry = _sb_weights(log_b, n, suffix, carry, mask)
        acc = [acc[h] + _dot(a[h * lq:(h + 1) * lq, :], v_heads[h]) for h in range(SB_HEADS)]
        return carry, acc

    @pl.when(j == 0)
    def _():
        kpad_ref[...] = jnp.zeros_like(kpad_ref)
        vpad_ref[...] = jnp.zeros_like(vpad_ref)
        for h in range(SB_HEADS):
            hs = slice(h * SB_HEAD_DIM, (h + 1) * SB_HEAD_DIM)
            kpad_ref[h, 0:lq, :] = kn_ref[:, hs]
            vpad_ref[h, 0:lq, :] = vn_ref[:, hs]
        qidx = lax.broadcasted_iota(jnp.int32, (nrow, PAGE_SIZE), 0) % lq
        kidx = lax.broadcasted_iota(jnp.int32, (nrow, PAGE_SIZE), 1)
        carry, acc = tile([kpad_ref[h] for h in range(SB_HEADS)], [vpad_ref[h] for h in range(SB_HEADS)],
                          jnp.zeros((nrow, PAGE_SIZE), F32),
                          [jnp.zeros((lq, SB_HEAD_DIM), F32)] * SB_HEADS, kidx < qidx)
        carry_ref[...] = carry
        acc_ref[...] = jnp.concatenate(acc, axis=0)

    nchunk = (PAGE_SIZE * SB_HEADS) // LANES
    ntile = g * nchunk
    q_stack = jnp.concatenate(q_bf, axis=0)
    own = (lax.broadcasted_iota(jnp.int32, (nrow, LANES), 1) % SB_HEADS) == rhead
    tiles = []
    for p in range(g):
        z = _dot_nt(q_stack, k_refs[p][0])
        tiles += [z[:, c * LANES:(c + 1) * LANES] for c in reversed(range(nchunk))]
    z = jnp.concatenate(tiles, axis=0).reshape(ntile, nrow, LANES)
    log_b, n = _sb_logits(z * SB_LOGIT_SCALE + bias2[None], own[None])
    sums = jnp.dot(n.reshape(ntile * nrow, LANES), suffix, preferred_element_type=F32)
    sums = sums.reshape(ntile, nrow, 2 * LANES)
    carry = carry_ref[...]
    carries = []
    for t in range(ntile):
        carries.append(carry)
        carry = carry + sums[t, :, LANES:]
    carry_ref[...] = carry
    a = jnp.exp2(log_b - sums[:, :, :LANES] - jnp.stack(carries, axis=0))
    a = jnp.where(own[None], a, 0.0).astype(BF16)
    acc = acc_ref[...]
    for p in range(g):
        a_page = jnp.concatenate([a[p * nchunk + (nchunk - 1 - c)] for c in range(nchunk)], axis=1)
        acc = acc + jnp.dot(a_page, v_refs[p][0].astype(BF16), preferred_element_type=F32)
    acc_ref[...] = acc

    @pl.when(j == pl.num_programs(1) - 1)
    def _():
        for h in range(SB_HEADS):
            o_ref[:, h * SB_HEAD_DIM:(h + 1) * SB_HEAD_DIM] = acc[h * lq:(h + 1) * lq, :].astype(o_ref.dtype)


def _sb_paged(proj, bias, pool_k, pool_v, page_table, bsz, lq, pages_per_step):
    n_pages = page_table.shape[1]
    g = pages_per_step
    nsteps = n_pages // g
    qcol = (2 * GLA_QK + 2 * GLA_V) // SB_W

    def page_spec(p):
        return pl.BlockSpec((1, PAGE_SIZE * SB_HEADS, SB_HEAD_DIM),
                            lambda b, j, pt: (pt[b, n_pages - 1 - (j * g + p)], 0, 0))

    grid_spec = pltpu.PrefetchScalarGridSpec(
        num_scalar_prefetch=1,
        grid=(bsz, nsteps),
        in_specs=[pl.BlockSpec(memory_space=pltpu.SMEM),
                  pl.BlockSpec((lq, SB_W), lambda b, j, pt: (b, qcol)),
                  pl.BlockSpec((lq, SB_W), lambda b, j, pt: (b, qcol + 1)),
                  pl.BlockSpec((lq, SB_W), lambda b, j, pt: (b, qcol + 2))]
        + [page_spec(p) for p in range(g)] + [page_spec(p) for p in range(g)],
        out_specs=pl.BlockSpec((lq, SB_W), lambda b, j, pt: (b, 0)),
        scratch_shapes=[pltpu.VMEM((SB_HEADS * lq, PAGE_SIZE), F32),
                        pltpu.VMEM((SB_HEADS * lq, SB_HEAD_DIM), F32),
                        pltpu.VMEM((SB_HEADS, PAGE_SIZE, SB_HEAD_DIM), F32),
                        pltpu.VMEM((SB_HEADS, PAGE_SIZE, SB_HEAD_DIM), F32)])
    return pl.pallas_call(
        functools.partial(_sb_paged_kernel, pages_per_step=g),
        grid_spec=grid_spec,
        out_shape=jax.ShapeDtypeStruct((bsz * lq, SB_W), _act_dtype(lq)),
        compiler_params=_cparams(("parallel", "arbitrary")),
        name="sb_paged",
    )(page_table, bias, proj, proj, proj, *([pool_k] * g), *([pool_v] * g))


def _ssd_chunk(r, xc_ref, z_ref, dtr_ref, dtb_ref, dsk_ref, hs_ref, ypre_ref, a, causal, causal_bf,
               lo_lanes, lo_rows):
    c = r.stop - r.start
    hpg = SSD_HEADS // SSD_GROUPS
    xc = xc_ref[r, :]
    dt = _softplus(dtr_ref[r, :] + dtb_ref[...])
    cum = _sel_dot(causal_bf, dt * a)
    transpose = (lambda t: t.T) if c % LANES == 0 else _transpose_f32
    cum_t = transpose(cum)
    dt_t = transpose(dt)
    last = cum[c - 1:c, :]
    e_cum = jnp.exp(cum)
    dec = jnp.exp(last - cum) * dt
    e_last_t = jnp.exp(cum_t[:, c - 1:c])

    for g in range(SSD_GROUPS):
        b_g = xc[:, SSD_INNER + g * SSD_STATE:SSD_INNER + (g + 1) * SSD_STATE]
        c_g = xc[:, SSD_INNER + (SSD_GROUPS + g) * SSD_STATE:SSD_INNER + (SSD_GROUPS + g + 1) * SSD_STATE]
        cb = _dot_nt(c_g, b_g)
        for pp in range(hpg // 2):
            pair = g * (hpg // 2) + pp
            h0, h1 = 2 * pair, 2 * pair + 1
            x_pair = xc[:, pair * LANES:(pair + 1) * LANES]
            hs = hs_ref[pair]
            y_pair = _dot_nt(c_g, hs) * jnp.where(lo_lanes, e_cum[:, h0:h0 + 1], e_cum[:, h1:h1 + 1])
            x_bf = x_pair.astype(BF16)
            intra = []
            for hh in (h0, h1):
                lmat = jnp.exp(jnp.where(causal, cum[:, hh:hh + 1] - cum_t[hh:hh + 1, :], NEG_INF))
                w = lmat * dt_t[hh:hh + 1, :] * cb
                intra.append(jnp.dot(w.astype(BF16), x_bf, preferred_element_type=F32))
            y_pair = y_pair + jnp.where(lo_lanes, intra[0], intra[1])
            xd = x_pair * jnp.where(lo_lanes, dec[:, h0:h0 + 1], dec[:, h1:h1 + 1])
            scale = jnp.where(lo_rows, e_last_t[h0:h0 + 1, :], e_last_t[h1:h1 + 1, :])
            upd = _dot(xd.T, b_g) if c % LANES == 0 else _dot_tn(xd, b_g)
            hs_ref[pair] = scale * hs + upd
            y_pair = y_pair + x_pair * dsk_ref[:, pair * LANES:(pair + 1) * LANES]
            zz = z_ref[r, pair * LANES:(pair + 1) * LANES]
            ypre_ref[r, pair * LANES:(pair + 1) * LANES] = y_pair * _silu(zz)


def _ssd_kernel(xbc_ref, z_ref, dtr_ref, cw_ref, cb_ref, dtb_ref, alog_ref, dsk_ref, ng_ref,
                conv0_ref, h0_ref, y_ref, hout_ref, xext_ref, hs_ref, ypre_ref, xc_ref, *, chunk):
    ci = pl.program_id(1)
    rows = xbc_ref.shape[0]
    c = chunk
    tail = SSD_CONV - 1
    pad = 8

    @pl.when(ci == 0)
    def _():
        xext_ref[0:pad, :] = conv0_ref[0]
        hs_ref[...] = h0_ref[0]

    xext_ref[pad:pad + rows, :] = xbc_ref[...]
    xfull = xext_ref[...]
    acc = cb_ref[...] + xfull[pad:, :] * cw_ref[tail:tail + 1, :]
    for w in range(tail):
        acc = acc + pltpu.roll(xfull, tail - w, axis=0)[pad:, :] * cw_ref[w:w + 1, :]
    xc_ref[...] = _silu(acc)
    xext_ref[0:pad, :] = xext_ref[rows:rows + pad, :]

    a = -jnp.exp(alog_ref[...])
    row = lax.broadcasted_iota(jnp.int32, (c, c), 0)
    col = lax.broadcasted_iota(jnp.int32, (c, c), 1)
    causal = col <= row
    causal_bf = jnp.where(causal, 1.0, 0.0).astype(BF16)
    lo_lanes = lax.broadcasted_iota(jnp.int32, (c, LANES), 1) < SSD_HEAD_DIM
    lo_rows = lax.broadcasted_iota(jnp.int32, (LANES, LANES), 0) < SSD_HEAD_DIM
    for sc in range(rows // c):
        _ssd_chunk(slice(sc * c, (sc + 1) * c), xc_ref, z_ref, dtr_ref, dtb_ref, dsk_ref, hs_ref, ypre_ref,
                   a, causal, causal_bf, lo_lanes, lo_rows)
    gw = SSD_INNER // SSD_GROUPS
    for g in range(SSD_GROUPS):
        gs = slice(g * gw, (g + 1) * gw)
        y_ref[:, gs] = (_rms_scale(ypre_ref[:, gs]) * ng_ref[:, gs]).astype(y_ref.dtype)

    @pl.when(ci == pl.num_programs(1) - 1)
    def _():
        hout_ref[0] = hs_ref[...]


def _ssd(zx, side, cw, cb, dtb, alog, dsk, ng, conv0, h0, bsz, length, chunk, rows):
    nc = length // rows
    m = bsz * length
    row = lambda b, c: b * nc + c
    const = lambda b, c: (0, 0)
    npair = SSD_HEADS // 2
    return pl.pallas_call(
        functools.partial(_ssd_kernel, chunk=chunk),
        grid=(bsz, nc),
        in_specs=[pl.BlockSpec((rows, SSD_CONV_DIM), lambda b, c: (row(b, c), 0)),
                  pl.BlockSpec((rows, SSD_INNER), lambda b, c: (row(b, c), SSD_CONV_DIM // SSD_INNER)),
                  pl.BlockSpec((rows, SIDE_W), lambda b, c: (row(b, c), 0)),
                  pl.BlockSpec((SSD_CONV, SSD_CONV_DIM), const),
                  pl.BlockSpec((1, SSD_CONV_DIM), const),
                  pl.BlockSpec((1, SIDE_W), const),
                  pl.BlockSpec((1, SIDE_W), const),
                  pl.BlockSpec((1, SSD_INNER), const),
                  pl.BlockSpec((1, SSD_INNER), const),
                  pl.BlockSpec((1, 8, SSD_CONV_DIM), lambda b, c: (b, 0, 0)),
                  pl.BlockSpec((1, npair, LANES, SSD_STATE), lambda b, c: (b, 0, 0, 0))],
        out_specs=[pl.BlockSpec((rows, SSD_INNER), lambda b, c: (row(b, c), 0)),
                   pl.BlockSpec((1, npair, LANES, SSD_STATE), lambda b, c: (b, 0, 0, 0))],
        out_shape=[jax.ShapeDtypeStruct((m, SSD_INNER), _act_dtype(chunk)),
                   jax.ShapeDtypeStruct((bsz, npair, LANES, SSD_STATE), F32)],
        scratch_shapes=[pltpu.VMEM((rows + 8, SSD_CONV_DIM), F32),
                        pltpu.VMEM((npair, LANES, SSD_STATE), F32),
                        pltpu.VMEM((rows, SSD_INNER), F32),
                        pltpu.VMEM((rows, SSD_CONV_DIM), F32)],
        compiler_params=_cparams(("parallel", "arbitrary")),
        name="ssd",
    )(zx, zx, side, cw, cb, dtb, alog, dsk, ng, conv0, h0)


def _pad_cols(w, width):
    return jnp.pad(w, ((0, 0), (0, width - w.shape[1])))


def _row(v, width=None):
    v = v.reshape(1, -1).astype(F32)
    return v if width is None else _pad_cols(v, width)


def _prep_params(p):
    w_in = p["w_in_even"][0]
    n_gla = 2 * GLA_QK + GLA_V
    q = {}
    q["even_main"] = jnp.concatenate([w_in[:, :n_gla], w_in[:, n_gla + GLA_GATE_RANK:]], axis=1).astype(BF16)
    q["even_side"] = _pad_cols(w_in[:, n_gla:n_gla + GLA_GATE_RANK], SIDE_W).astype(BF16)
    q["gate_w2"] = jnp.pad(p["gla_w_gate2"][0], ((0, SIDE_W - GLA_GATE_RANK), (0, 0))).astype(BF16)
    q["gate_b"] = _row(p["gla_b_gate"][0])
    q["gla_g"] = _row(p["gla_norm_g"][0])
    q["sb_bias"] = p["sb_logit_bias"][0].astype(F32)
    q["even_out"] = p["w_out_even"][0].astype(BF16)
    w_ssd = p["ssd_w_in"][0]
    q["ssd_main"] = jnp.concatenate([w_ssd[:, SSD_INNER:SSD_INNER + SSD_CONV_DIM], w_ssd[:, :SSD_INNER]],
                                    axis=1).astype(BF16)
    q["ssd_side"] = _pad_cols(w_ssd[:, SSD_INNER + SSD_CONV_DIM:], SIDE_W).astype(BF16)
    q["conv_w"] = p["ssd_conv_w"][0].astype(F32)
    q["conv_b"] = _row(p["ssd_conv_b"][0])
    q["dt_bias"] = _row(p["ssd_dt_bias"][0], SIDE_W)
    q["a_log"] = _row(p["ssd_a_log"][0], SIDE_W)
    q["d_skip"] = _row(jnp.repeat(p["ssd_d"][0], SSD_HEAD_DIM))
    q["ssd_g"] = _row(p["ssd_norm_g"][0])
    q["ssd_out"] = p["ssd_w_out"][0].astype(BF16)
    for name in ("norm_mix_pre", "norm_mix_post", "norm_mlp_pre", "norm_mlp_post"):
        q[name] = [_row(p[name][li]) for li in range(2)]
    q["mlp_up"] = [p["mlp_w_up"][li].astype(BF16) for li in range(2)]
    q["mlp_down"] = [p["mlp_w_down"][li].astype(BF16) for li in range(2)]
    return q


def _trunk(x, paged, gla_s0, ssm_h0, conv0, q, cfg):
    bsz, length, d = x.shape
    m = bsz * length
    x2 = x.reshape(m, d)

    kcol = 2 * GLA_QK + 2 * GLA_V + SB_W
    proj, side, sb_k, sb_v = _norm_matmul(x2, q["norm_mix_pre"][0], q["even_main"], q["even_side"],
                                          cfg["tm_mm"], cfg["tn_even"], head_major=(kcol, kcol + SB_W))
    o_gla, gla_new = _gla(proj, side, q["gate_w2"], q["gate_b"], q["gla_g"], gla_s0,
                          bsz, length, cfg["gla_rows"], cfg["gla_sub"])
    if paged is None:
        o_sb = _sb_prompt(proj, q["sb_bias"], bsz, length, cfg["sb_tq"], cfg["sb_tk"], cfg["sb_split"],
                          cfg["sb_unroll"])
    else:
        o_sb = _sb_paged(proj, q["sb_bias"], paged[0], paged[1], paged[2], bsz, length, cfg["sb_pages"])
    sb_k = sb_k.reshape(bsz, length, SB_HEADS, SB_HEAD_DIM)
    sb_v = sb_v.reshape(bsz, length, SB_HEADS, SB_HEAD_DIM)
    x2 = _out_proj([o_gla, o_sb], q["even_out"], x2, q["norm_mix_post"][0], cfg["tm_mm"])
    x2 = _mlp(x2, q["norm_mlp_pre"][0], q["mlp_up"][0], q["mlp_down"][0], q["norm_mlp_post"][0],
              cfg["tm_mm"], 1024)

    zx, dtr = _norm_matmul(x2, q["norm_mix_pre"][1], q["ssd_main"], q["ssd_side"], cfg["tm_mm"], cfg["tn_ssd"])
    conv_pad = jnp.pad(conv0, ((0, 0), (8 - (SSD_CONV - 1), 0), (0, 0)))
    h0 = ssm_h0.reshape(bsz, SSD_HEADS // 2, 2 * SSD_HEAD_DIM, SSD_STATE)
    y, h_new = _ssd(zx, dtr, q["conv_w"], q["conv_b"], q["dt_bias"], q["a_log"], q["d_skip"], q["ssd_g"],
                    conv_pad, h0, bsz, length, cfg["ssd_chunk"], cfg["ssd_rows"])
    conv_new = zx.reshape(bsz, length, SSD_MAIN)[:, length - (SSD_CONV - 1):, :SSD_CONV_DIM]
    x2 = _out_proj([y], q["ssd_out"], x2, q["norm_mix_post"][1], cfg["tm_mm"])
    x2 = _mlp(x2, q["norm_mlp_pre"][1], q["mlp_up"][1], q["mlp_down"][1], q["norm_mlp_post"][1],
              cfg["tm_mm"], 1024)

    return (x2.reshape(bsz, length, d), sb_k[None], sb_v[None], gla_new[None],
            h_new.reshape(bsz, SSD_HEADS, SSD_HEAD_DIM, SSD_STATE)[None], conv_new[None])


def _prompt_cfg(length):
    return dict(tn_even=EVEN_MAIN // 2, tn_ssd=SSD_MAIN // 3, tm_mm=min(1024, 2 * length),
                gla_rows=min(256, length), gla_sub=16,
                sb_tq=min(512, length), sb_tk=128, sb_split=2, sb_unroll=2,
                ssd_chunk=min(128, length), ssd_rows=min(256, length))


def _sample_cfg(bsz, length, n_pages):
    return dict(tn_even=EVEN_MAIN // 2, tn_ssd=SSD_MAIN // 3, tm_mm=bsz * length, gla_rows=length, gla_sub=length, sb_pages=min(32, n_pages),
                ssd_chunk=length, ssd_rows=length)


def kernel(x_prompt, x_sample, cache_sb_k, cache_sb_v, state_gla, state_ssm, state_conv, page_table, w_in_even, gla_w_gate2, gla_b_gate, gla_norm_g, sb_logit_bias, w_out_even, ssd_w_in, ssd_conv_w, ssd_conv_b, ssd_dt_bias, ssd_a_log, ssd_d, ssd_norm_g, ssd_w_out, norm_mix_pre, norm_mix_post, norm_mlp_pre, norm_mlp_post, mlp_w_up, mlp_w_down):
    q = _prep_params(dict(
        w_in_even=w_in_even, gla_w_gate2=gla_w_gate2, gla_b_gate=gla_b_gate, gla_norm_g=gla_norm_g,
        sb_logit_bias=sb_logit_bias, w_out_even=w_out_even, ssd_w_in=ssd_w_in, ssd_conv_w=ssd_conv_w,
        ssd_conv_b=ssd_conv_b, ssd_dt_bias=ssd_dt_bias, ssd_a_log=ssd_a_log, ssd_d=ssd_d,
        ssd_norm_g=ssd_norm_g, ssd_w_out=ssd_w_out, norm_mix_pre=norm_mix_pre, norm_mix_post=norm_mix_post,
        norm_mlp_pre=norm_mlp_pre, norm_mlp_post=norm_mlp_post, mlp_w_up=mlp_w_up, mlp_w_down=mlp_w_down))
    bp, lp, _ = x_prompt.shape
    bs, ls, _ = x_sample.shape
    zeros = lambda *s: jnp.zeros(s, F32)
    prompt = _trunk(x_prompt, None, zeros(bp, GLA_HEADS, GLA_DK, GLA_DV),
                    zeros(bp, SSD_HEADS, SSD_HEAD_DIM, SSD_STATE), zeros(bp, SSD_CONV - 1, SSD_CONV_DIM),
                    q, _prompt_cfg(lp))
    n_pool = cache_sb_k.shape[1]
    pool_k = cache_sb_k.reshape(n_pool, PAGE_SIZE * SB_HEADS, SB_HEAD_DIM)
    pool_v = cache_sb_v.reshape(n_pool, PAGE_SIZE * SB_HEADS, SB_HEAD_DIM)
    sample = _trunk(x_sample, (pool_k, pool_v, page_table), state_gla[0], state_ssm[0], state_conv[0],
                    q, _sample_cfg(bs, ls, page_table.shape[1]))
    return (prompt[0], sample[0]) + prompt[1:] + sample[1:]
```

```python
import functools

import jax
import jax.numpy as jnp
from jax import lax
from jax.experimental import pallas as pl
from jax.experimental.pallas import tpu as pltpu

F32 = jnp.float32
BF16 = jnp.bfloat16

D_MODEL = 1024
RMS_EPS = 1e-6
D_FF = 4 * D_MODEL
GLA_HEADS = 4
GLA_DK = 64
GLA_DV = 128
GLA_GATE_RANK = 16
GLA_GATE_TEMP = 16.0
GLA_QK = GLA_HEADS * GLA_DK
GLA_V = GLA_HEADS * GLA_DV
SB_HEADS = 4
SB_HEAD_DIM = 128
SB_W = SB_HEADS * SB_HEAD_DIM
PAGE_SIZE = 128
EVEN_MAIN = 2 * GLA_QK + 2 * GLA_V + 3 * SB_W
SSD_INNER = 2048
SSD_HEAD_DIM = 64
SSD_HEADS = 32
SSD_GROUPS = 8
SSD_STATE = 128
SSD_CONV = 4
SSD_CONV_DIM = SSD_INNER + 2 * SSD_GROUPS * SSD_STATE
SSD_MAIN = SSD_CONV_DIM + SSD_INNER
LANES = 128
SIDE_W = LANES
VMEM_LIMIT = 48 * 1024 * 1024

NEG_INF = float("-inf")


def _act_dtype(rows):
    return BF16 if rows % 16 == 0 else F32


def _cparams(sem, fuse_inputs=None):
    return pltpu.CompilerParams(dimension_semantics=sem, vmem_limit_bytes=VMEM_LIMIT,
                                allow_input_fusion=fuse_inputs)


def _dot(a, b):
    return jnp.dot(a.astype(BF16), b.astype(BF16), preferred_element_type=F32)


def _dot_nt(a, b):
    return lax.dot_general(a.astype(BF16), b.astype(BF16), (((1,), (1,)), ((), ())),
                           preferred_element_type=F32)


def _dot_tn(a, b):
    return lax.dot_general(a.astype(BF16), b.astype(BF16), (((0,), (0,)), ((), ())),
                           preferred_element_type=F32)


def _split3(x):
    hi = x.astype(BF16)
    r1 = x - hi.astype(F32)
    mid = r1.astype(BF16)
    lo = (r1 - mid.astype(F32)).astype(BF16)
    return hi, mid, lo


def _sel_dot(sel, x):
    hi, mid, lo = _split3(x)
    d = functools.partial(jnp.dot, preferred_element_type=F32)
    return d(sel, hi) + d(sel, mid) + d(sel, lo)


def _dot_sel(x, sel):
    hi, mid, lo = _split3(x)
    d = functools.partial(jnp.dot, preferred_element_type=F32)
    return d(hi, sel) + d(mid, sel) + d(lo, sel)


def _transpose_f32(x):
    n = x.shape[1]
    eye = (lax.broadcasted_iota(jnp.int32, (n, n), 0)
           == lax.broadcasted_iota(jnp.int32, (n, n), 1)).astype(BF16)
    hi, mid, lo = _split3(x)
    d = lambda p: lax.dot_general(eye, p, (((1,), (1,)), ((), ())), preferred_element_type=F32)
    return d(hi) + d(mid) + d(lo)


def _softplus_neg_abs(x):
    e = jnp.exp(-jnp.abs(x))
    u = 1.0 + e
    return jnp.where(u == 1.0, e, jnp.log(u) * (e / (u - 1.0)))


def _log_sigmoid(x):
    return jnp.minimum(x, 0.0) - _softplus_neg_abs(x)


def _softplus(x):
    return jnp.maximum(x, 0.0) + _softplus_neg_abs(x)


def _silu(x):
    h = 0.5 * x
    return h + h * jnp.tanh(h)


def _rms_scale(x):
    return x * lax.rsqrt(jnp.mean(x * x, axis=-1, keepdims=True) + RMS_EPS)


def _norm_matmul_kernel(x_ref, g_ref, w_ref, ws_ref, o_ref, os_ref, *rest, head_major):
    hm_refs, xn_ref = rest[:-1], rest[-1]
    j = pl.program_id(1)
    tm, tn = o_ref.shape

    @pl.when(j == 0)
    def _():
        xn_ref[...] = (_rms_scale(x_ref[...]) * g_ref[...]).astype(BF16)
        os_ref[...] = jnp.dot(xn_ref[...], ws_ref[...], preferred_element_type=F32)

    cols = pl.ds(pl.multiple_of(j * tn, tn), tn)
    res = jnp.dot(xn_ref[...], w_ref[:, cols], preferred_element_type=F32)
    o_ref[...] = res
    for ref, start in zip(hm_refs, head_major):
        tile, off = divmod(start, tn)

        @pl.when(j == tile)
        def _():
            for h in range(SB_HEADS):
                ref[pl.ds(h, tm, stride=SB_HEADS), :] = res[:, off + h * SB_HEAD_DIM:off + (h + 1) * SB_HEAD_DIM]


def _norm_matmul(x, g, w, w_side, tm, tn, head_major=()):
    m, d = x.shape
    n = w.shape[1]
    once = pl.Buffered(1)
    return pl.pallas_call(
        functools.partial(_norm_matmul_kernel, head_major=tuple(head_major)),
        grid=(m // tm, n // tn),
        in_specs=[pl.BlockSpec((tm, d), lambda i, j: (i, 0)),
                  pl.BlockSpec((1, d), lambda i, j: (0, 0)),
                  pl.BlockSpec((d, n), lambda i, j: (0, 0), pipeline_mode=once),
                  pl.BlockSpec((d, SIDE_W), lambda i, j: (0, 0), pipeline_mode=once)],
        out_specs=[pl.BlockSpec((tm, tn), lambda i, j: (i, j)),
                   pl.BlockSpec((tm, SIDE_W), lambda i, j: (i, 0))]
        + [pl.BlockSpec((tm * SB_HEADS, SB_HEAD_DIM), lambda i, j: (i, 0)) for _ in head_major],
        out_shape=[jax.ShapeDtypeStruct((m, n), F32),
                   jax.ShapeDtypeStruct((m, SIDE_W), F32)]
        + [jax.ShapeDtypeStruct((m * SB_HEADS, SB_HEAD_DIM), F32) for _ in head_major],
        scratch_shapes=[pltpu.VMEM((tm, d), BF16)],
        compiler_params=_cparams(("parallel", "arbitrary")),
        name="norm_matmul",
    )(x, g, w, w_side)


def _out_proj_kernel(*refs, widths):
    n = len(widths)
    part_refs, w_ref, x_ref, g_ref, o_ref = refs[:n], refs[n], refs[n + 1], refs[n + 2], refs[n + 3]
    acc = None
    off = 0
    for p_ref, wd in zip(part_refs, widths):
        t = jnp.dot(p_ref[...].astype(BF16), w_ref[off:off + wd, :], preferred_element_type=F32)
        acc = t if acc is None else acc + t
        off += wd
    o_ref[...] = x_ref[...] + _rms_scale(acc) * g_ref[...]


def _out_proj(parts, w, x, g, tm):
    m, d = x.shape
    widths = tuple(p.shape[1] for p in parts)
    in_specs = [pl.BlockSpec((tm, wd), lambda i: (i, 0)) for wd in widths]
    in_specs += [pl.BlockSpec(w.shape, lambda i: (0, 0)),
                 pl.BlockSpec((tm, d), lambda i: (i, 0)),
                 pl.BlockSpec((1, d), lambda i: (0, 0))]
    return pl.pallas_call(
        functools.partial(_out_proj_kernel, widths=widths),
        grid=(m // tm,),
        in_specs=in_specs,
        out_specs=pl.BlockSpec((tm, d), lambda i: (i, 0)),
        out_shape=jax.ShapeDtypeStruct((m, d), F32),
        compiler_params=_cparams(("parallel",)),
        name="out_proj",
    )(*parts, w, x, g)


def _mlp_kernel(x_ref, g1_ref, wu_ref, wd_ref, g2_ref, o_ref, xn_ref, acc_ref):
    f = pl.program_id(1)

    @pl.when(f == 0)
    def _():
        xn_ref[...] = (_rms_scale(x_ref[...]) * g1_ref[...]).astype(BF16)
        acc_ref[...] = jnp.zeros_like(acc_ref)

    h = jnp.dot(xn_ref[...], wu_ref[...], preferred_element_type=F32)
    h = jnp.square(jnp.maximum(h, 0.0))
    acc_ref[...] += jnp.dot(h.astype(BF16), wd_ref[...], preferred_element_type=F32)

    @pl.when(f == pl.num_programs(1) - 1)
    def _():
        o_ref[...] = x_ref[...] + _rms_scale(acc_ref[...]) * g2_ref[...]


def _mlp(x, g1, wu, wd, g2, tm, tf):
    m, d = x.shape
    ff = wu.shape[1]
    return pl.pallas_call(
        _mlp_kernel,
        grid=(m // tm, ff // tf),
        in_specs=[pl.BlockSpec((tm, d), lambda i, f: (i, 0)),
                  pl.BlockSpec((1, d), lambda i, f: (0, 0)),
                  pl.BlockSpec((d, tf), lambda i, f: (0, f)),
                  pl.BlockSpec((tf, d), lambda i, f: (f, 0)),
                  pl.BlockSpec((1, d), lambda i, f: (0, 0))],
        out_specs=pl.BlockSpec((tm, d), lambda i, f: (i, 0)),
        out_shape=jax.ShapeDtypeStruct((m, d), F32),
        scratch_shapes=[pltpu.VMEM((tm, d), BF16), pltpu.VMEM((tm, d), F32)],
        compiler_params=_cparams(("parallel", "arbitrary"), [False, False, True, True, False]),
        name="mlp",
    )(x, g1, wu, wd, g2)


def _gla_kernel(q_ref, k_ref, v_ref, r_ref, glr_ref, w2_ref, bg_ref, g_ref, s0_ref,
                o_ref, s_ref, st_ref, b_ref, hi_ref, lo_ref, *, sub, unroll):
    tb = pl.program_id(1)
    t_rows = q_ref.shape[0]
    nsub = t_rows // sub
    npair = GLA_HEADS // 2

    @pl.when(tb == 0)
    def _():
        st_ref[...] = s0_ref[0]

    la = _log_sigmoid(_dot(glr_ref[...], w2_ref[...]) + bg_ref[...]) * (1.0 / GLA_GATE_TEMP)
    row = lax.broadcasted_iota(jnp.int32, (t_rows, t_rows), 0)
    col = lax.broadcasted_iota(jnp.int32, (t_rows, t_rows), 1)
    tri = jnp.where(((row // sub) == (col // sub)) & (col <= row), 1.0, 0.0).astype(BF16)
    b_ref[...] = _sel_dot(tri, la)
    la_hi = la.astype(BF16)
    hi_ref[...] = la_hi
    lo_ref[...] = (la - la_hi.astype(F32)).astype(BF16)
    tn = lambda a, b: lax.dot_general(a, b, (((0,), (0,)), ((), ())), preferred_element_type=F32)
    ones_sub = jnp.ones((sub, LANES), BF16)

    er = lax.broadcasted_iota(jnp.int32, (GLA_QK, GLA_V), 0) // GLA_DK
    ec = lax.broadcasted_iota(jnp.int32, (GLA_QK, GLA_V), 1) // GLA_DV
    expand = jnp.where(er == ec, 1.0, 0.0).astype(BF16)
    t_iota = lax.broadcasted_iota(jnp.int32, (sub, GLA_QK), 0)
    zeros_blk = jnp.zeros((GLA_DK, GLA_DV), F32)

    def sub_chunk(i):
        rows = pl.ds(pl.multiple_of(i * sub, sub), sub)
        b_i = b_ref[rows, :]
        q_i = q_ref[rows, :] * (GLA_DK ** -0.5)
        k_i = k_ref[rows, :]
        v_i = v_ref[rows, :]
        dfull = jnp.exp(tn(hi_ref[rows, :], ones_sub) + tn(lo_ref[rows, :], ones_sub))
        qe = (q_i * jnp.exp(b_i)).astype(BF16)
        kd = (k_i * jnp.exp(b_i[sub - 1:sub, :] - b_i)).astype(BF16)
        v_bf = v_i.astype(BF16)
        inter = []
        for p in range(npair):
            ha, hb = 2 * p, 2 * p + 1
            s_a, s_b = st_ref[ha], st_ref[hb]
            lanes = slice(p * LANES, (p + 1) * LANES)
            w_pair = jnp.concatenate([jnp.concatenate([s_a, zeros_blk], axis=1),
                                      jnp.concatenate([zeros_blk, s_b], axis=1)], axis=0)
            inter.append(jnp.dot(qe[:, lanes], w_pair.astype(BF16), preferred_element_type=F32))
            u = tn(kd[:, lanes], v_bf[:, ha * GLA_DV:(hb + 1) * GLA_DV])
            st_ref[ha] = s_a * dfull[ha * GLA_DK:(ha + 1) * GLA_DK, :] + u[:GLA_DK, :GLA_DV]
            st_ref[hb] = s_b * dfull[hb * GLA_DK:(hb + 1) * GLA_DK, :] + u[GLA_DK:, GLA_DV:]
        prods = []
        for j in range(sub):
            e = jnp.exp(jnp.where(t_iota >= j, b_i - b_i[j:j + 1, :], NEG_INF))
            prods.append(q_i * e * k_i[j:j + 1, :])
        p_all = jnp.concatenate(prods, axis=0)
        r_all = _dot(p_all, expand)
        o = jnp.concatenate(inter, axis=1)
        for j in range(sub):
            o = o + r_all[j * sub:(j + 1) * sub, :] * v_i[j:j + 1, :]
        r_i = r_ref[rows, :]
        outs = []
        for h in range(GLA_HEADS):
            hv = slice(h * GLA_DV, (h + 1) * GLA_DV)
            outs.append(_rms_scale(o[:, hv]) * g_ref[...] * _silu(r_i[:, hv]))
        o_ref[rows, :] = jnp.concatenate(outs, axis=1).astype(o_ref.dtype)

    def body(it, carry):
        for u in range(unroll):
            sub_chunk(it * unroll + u)
        return carry

    lax.fori_loop(0, nsub // unroll, body, 0)

    @pl.when(tb == pl.num_programs(1) - 1)
    def _():
        s_ref[0] = st_ref[...]


def _gla(proj, side, w2, bg, g, s0, bsz, length, t_rows, sub):
    nt = length // t_rows
    m = bsz * length
    row = lambda b, t: b * nt + t
    return pl.pallas_call(
        functools.partial(_gla_kernel, sub=sub, unroll=16 if (t_rows // sub) % 16 == 0 else 1),
        grid=(bsz, nt),
        in_specs=[pl.BlockSpec((t_rows, GLA_QK), lambda b, t: (row(b, t), 0)),
                  pl.BlockSpec((t_rows, GLA_QK), lambda b, t: (row(b, t), 1)),
                  pl.BlockSpec((t_rows, GLA_V), lambda b, t: (row(b, t), 1)),
                  pl.BlockSpec((t_rows, GLA_V), lambda b, t: (row(b, t), 2)),
                  pl.BlockSpec((t_rows, SIDE_W), lambda b, t: (row(b, t), 0)),
                  pl.BlockSpec((SIDE_W, GLA_QK), lambda b, t: (0, 0)),
                  pl.BlockSpec((1, GLA_QK), lambda b, t: (0, 0)),
                  pl.BlockSpec((1, GLA_DV), lambda b, t: (0, 0)),
                  pl.BlockSpec((1, GLA_HEADS, GLA_DK, GLA_DV), lambda b, t: (b, 0, 0, 0))],
        out_specs=[pl.BlockSpec((t_rows, GLA_V), lambda b, t: (row(b, t), 0)),
                   pl.BlockSpec((1, GLA_HEADS, GLA_DK, GLA_DV), lambda b, t: (b, 0, 0, 0))],
        out_shape=[jax.ShapeDtypeStruct((m, GLA_V), _act_dtype(sub)),
                   jax.ShapeDtypeStruct((bsz, GLA_HEADS, GLA_DK, GLA_DV), F32)],
        scratch_shapes=[pltpu.VMEM((GLA_HEADS, GLA_DK, GLA_DV), F32),
                        pltpu.VMEM((t_rows, GLA_QK), F32),
                        pltpu.VMEM((t_rows, GLA_QK), BF16),
                        pltpu.VMEM((t_rows, GLA_QK), BF16)],
        compiler_params=_cparams(("parallel", "arbitrary")),
        name="gla",
    )(proj, proj, proj, proj, side, w2, bg, g, s0)


LOG2_E = 1.4426950408889634
SB_LOGIT_SCALE = SB_HEAD_DIM ** -0.5 * LOG2_E
SB_T_MAX = 64.0


def _suffix_matrix(tk):
    jr = lax.broadcasted_iota(jnp.int32, (tk, tk + LANES), 0)
    jc = lax.broadcasted_iota(jnp.int32, (tk, tk + LANES), 1)
    return jnp.where((jr > jc) | (jc >= tk), 1.0, 0.0).astype(BF16)


def _sb_logits(t, mask):
    n = jnp.where(t > SB_T_MAX, t, jnp.log2(1.0 + jnp.exp2(t)))
    log_b = t - n
    if mask is not None:
        n = jnp.where(mask, n, 0.0)
    return log_b, n.astype(BF16)


def _sb_weights(log_b, n, suffix, carry, mask):
    tk = log_b.shape[1]
    sums = jnp.dot(n, suffix, preferred_element_type=F32)
    a = jnp.exp2(log_b - sums[:, :tk] - jnp.concatenate([carry] * (tk // LANES), axis=1))
    if mask is not None:
        a = jnp.where(mask, a, 0.0)
    return a.astype(BF16), carry + sums[:, tk:]


def _sb_tile(q_bf, k_blk, v_blk, bias2, suffix, carry, acc, mask):
    log_b, n = _sb_logits(_dot_nt(q_bf, k_blk) * SB_LOGIT_SCALE + bias2, mask)
    a, carry = _sb_weights(log_b, n, suffix, carry, mask)
    return carry, acc + jnp.dot(a, v_blk.astype(BF16), preferred_element_type=F32)


def _sb_prompt_kernel(bias_ref, q_ref, k_ref, v_ref, o_ref, lb_ref, n_ref, a_ref, carry_ref, acc_ref,
                      *, tk, nsplit, unroll):
    h = pl.program_id(1)
    qi = pl.program_id(2)
    tq = q_ref.shape[0]
    th = tq // nsplit
    bias2 = bias_ref[h] * LOG2_E
    suffix = _suffix_matrix(tk)
    q0 = qi * tq
    q_bf = [q_ref[s * th:(s + 1) * th, :].astype(BF16) for s in range(nsplit)]
    zero = jnp.zeros((th, LANES), F32)
    carry = [zero] * nsplit
    acc = [zero] * nsplit
    row = lax.broadcasted_iota(jnp.int32, (th, tk), 0)
    col = lax.broadcasted_iota(jnp.int32, (th, tk), 1)

    for d in reversed(range(tq // tk)):
        k0 = pl.multiple_of(q0 + d * tk, tk)
        k_blk = k_ref[pl.ds(k0, tk), :]
        v_blk = v_ref[pl.ds(k0, tk), :]
        for s in range(nsplit):
            if d * tk >= (s + 1) * th:
                continue
            mask = None if (d + 1) * tk <= s * th else (col + d * tk) < (row + s * th)
            carry[s], acc[s] = _sb_tile(q_bf[s], k_blk, v_blk, bias2, suffix, carry[s], acc[s], mask)
    for s in range(nsplit):
        carry_ref[s * th:(s + 1) * th, :] = carry[s]
        acc_ref[s * th:(s + 1) * th, :] = acc[s]

    q_all = q_ref[...].astype(BF16)
    nfull = q0 // tk

    def key_rows(f):
        return pl.ds(pl.multiple_of(jnp.maximum(nfull - 1 - f, 0) * tk, tk), tk)

    def stage_logits(f, slot):
        log_b, n = _sb_logits(_dot_nt(q_all, k_ref[key_rows(f), :]) * SB_LOGIT_SCALE + bias2, None)
        lb_ref[slot] = log_b
        n_ref[slot] = n

    def stage_weights(slot):
        a, carry = _sb_weights(lb_ref[slot], n_ref[slot], suffix, carry_ref[...], None)
        a_ref[slot] = a
        carry_ref[...] = carry

    def stage_values(f, slot):
        acc_ref[...] += jnp.dot(a_ref[slot], v_ref[key_rows(f), :].astype(BF16), preferred_element_type=F32)

    for u in range(2 * unroll):
        stage_logits(u, u)
    for u in range(unroll):
        stage_weights(u)

    def body(it, c):
        for parity in range(2):
            cur = parity * unroll
            nxt = unroll - cur
            f0 = (2 * it + parity) * unroll
            for u in range(unroll):
                stage_logits(f0 + 2 * unroll + u, cur + u)
            for u in range(unroll):
                stage_weights(nxt + u)
            for u in range(unroll):
                stage_values(f0 + u, cur + u)
        return c

    lax.fori_loop(0, nfull // (2 * unroll), body, 0)
    o_ref[...] = acc_ref[...].astype(o_ref.dtype)


def _sb_prompt(proj, bias, bsz, length, tq, tk, nsplit, unroll):
    m = bsz * length
    nq = length // tq
    qcol = (2 * GLA_QK + 2 * GLA_V) // SB_HEAD_DIM
    kcol = qcol + SB_HEADS
    vcol = kcol + SB_HEADS
    return pl.pallas_call(
        functools.partial(_sb_prompt_kernel, tk=tk, nsplit=nsplit, unroll=unroll),
        grid=(bsz, SB_HEADS, nq),
        in_specs=[pl.BlockSpec(memory_space=pltpu.SMEM),
                  pl.BlockSpec((tq, SB_HEAD_DIM), lambda b, h, i: (b * nq + i, qcol + h)),
                  pl.BlockSpec((length, SB_HEAD_DIM), lambda b, h, i: (b, kcol + h)),
                  pl.BlockSpec((length, SB_HEAD_DIM), lambda b, h, i: (b, vcol + h))],
        out_specs=pl.BlockSpec((tq, SB_HEAD_DIM), lambda b, h, i: (b * nq + i, h)),
        out_shape=jax.ShapeDtypeStruct((m, SB_W), _act_dtype(tq)),
        scratch_shapes=[pltpu.VMEM((2 * unroll, tq, tk), F32),
                        pltpu.VMEM((2 * unroll, tq, tk), BF16),
                        pltpu.VMEM((2 * unroll, tq, tk), BF16),
                        pltpu.VMEM((tq, LANES), F32),
                        pltpu.VMEM((tq, SB_HEAD_DIM), F32)],
        compiler_params=_cparams(("parallel", "parallel", "arbitrary")),
        name="sb_prompt",
    )(bias, proj, proj, proj)


def _sb_paged_kernel(pt_ref, bias_ref, q_ref, kn_ref, vn_ref, *refs, pages_per_step):
    g = pages_per_step
    k_refs, v_refs = refs[:g], refs[g:2 * g]
    o_ref, carry_ref, acc_ref, kpad_ref, vpad_ref = refs[2 * g:]
    j = pl.program_id(1)
    lq = q_ref.shape[0]
    nrow = SB_HEADS * lq
    suffix = _suffix_matrix(PAGE_SIZE)
    rhead = lax.broadcasted_iota(jnp.int32, (nrow, PAGE_SIZE), 0) // lq
    bias2 = jnp.zeros((nrow, PAGE_SIZE), F32)
    for h in range(SB_HEADS):
        bias2 = jnp.where(rhead == h, bias_ref[h] * LOG2_E, bias2)
    q_bf = [q_ref[:, h * SB_HEAD_DIM:(h + 1) * SB_HEAD_DIM].astype(BF16) for h in range(SB_HEADS)]

    def tile(k_heads, v_heads, carry, acc, mask):
        z = jnp.concatenate([_dot_nt(q_bf[h], k_heads[h]) for h in range(SB_HEADS)], axis=0)
        log_b, n = _sb_logits(z * SB_LOGIT_SCALE + bias2, mask)
        a, carry = _sb_weights(log_b, n, suffix, carry, mask)
        acc = [acc[h] + _dot(a[h * lq:(h + 1) * lq, :], v_heads[h]) for h in range(SB_HEADS)]
        return carry, acc

    @pl.when(j == 0)
    def _():
        kpad_ref[...] = jnp.zeros_like(kpad_ref)
        vpad_ref[...] = jnp.zeros_like(vpad_ref)
        for h in range(SB_HEADS):
            hs = slice(h * SB_HEAD_DIM, (h + 1) * SB_HEAD_DIM)
            kpad_ref[h, 0:lq, :] = kn_ref[:, hs]
            vpad_ref[h, 0:lq, :] = vn_ref[:, hs]
        qidx = lax.broadcasted_iota(jnp.int32, (nrow, PAGE_SIZE), 0) % lq
        kidx = lax.broadcasted_iota(jnp.int32, (nrow, PAGE_SIZE), 1)
        carry, acc = tile([kpad_ref[h] for h in range(SB_HEADS)], [vpad_ref[h] for h in range(SB_HEADS)],
                          jnp.zeros((nrow, PAGE_SIZE), F32),
                          [jnp.zeros((lq, SB_HEAD_DIM), F32)] * SB_HEADS, kidx < qidx)
        carry_ref[...] = carry
        acc_ref[...] = jnp.concatenate(acc, axis=0)

    nchunk = (PAGE_SIZE * SB_HEADS) // LANES
    ntile = g * nchunk
    q_stack = jnp.concatenate(q_bf, axis=0)
    own = (lax.broadcasted_iota(jnp.int32, (nrow, LANES), 1) % SB_HEADS) == rhead
    tiles = []
    for p in range(g):
        z = _dot_nt(q_stack, k_refs[p][0])
        tiles += [z[:, c * LANES:(c + 1) * LANES] for c in reversed(range(nchunk))]
    z = jnp.concatenate(tiles, axis=0).reshape(ntile, nrow, LANES)
    log_b, n = _sb_logits(z * SB_LOGIT_SCALE + bias2[None], own[None])
    sums = jnp.dot(n.reshape(ntile * nrow, LANES), suffix, preferred_element_type=F32)
    sums = sums.reshape(ntile, nrow, 2 * LANES)
    carry = carry_ref[...]
    carries = []
    for t in range(ntile):
        carries.append(carry)
        carry = carry + sums[t, :, LANES:]
    carry_ref[...] = carry
    a = jnp.exp2(log_b - sums[:, :, :LANES] - jnp.stack(carries, axis=0))
    a = jnp.where(own[None], a, 0.0).astype(BF16)
    acc = acc_ref[...]
    for p in range(g):
        a_page = jnp.concatenate([a[p * nchunk + (nchunk - 1 - c)] for c in range(nchunk)], axis=1)
        acc = acc + jnp.dot(a_page, v_refs[p][0].astype(BF16), preferred_element_type=F32)
    acc_ref[...] = acc

    @pl.when(j == pl.num_programs(1) - 1)
    def _():
        for h in range(SB_HEADS):
            o_ref[:, h * SB_HEAD_DIM:(h + 1) * SB_HEAD_DIM] = acc[h * lq:(h + 1) * lq, :].astype(o_ref.dtype)


def _sb_paged(proj, bias, pool_k, pool_v, page_table, bsz, lq, pages_per_step):
    n_pages = page_table.shape[1]
    g = pages_per_step
    nsteps = n_pages // g
    qcol = (2 * GLA_QK + 2 * GLA_V) // SB_W

    def page_spec(p):
        return pl.BlockSpec((1, PAGE_SIZE * SB_HEADS, SB_HEAD_DIM),
                            lambda b, j, pt: (pt[b, n_pages - 1 - (j * g + p)], 0, 0))

    grid_spec = pltpu.PrefetchScalarGridSpec(
        num_scalar_prefetch=1,
        grid=(bsz, nsteps),
        in_specs=[pl.BlockSpec(memory_space=pltpu.SMEM),
                  pl.BlockSpec((lq, SB_W), lambda b, j, pt: (b, qcol)),
                  pl.BlockSpec((lq, SB_W), lambda b, j, pt: (b, qcol + 1)),
                  pl.BlockSpec((lq, SB_W), lambda b, j, pt: (b, qcol + 2))]
        + [page_spec(p) for p in range(g)] + [page_spec(p) for p in range(g)],
        out_specs=pl.BlockSpec((lq, SB_W), lambda b, j, pt: (b, 0)),
        scratch_shapes=[pltpu.VMEM((SB_HEADS * lq, PAGE_SIZE), F32),
                        pltpu.VMEM((SB_HEADS * lq, SB_HEAD_DIM), F32),
                        pltpu.VMEM((SB_HEADS, PAGE_SIZE, SB_HEAD_DIM), F32),
                        pltpu.VMEM((SB_HEADS, PAGE_SIZE, SB_HEAD_DIM), F32)])
    return pl.pallas_call(
        functools.partial(_sb_paged_kernel, pages_per_step=g),
        grid_spec=grid_spec,
        out_shape=jax.ShapeDtypeStruct((bsz * lq, SB_W), _act_dtype(lq)),
        compiler_params=_cparams(("parallel", "arbitrary")),
        name="sb_paged",
    )(page_table, bias, proj, proj, proj, *([pool_k] * g), *([pool_v] * g))


def _ssd_chunk(r, xc_ref, z_ref, dtr_ref, dtb_ref, dsk_ref, hs_ref, ypre_ref, a, causal, causal_bf,
               lo_lanes, lo_rows):
    c = r.stop - r.start
    hpg = SSD_HEADS // SSD_GROUPS
    xc = xc_ref[r, :]
    dt = _softplus(dtr_ref[r, :] + dtb_ref[...])
    cum = _sel_dot(causal_bf, dt * a)
    cum_t = _transpose_f32(cum)
    dt_t = _transpose_f32(dt)
    last = cum[c - 1:c, :]
    e_cum = jnp.exp(cum)
    dec = jnp.exp(last - cum) * dt
    e_last_t = jnp.exp(cum_t[:, c - 1:c])

    for g in range(SSD_GROUPS):
        b_g = xc[:, SSD_INNER + g * SSD_STATE:SSD_INNER + (g + 1) * SSD_STATE]
        c_g = xc[:, SSD_INNER + (SSD_GROUPS + g) * SSD_STATE:SSD_INNER + (SSD_GROUPS + g + 1) * SSD_STATE]
        cb = _dot_nt(c_g, b_g)
        for pp in range(hpg // 2):
            pair = g * (hpg // 2) + pp
            h0, h1 = 2 * pair, 2 * pair + 1
            x_pair = xc[:, pair * LANES:(pair + 1) * LANES]
            hs = hs_ref[pair]
            y_pair = _dot_nt(c_g, hs) * jnp.where(lo_lanes, e_cum[:, h0:h0 + 1], e_cum[:, h1:h1 + 1])
            x_bf = x_pair.astype(BF16)
            intra = []
            for hh in (h0, h1):
                lmat = jnp.exp(jnp.where(causal, cum[:, hh:hh + 1] - cum_t[hh:hh + 1, :], NEG_INF))
                w = lmat * dt_t[hh:hh + 1, :] * cb
                intra.append(jnp.dot(w.astype(BF16), x_bf, preferred_element_type=F32))
            y_pair = y_pair + jnp.where(lo_lanes, intra[0], intra[1])
            xd = x_pair * jnp.where(lo_lanes, dec[:, h0:h0 + 1], dec[:, h1:h1 + 1])
            scale = jnp.where(lo_rows, e_last_t[h0:h0 + 1, :], e_last_t[h1:h1 + 1, :])
            upd = _dot(xd.T, b_g) if c % LANES == 0 else _dot_tn(xd, b_g)
            hs_ref[pair] = scale * hs + upd
            y_pair = y_pair + x_pair * dsk_ref[:, pair * LANES:(pair + 1) * LANES]
            zz = z_ref[r, pair * LANES:(pair + 1) * LANES]
            ypre_ref[r, pair * LANES:(pair + 1) * LANES] = y_pair * _silu(zz)


def _ssd_kernel(xbc_ref, z_ref, dtr_ref, cw_ref, cb_ref, dtb_ref, alog_ref, dsk_ref, ng_ref,
                conv0_ref, h0_ref, y_ref, hout_ref, xext_ref, hs_ref, ypre_ref, xc_ref, *, chunk):
    ci = pl.program_id(1)
    rows = xbc_ref.shape[0]
    c = chunk
    tail = SSD_CONV - 1
    pad = 8

    @pl.when(ci == 0)
    def _():
        xext_ref[0:pad, :] = conv0_ref[0]
        hs_ref[...] = h0_ref[0]

    xext_ref[pad:pad + rows, :] = xbc_ref[...]
    xfull = xext_ref[...]
    acc = cb_ref[...] + xfull[pad:, :] * cw_ref[tail:tail + 1, :]
    for w in range(tail):
        acc = acc + pltpu.roll(xfull, tail - w, axis=0)[pad:, :] * cw_ref[w:w + 1, :]
    xc_ref[...] = _silu(acc)
    xext_ref[0:pad, :] = xext_ref[rows:rows + pad, :]

    a = -jnp.exp(alog_ref[...])
    row = lax.broadcasted_iota(jnp.int32, (c, c), 0)
    col = lax.broadcasted_iota(jnp.int32, (c, c), 1)
    causal = col <= row
    causal_bf = jnp.where(causal, 1.0, 0.0).astype(BF16)
    lo_lanes = lax.broadcasted_iota(jnp.int32, (c, LANES), 1) < SSD_HEAD_DIM
    lo_rows = lax.broadcasted_iota(jnp.int32, (LANES, LANES), 0) < SSD_HEAD_DIM
    for sc in range(rows // c):
        _ssd_chunk(slice(sc * c, (sc + 1) * c), xc_ref, z_ref, dtr_ref, dtb_ref, dsk_ref, hs_ref, ypre_ref,
                   a, causal, causal_bf, lo_lanes, lo_rows)
    gw = SSD_INNER // SSD_GROUPS
    for g in range(SSD_GROUPS):
        gs = slice(g * gw, (g + 1) * gw)
        y_ref[:, gs] = (_rms_scale(ypre_ref[:, gs]) * ng_ref[:, gs]).astype(y_ref.dtype)

    @pl.when(ci == pl.num_programs(1) - 1)
    def _():
        hout_ref[0] = hs_ref[...]


def _ssd(zx, side, cw, cb, dtb, alog, dsk, ng, conv0, h0, bsz, length, chunk, rows):
    nc = length // rows
    m = bsz * length
    row = lambda b, c: b * nc + c
    const = lambda b, c: (0, 0)
    npair = SSD_HEADS // 2
    return pl.pallas_call(
        functools.partial(_ssd_kernel, chunk=chunk),
        grid=(bsz, nc),
        in_specs=[pl.BlockSpec((rows, SSD_CONV_DIM), lambda b, c: (row(b, c), 0)),
                  pl.BlockSpec((rows, SSD_INNER), lambda b, c: (row(b, c), SSD_CONV_DIM // SSD_INNER)),
                  pl.BlockSpec((rows, SIDE_W), lambda b, c: (row(b, c), 0)),
                  pl.BlockSpec((SSD_CONV, SSD_CONV_DIM), const),
                  pl.BlockSpec((1, SSD_CONV_DIM), const),
                  pl.BlockSpec((1, SIDE_W), const),
                  pl.BlockSpec((1, SIDE_W), const),
                  pl.BlockSpec((1, SSD_INNER), const),
                  pl.BlockSpec((1, SSD_INNER), const),
                  pl.BlockSpec((1, 8, SSD_CONV_DIM), lambda b, c: (b, 0, 0)),
                  pl.BlockSpec((1, npair, LANES, SSD_STATE), lambda b, c: (b, 0, 0, 0))],
        out_specs=[pl.BlockSpec((rows, SSD_INNER), lambda b, c: (row(b, c), 0)),
                   pl.BlockSpec((1, npair, LANES, SSD_STATE), lambda b, c: (b, 0, 0, 0))],
        out_shape=[jax.ShapeDtypeStruct((m, SSD_INNER), _act_dtype(chunk)),
                   jax.ShapeDtypeStruct((bsz, npair, LANES, SSD_STATE), F32)],
        scratch_shapes=[pltpu.VMEM((rows + 8, SSD_CONV_DIM), F32),
                        pltpu.VMEM((npair, LANES, SSD_STATE), F32),
                        pltpu.VMEM((rows, SSD_INNER), F32),
                        pltpu.VMEM((rows, SSD_CONV_DIM), F32)],
        compiler_params=_cparams(("parallel", "arbitrary")),
        name="ssd",
    )(zx, zx, side, cw, cb, dtb, alog, dsk, ng, conv0, h0)


def _pad_cols(w, width):
    return jnp.pad(w, ((0, 0), (0, width - w.shape[1])))


def _row(v, width=None):
    v = v.reshape(1, -1).astype(F32)
    return v if width is None else _pad_cols(v, width)


def _prep_params(p):
    w_in = p["w_in_even"][0]
    n_gla = 2 * GLA_QK + GLA_V
    q = {}
    q["even_main"] = jnp.concatenate([w_in[:, :n_gla], w_in[:, n_gla + GLA_GATE_RANK:]], axis=1).astype(BF16)
    q["even_side"] = _pad_cols(w_in[:, n_gla:n_gla + GLA_GATE_RANK], SIDE_W).astype(BF16)
    q["gate_w2"] = jnp.pad(p["gla_w_gate2"][0], ((0, SIDE_W - GLA_GATE_RANK), (0, 0))).astype(BF16)
    q["gate_b"] = _row(p["gla_b_gate"][0])
    q["gla_g"] = _row(p["gla_norm_g"][0])
    q["sb_bias"] = p["sb_logit_bias"][0].astype(F32)
    q["even_out"] = p["w_out_even"][0].astype(BF16)
    w_ssd = p["ssd_w_in"][0]
    q["ssd_main"] = jnp.concatenate([w_ssd[:, SSD_INNER:SSD_INNER + SSD_CONV_DIM], w_ssd[:, :SSD_INNER]],
                                    axis=1).astype(BF16)
    q["ssd_side"] = _pad_cols(w_ssd[:, SSD_INNER + SSD_CONV_DIM:], SIDE_W).astype(BF16)
    q["conv_w"] = p["ssd_conv_w"][0].astype(F32)
    q["conv_b"] = _row(p["ssd_conv_b"][0])
    q["dt_bias"] = _row(p["ssd_dt_bias"][0], SIDE_W)
    q["a_log"] = _row(p["ssd_a_log"][0], SIDE_W)
    q["d_skip"] = _row(jnp.repeat(p["ssd_d"][0], SSD_HEAD_DIM))
    q["ssd_g"] = _row(p["ssd_norm_g"][0])
    q["ssd_out"] = p["ssd_w_out"][0].astype(BF16)
    for name in ("norm_mix_pre", "norm_mix_post", "norm_mlp_pre", "norm_mlp_post"):
        q[name] = [_row(p[name][li]) for li in range(2)]
    q["mlp_up"] = [p["mlp_w_up"][li].astype(BF16) for li in range(2)]
    q["mlp_down"] = [p["mlp_w_down"][li].astype(BF16) for li in range(2)]
    return q


def _trunk(x, paged, gla_s0, ssm_h0, conv0, q, cfg):
    bsz, length, d = x.shape
    m = bsz * length
    x2 = x.reshape(m, d)

    kcol = 2 * GLA_QK + 2 * GLA_V + SB_W
    proj, side, sb_k, sb_v = _norm_matmul(x2, q["norm_mix_pre"][0], q["even_main"], q["even_side"],
                                          cfg["tm_mm"], cfg["tn_even"], head_major=(kcol, kcol + SB_W))
    o_gla, gla_new = _gla(proj, side, q["gate_w2"], q["gate_b"], q["gla_g"], gla_s0,
                          bsz, length, cfg["gla_rows"], cfg["gla_sub"])
    if paged is None:
        o_sb = _sb_prompt(proj, q["sb_bias"], bsz, length, cfg["sb_tq"], cfg["sb_tk"], cfg["sb_split"],
                          cfg["sb_unroll"])
    else:
        o_sb = _sb_paged(proj, q["sb_bias"], paged[0], paged[1], paged[2], bsz, length, cfg["sb_pages"])
    sb_k = sb_k.reshape(bsz, length, SB_HEADS, SB_HEAD_DIM)
    sb_v = sb_v.reshape(bsz, length, SB_HEADS, SB_HEAD_DIM)
    x2 = _out_proj([o_gla, o_sb], q["even_out"], x2, q["norm_mix_post"][0], cfg["tm_mm"])
    x2 = _mlp(x2, q["norm_mlp_pre"][0], q["mlp_up"][0], q["mlp_down"][0], q["norm_mlp_post"][0],
              cfg["tm_mm"], 1024)

    zx, dtr = _norm_matmul(x2, q["norm_mix_pre"][1], q["ssd_main"], q["ssd_side"], cfg["tm_mm"], cfg["tn_ssd"])
    conv_pad = jnp.pad(conv0, ((0, 0), (8 - (SSD_CONV - 1), 0), (0, 0)))
    h0 = ssm_h0.reshape(bsz, SSD_HEADS // 2, 2 * SSD_HEAD_DIM, SSD_STATE)
    y, h_new = _ssd(zx, dtr, q["conv_w"], q["conv_b"], q["dt_bias"], q["a_log"], q["d_skip"], q["ssd_g"],
                    conv_pad, h0, bsz, length, cfg["ssd_chunk"], cfg["ssd_rows"])
    conv_new = zx.reshape(bsz, length, SSD_MAIN)[:, length - (SSD_CONV - 1):, :SSD_CONV_DIM]
    x2 = _out_proj([y], q["ssd_out"], x2, q["norm_mix_post"][1], cfg["tm_mm"])
    x2 = _mlp(x2, q["norm_mlp_pre"][1], q["mlp_up"][1], q["mlp_down"][1], q["norm_mlp_post"][1],
              cfg["tm_mm"], 1024)

    return (x2.reshape(bsz, length, d), sb_k[None], sb_v[None], gla_new[None],
            h_new.reshape(bsz, SSD_HEADS, SSD_HEAD_DIM, SSD_STATE)[None], conv_new[None])


def _prompt_cfg(length):
    return dict(tn_even=EVEN_MAIN // 2, tn_ssd=SSD_MAIN // 3, tm_mm=min(1024, 2 * length),
                gla_rows=min(256, length), gla_sub=16,
                sb_tq=min(512, length), sb_tk=128, sb_split=2, sb_unroll=2,
                ssd_chunk=min(128, length), ssd_rows=min(256, length))


def _sample_cfg(bsz, length, n_pages):
    return dict(tn_even=EVEN_MAIN // 2, tn_ssd=SSD_MAIN // 3, tm_mm=bsz * length, gla_rows=length, gla_sub=length, sb_pages=min(32, n_pages),
                ssd_chunk=length, ssd_rows=length)


def kernel(x_prompt, x_sample, cache_sb_k, cache_sb_v, state_gla, state_ssm, state_conv, page_table, w_in_even, gla_w_gate2, gla_b_gate, gla_norm_g, sb_logit_bias, w_out_even, ssd_w_in, ssd_conv_w, ssd_conv_b, ssd_dt_bias, ssd_a_log, ssd_d, ssd_norm_g, ssd_w_out, norm_mix_pre, norm_mix_post, norm_mlp_pre, norm_mlp_post, mlp_w_up, mlp_w_down):
    q = _prep_params(dict(
        w_in_even=w_in_even, gla_w_gate2=gla_w_gate2, gla_b_gate=gla_b_gate, gla_norm_g=gla_norm_g,
        sb_logit_bias=sb_logit_bias, w_out_even=w_out_even, ssd_w_in=ssd_w_in, ssd_conv_w=ssd_conv_w,
        ssd_conv_b=ssd_conv_b, ssd_dt_bias=ssd_dt_bias, ssd_a_log=ssd_a_log, ssd_d=ssd_d,
        ssd_norm_g=ssd_norm_g, ssd_w_out=ssd_w_out, norm_mix_pre=norm_mix_pre, norm_mix_post=norm_mix_post,
        norm_mlp_pre=norm_mlp_pre, norm_mlp_post=norm_mlp_post, mlp_w_up=mlp_w_up, mlp_w_down=mlp_w_down))
    bp, lp, _ = x_prompt.shape
    bs, ls, _ = x_sample.shape
    zeros = lambda *s: jnp.zeros(s, F32)
    prompt = _trunk(x_prompt, None, zeros(bp, GLA_HEADS, GLA_DK, GLA_DV),
                    zeros(bp, SSD_HEADS, SSD_HEAD_DIM, SSD_STATE), zeros(bp, SSD_CONV - 1, SSD_CONV_DIM),
                    q, _prompt_cfg(lp))
    n_pool = cache_sb_k.shape[1]
    pool_k = cache_sb_k.reshape(n_pool, PAGE_SIZE * SB_HEADS, SB_HEAD_DIM)
    pool_v = cache_sb_v.reshape(n_pool, PAGE_SIZE * SB_HEADS, SB_HEAD_DIM)
    sample = _trunk(x_sample, (pool_k, pool_v, page_table), state_gla[0], state_ssm[0], state_conv[0],
                    q, _sample_cfg(bs, ls, page_table.shape[1]))
    return (prompt[0], sample[0]) + prompt[1:] + sample[1:]
```
